```python
import math
import jax, jax.numpy as jnp
from jax import lax
import numpy as np

D_MODEL = 1024
BATCH = 4
SEQ = 4096
DEPTH = 2
DEC_BATCH = 32
DEC_SEQ = 4
PAST_LEN = 8192
PAGE_SIZE = 128

D_CONV = D_MODEL // 2
CONV_WIDTH = 31
HD_B = 64
H_B = (D_MODEL // 2) // HD_B
D_B = H_B * HD_B
MOBA_BLOCK = 256
MOBA_TOPK = 3
Q_CHUNK = 32
N_BUCKETS = 32
MAX_DISTANCE = 128
H_C = 4
DK_C = (D_MODEL // 4) // H_C
DV_C = (D_MODEL // 2) // H_C
GLA_RANK = 16
GATE_NORM = 16.0
GLA_CHUNK = 16
D_FF = 4 * D_MODEL
EPS = 1e-6
IN_SPLITS = (2 * D_CONV, D_B, D_B, D_B, H_C * DK_C, H_C * DK_C, H_C * DV_C, H_C * DV_C, GLA_RANK, D_MODEL, D_MODEL, D_MODEL)
N_IN = sum(IN_SPLITS)

kernel_name = 'hybrid_conv_moba_gla_adaln_step'


def _split(z, sizes):
    idx = np.cumsum(sizes)[:-1].tolist()
    return jnp.split(z, idx, axis=-1)


def _rms(x, g):
    xf = x.astype(jnp.float32)
    y = xf * lax.rsqrt(jnp.mean(xf * xf, axis=-1, keepdims=True) + EPS)
    return y.astype(x.dtype) * g


def _layernorm(x, g, b):
    xf = x.astype(jnp.float32)
    mu = jnp.mean(xf, axis=-1, keepdims=True)
    var = jnp.mean(jnp.square(xf - mu), axis=-1, keepdims=True)
    return ((xf - mu) * lax.rsqrt(var + EPS)).astype(x.dtype) * g + b


def _t5_bucket(dist):
    max_exact = N_BUCKETS // 2
    d = jnp.maximum(dist, 1).astype(jnp.float32)
    large = max_exact + (jnp.log(d / max_exact) / math.log(MAX_DISTANCE / max_exact) * (N_BUCKETS - max_exact)).astype(jnp.int32)
    large = jnp.minimum(large, N_BUCKETS - 1)
    return jnp.where(dist < max_exact, dist, large)


def _conv_branch(u_glu, conv_buf, w_dw, b_dw, ln_g, ln_b, w_pw2):
    a, g = jnp.split(u_glu, 2, axis=-1)
    u = a * jax.nn.sigmoid(g)
    ext = jnp.concatenate([conv_buf.astype(u.dtype), u], axis=1)
    y = lax.conv_general_dilated(ext, w_dw[:, None, :], window_strides=(1,), padding='VALID',
                                 dimension_numbers=('NWC', 'WIO', 'NWC'), feature_group_count=D_CONV)
    y = jax.nn.silu(_layernorm(y + b_dw, ln_g, ln_b))
    return y @ w_pw2, ext[:, -(CONV_WIDTH - 1):]


def _moba_attention(q, k_all, v_all, q_pos, rel_bias):
    B, Tq, H, hd = q.shape
    L = k_all.shape[1]
    nb = -(-L // MOBA_BLOCK)
    pad = nb * MOBA_BLOCK - L
    def to_blocks(t):
        t = jnp.pad(t, ((0, 0), (0, pad), (0, 0), (0, 0)))
        return t.reshape(B, nb, MOBA_BLOCK, H, hd).transpose(0, 3, 1, 2, 4)
    kb = to_blocks(k_all)
    vb = to_blocks(v_all)
    kmean = jnp.mean(kb.astype(jnp.float32), axis=3)
    topk = min(MOBA_TOPK, nb)
    qc = Q_CHUNK if Tq % Q_CHUNK == 0 else Tq
    nq = Tq // qc
    qs = (q * (hd ** -0.5)).transpose(0, 2, 1, 3).reshape(B, H, nq, qc, hd).transpose(2, 0, 1, 3, 4)
    ps = q_pos.reshape(nq, qc)
    bi = jnp.arange(B)[:, None, None, None]
    hi = jnp.arange(H)[None, :, None, None]
    bias_hb = rel_bias.T
    offs = jnp.arange(MOBA_BLOCK, dtype=jnp.int32)

    def one_chunk(args):
        qq, pp = args
        own = pp // MOBA_BLOCK
        s = jnp.einsum('bhqd,bhnd->bhqn', qq.astype(jnp.float32), kmean)
        s = jnp.where(jnp.arange(nb)[None, None, None, :] < own[None, None, :, None], s, -jnp.inf)
        _, sel = lax.top_k(s, topk)
        idx = jnp.concatenate([sel.astype(jnp.int32), jnp.broadcast_to(own[None, None, :, None], (B, H, qc, 1))], axis=-1)
        kg = kb[bi, hi, idx]
        vg = vb[bi, hi, idx]
        logits = jnp.einsum('bhqd,bhqnkd->bhqnk', qq, kg).astype(jnp.float32)
        kpos = idx[..., None] * MOBA_BLOCK + offs
        slot_ok = jnp.concatenate([jnp.arange(topk)[None, :] < jnp.minimum(own, topk)[:, None],
                                   jnp.ones((qc, 1), dtype=bool)], axis=1)
        ok = slot_ok[None, None, :, :, None] & (kpos <= pp[None, None, :, None, None])
        dist = jnp.maximum(pp[None, None, :, None, None] - kpos, 0)
        bias = bias_hb[hi[..., None], _t5_bucket(dist)].astype(jnp.float32)
        logits = jnp.where(ok, logits + bias, -jnp.inf)
        p = jax.nn.softmax(logits.reshape(B, H, qc, -1), axis=-1).reshape(logits.shape)
        return jnp.einsum('bhqnk,bhqnkd->bhqd', p.astype(vg.dtype), vg)

    outs = lax.map(one_chunk, (qs, ps))
    return outs.transpose(1, 0, 3, 2, 4).reshape(B, Tq, H, hd)


def _gla(q, k, v, log_a, S0):
    B, T, H, dk = q.shape
    dv = v.shape[-1]
    C = GLA_CHUNK if T % GLA_CHUNK == 0 else T
    n = T // C
    def blk(t):
        return t.astype(jnp.float32).reshape(B, n, C, H, -1).transpose(0, 3, 1, 2, 4)
    qf, kf, vf, la = blk(q * (dk ** -0.5)), blk(k), blk(v), blk(log_a)
    bc = jnp.cumsum(la, axis=3)
    causal = jnp.tril(jnp.ones((C, C), dtype=bool))
    diff = bc[:, :, :, :, None, :] - bc[:, :, :, None, :, :]
    decay = jnp.exp(jnp.where(causal[:, :, None], diff, -jnp.inf))
    att = jnp.einsum('bhntd,bhnsd,bhntsd->bhnts', qf, kf, decay)
    o_intra = jnp.einsum('bhnts,bhnsv->bhntv', att, vf)
    b_last = bc[:, :, :, -1:, :]
    q_dec = qf * jnp.exp(bc)
    k_dec = kf * jnp.exp(b_last - bc)
    a_chunk = jnp.exp(b_last[:, :, :, 0, :])

    def step(S, inp):
        qd, kd, vv, ac = inp
        o = jnp.einsum('bhtd,bhdv->bhtv', qd, S)
        S = ac[..., None] * S + jnp.einsum('bhsd,bhsv->bhdv', kd, vv)
        return S, o

    mv = lambda t: jnp.moveaxis(t, 2, 0)
    S_fin, o_inter = lax.scan(step, S0.astype(jnp.float32), (mv(q_dec), mv(k_dec), mv(vf), mv(a_chunk)))
    o = o_intra + jnp.moveaxis(o_inter, 0, 2)
    return o.transpose(0, 2, 3, 1, 4).reshape(B, T, H, dv), S_fin


def _decoder_layer(x, c, l, conv_buf, k_past, v_past, S0, P, rel_bias):
    B, T, _ = x.shape
    mod = jax.nn.silu(c) @ P['w_ada'][l] + P['b_ada'][l]
    sh1, sc1, gt1, sh2, sc2, gt2 = jnp.split(mod[:, None, :], 6, axis=-1)
    h = _rms(x, P['g_norm1'][l]) * (1 + sc1) + sh1
    z = h @ P['w_in'][l]
    (u_glu, q_b, k_b, v_b, q_c, k_c, v_c, r_c, lr_c, m_a, m_b, m_c) = _split(z, IN_SPLITS)

    y_a, conv_new = _conv_branch(u_glu, conv_buf, P['w_dw'][l], P['b_dw'][l], P['ln_g'][l], P['ln_b'][l], P['w_pw2'][l])

    qh = q_b.reshape(B, T, H_B, HD_B)
    kh = k_b.reshape(B, T, H_B, HD_B)
    vh = v_b.reshape(B, T, H_B, HD_B)
    if k_past is None:
        past_len = 0
        k_all, v_all = kh, vh
    else:
        past_len = k_past.shape[1]
        k_all = jnp.concatenate([k_past.astype(kh.dtype), kh], axis=1)
        v_all = jnp.concatenate([v_past.astype(vh.dtype), vh], axis=1)
    q_pos = past_len + jnp.arange(T, dtype=jnp.int32)
    o_b = _moba_attention(qh, k_all, v_all, q_pos, rel_bias).reshape(B, T, D_B)
    y_b = o_b @ P['w_pb'][l]

    log_a = jax.nn.log_sigmoid((lr_c @ P['w_a2'][l] + P['b_a'][l]).astype(jnp.float32)) / GATE_NORM
    o_c, S_new = _gla(q_c.reshape(B, T, H_C, DK_C), k_c.reshape(B, T, H_C, DK_C),
                      v_c.reshape(B, T, H_C, DV_C), log_a.reshape(B, T, H_C, DK_C), S0)
    o_c = o_c * lax.rsqrt(jnp.mean(o_c * o_c, axis=-1, keepdims=True) + EPS)
    o_c = o_c * P['g_gla'][l].astype(jnp.float32).reshape(H_C, DV_C) * jax.nn.silu(r_c.astype(jnp.float32)).reshape(B, T, H_C, DV_C)
    y_c = o_c.reshape(B, T, H_C * DV_C).astype(x.dtype) @ P['w_pc'][l]

    m = jax.nn.sigmoid(m_a) * y_a + jax.nn.sigmoid(m_b) * y_b + jax.nn.sigmoid(m_c) * y_c
    x = x + gt1 * (m @ P['w_o'][l])

    h2 = _rms(x, P['g_norm2'][l]) * (1 + sc2) + sh2
    x = x + gt2 * (jnp.square(jax.nn.relu(h2 @ P['w_up'][l])) @ P['w_down'][l])
    return x, kh, vh, conv_new, S_new.astype(x.dtype)


def setup_inputs(seed: int = 0) -> dict:
    key = jax.random.key(seed)
    ks = jax.random.split(key, 40)
    n_pages = PAST_LEN // PAGE_SIZE
    n_used = DEC_BATCH * n_pages
    n_pool = n_used + max(1, n_used // 4)
    f32 = jnp.float32
    def nrm(k, shape, s):
        return jax.random.normal(k, shape, f32) * s
    page_table = jax.random.permutation(ks[6], n_pool)[:n_used].reshape(DEC_BATCH, n_pages).astype(jnp.int32)
    return {
        'x_prompt': nrm(ks[0], (BATCH, SEQ, D_MODEL), 1.0),
        'x_sample': nrm(ks[1], (DEC_BATCH, DEC_SEQ, D_MODEL), 1.0),
        'c_prompt': nrm(ks[2], (BATCH, D_MODEL), 1.0),
        'c_sample': nrm(ks[3], (DEC_BATCH, D_MODEL), 1.0),
        'cache_k': nrm(ks[4], (DEPTH, n_pool, PAGE_SIZE, H_B, HD_B), 1.0),
        'cache_v': nrm(ks[5], (DEPTH, n_pool, PAGE_SIZE, H_B, HD_B), 1.0),
        'page_table': page_table,
        'state_conv': nrm(ks[7], (DEPTH, DEC_BATCH, CONV_WIDTH - 1, D_CONV), 0.5),
        'state_gla': nrm(ks[8], (DEPTH, DEC_BATCH, H_C, DK_C, DV_C), 0.5),
        'w_ada': nrm(ks[9], (DEPTH, D_MODEL, 6 * D_MODEL), D_MODEL ** -0.5),
        'b_ada': nrm(ks[10], (DEPTH, 6 * D_MODEL), 0.01),
        'g_norm1': 1.0 + nrm(ks[11], (DEPTH, D_MODEL), 0.02),
        'w_in': nrm(ks[12], (DEPTH, D_MODEL, N_IN), D_MODEL ** -0.5),
        'w_dw': nrm(ks[13], (DEPTH, CONV_WIDTH, D_CONV), CONV_WIDTH ** -0.5),
        'b_dw': nrm(ks[14], (DEPTH, D_CONV), 0.01),
        'ln_g': 1.0 + nrm(ks[15], (DEPTH, D_CONV), 0.02),
        'ln_b': nrm(ks[16], (DEPTH, D_CONV), 0.01),
        'w_pw2': nrm(ks[17], (DEPTH, D_CONV, D_MODEL), D_CONV ** -0.5),
        'w_pb': nrm(ks[18], (DEPTH, D_B, D_MODEL), D_B ** -0.5),
        'rel_bias': nrm(ks[19], (N_BUCKETS, H_B), 0.5),
        'w_a2': nrm(ks[20], (DEPTH, GLA_RANK, H_C * DK_C), GLA_RANK ** -0.5),
        'b_a': nrm(ks[21], (DEPTH, H_C * DK_C), 0.1),
        'g_gla': 1.0 + nrm(ks[22], (DEPTH, H_C * DV_C), 0.02),
        'w_pc': nrm(ks[23], (DEPTH, H_C * DV_C, D_MODEL), (H_C * DV_C) ** -0.5),
        'w_o': nrm(ks[24], (DEPTH, D_MODEL, D_MODEL), D_MODEL ** -0.5),
        'g_norm2': 1.0 + nrm(ks[25], (DEPTH, D_MODEL), 0.02),
        'w_up': nrm(ks[26], (DEPTH, D_MODEL, D_FF), D_MODEL ** -0.5),
        'w_down': nrm(ks[27], (DEPTH, D_FF, D_MODEL), D_FF ** -0.5),
        'g_final': 1.0 + nrm(ks[28], (D_MODEL,), 0.02),
    }


def reference(x_prompt, x_sample, c_prompt, c_sample, cache_k, cache_v, page_table, state_conv, state_gla,
              w_ada, b_ada, g_norm1, w_in, w_dw, b_dw, ln_g, ln_b, w_pw2, w_pb, rel_bias, w_a2, b_a, g_gla,
              w_pc, w_o, g_norm2, w_up, w_down, g_final):
    P = {'w_ada': w_ada, 'b_ada': b_ada, 'g_norm1': g_norm1, 'w_in': w_in, 'w_dw': w_dw, 'b_dw': b_dw,
         'ln_g': ln_g, 'ln_b': ln_b, 'w_pw2': w_pw2, 'w_pb': w_pb, 'w_a2': w_a2, 'b_a': b_a, 'g_gla': g_gla,
         'w_pc': w_pc, 'w_o': w_o, 'g_norm2': g_norm2, 'w_up': w_up, 'w_down': w_down}
    nb_p = x_prompt.shape[0]
    nb_s = x_sample.shape[0]
    past_len = page_table.shape[1] * cache_k.shape[2]
    xp, xs = x_prompt, x_sample
    kp_l, vp_l, cp_l, sp_l = [], [], [], []
    ks_l, vs_l, cs_l, ss_l = [], [], [], []
    for l in range(DEPTH):
        conv0 = jnp.zeros((nb_p, CONV_WIDTH - 1, D_CONV), xp.dtype)
        s0 = jnp.zeros((nb_p, H_C, DK_C, DV_C), jnp.float32)
        xp, kp, vp, cp, sp = _decoder_layer(xp, c_prompt, l, conv0, None, None, s0, P, rel_bias)
        k_past = cache_k[l][page_table].reshape(nb_s, past_len, H_B, HD_B)
        v_past = cache_v[l][page_table].reshape(nb_s, past_len, H_B, HD_B)
        xs, kn, vn, cn, sn = _decoder_layer(xs, c_sample, l, state_conv[l], k_past, v_past, state_gla[l], P, rel_bias)
        kp_l.append(kp); vp_l.append(vp); cp_l.append(cp); sp_l.append(sp)
        ks_l.append(kn); vs_l.append(vn); cs_l.append(cn); ss_l.append(sn)
    y_prompt = _rms(xp, g_final)
    y_sample = _rms(xs, g_final)
    return (y_prompt, y_sample, jnp.stack(kp_l), jnp.stack(vp_l), jnp.stack(cp_l), jnp.stack(sp_l),
            jnp.stack(ks_l), jnp.stack(vs_l), jnp.stack(cs_l), jnp.stack(ss_l))
```

```python
import functools
import math

import numpy as np
import jax
import jax.numpy as jnp
from jax import lax
from jax.experimental import pallas as pl
from jax.experimental.pallas import tpu as pltpu

F32 = jnp.float32
BF16 = jnp.bfloat16

MOBA_BLOCK = 256
MOBA_TOPK = 3
MAX_DISTANCE = 128
GLA_HEADS = 4
GLA_CHUNK = 16
GATE_NORM = 16.0
EPS = 1e-6

LANES = 128
VMEM_LIMIT = 56 * 1024 * 1024

NEG = -1e30
TOKEN_TILE = 256
PAGES_PER_STEP = 8
CONV_HALO = 32

_NT = (((1,), (1,)), ((), ()))


def _dot(a, b):
    return jnp.dot(a, b, preferred_element_type=F32)


def _dot_nt(a, b):
    return lax.dot_general(a, b, _NT, preferred_element_type=F32)


def _dot_f32(a, b):
    return jnp.dot(a, b, precision=lax.Precision.HIGHEST, preferred_element_type=F32)


def _dot_nt_f32(a, b):
    return lax.dot_general(a, b, _NT, precision=lax.Precision.HIGHEST, preferred_element_type=F32)


def _params(*sem):
    return pltpu.CompilerParams(dimension_semantics=sem, vmem_limit_bytes=VMEM_LIMIT)


def _resident(shape, index_map):
    return pl.BlockSpec(shape, index_map, pipeline_mode=pl.Buffered(1))


def _rms_rows(x, g):
    return (x * lax.rsqrt(jnp.mean(x * x, axis=-1, keepdims=True) + EPS)) * g


def _log_sigmoid(x):
    return jnp.minimum(x, 0.0) - jnp.log1p(jnp.exp(-jnp.abs(x)))


def _top_k_mask(s, idx, axis, k, sentinel):
    sel = jnp.zeros(s.shape, dtype=jnp.bool_)
    for _ in range(k):
        mx = jnp.max(s, axis=axis, keepdims=True)
        first = jnp.min(jnp.where(s == mx, idx, sentinel), axis=axis, keepdims=True)
        pick = (idx == first) & (mx > -jnp.inf)
        sel = sel | pick
        s = jnp.where(pick, -jnp.inf, s)
    return sel


def _adaln_kernel(c_ref, w_ref, b_ref, o_ref):
    c = c_ref[...]
    s = c * jax.nn.sigmoid(c)
    o_ref[0] = _dot(s.astype(BF16), w_ref[0].astype(BF16)) + b_ref[0]


def _adaln(c_all, w_ada, b_ada):
    depth, d, n = w_ada.shape
    rows = c_all.shape[0]
    tn = 1536
    return pl.pallas_call(
        _adaln_kernel,
        grid=(depth, n // tn),
        in_specs=[
            pl.BlockSpec((rows, d), lambda l, j: (0, 0)),
            pl.BlockSpec((1, d, tn), lambda l, j: (l, 0, j)),
            pl.BlockSpec((1, 1, tn), lambda l, j: (l, 0, j)),
        ],
        out_specs=pl.BlockSpec((1, rows, tn), lambda l, j: (l, 0, j)),
        out_shape=jax.ShapeDtypeStruct((depth, rows, n), F32),
        compiler_params=_params("arbitrary", "arbitrary"),
        name="adaln",
    )(c_all, w_ada, b_ada.reshape(depth, 1, n))


def _inproj_kernel(x_ref, sc_ref, sh_ref, g_ref, w_ref, *outs, d_conv, d_b, hp, d_model,
                   q_scale, gq_scale, attn_aux):
    if attn_aux:
        (u_ref, q_ref, k_ref, v_ref, k16_ref, vt_ref, km_ref,
         gq_ref, gk_ref, gv_ref, gr_ref, lr_ref, gate_ref) = outs
    else:
        (u_ref, q_ref, k_ref, v_ref, gq_ref, gk_ref, gv_ref, gr_ref, lr_ref, gate_ref) = outs
    x = x_ref[...]
    h = _rms_rows(x, g_ref[0]) * (1.0 + sc_ref[0]) + sh_ref[0]
    hb = h.astype(BF16)

    pos = [0]

    def seg(width):
        lo = pos[0]
        pos[0] = lo + width
        return _dot(hb, w_ref[0, :, lo:lo + width])

    a = seg(d_conv)
    g = seg(d_conv)
    u_ref[...] = a * jax.nn.sigmoid(g)
    q_ref[...] = seg(d_b) * q_scale
    k = seg(d_b)
    k_ref[...] = k
    v = seg(d_b)
    v_ref[...] = v
    if attn_aux:
        k16_ref[0] = k.astype(BF16)
        vt_ref[0] = v.T.astype(BF16)
        km_ref[0] = jnp.mean(k, axis=0, keepdims=True)
    gq_ref[...] = seg(hp) * gq_scale
    gk_ref[...] = seg(hp)
    gv_ref[...] = seg(hp)
    gr_ref[...] = seg(hp)
    lr_ref[...] = seg(LANES)
    for i in range(3):
        gate_ref[:, i * d_model:(i + 1) * d_model] = jax.nn.sigmoid(seg(d_model))


def _inproj(x, sc, sh, g1, w_cat, layer, *, tm, tiles_per_group, dims, attn_aux):
    n, d = x.shape
    d_conv, d_b, hp, hd_b, dk_c = dims
    nw = w_cat.shape[-1]
    r = sc.shape[1]
    grid = (n // tm,)
    row = lambda i: (i, 0)
    grp = lambda i: (i // tiles_per_group, 0, 0)
    lay = lambda i: (layer, 0, 0)
    outs = [((n, d_conv), F32), ((n, d_b), F32), ((n, d_b), F32), ((n, d_b), F32)]
    specs = [pl.BlockSpec((tm, d_conv), row)] + [pl.BlockSpec((tm, d_b), row)] * 3
    if attn_aux:
        assert tm == MOBA_BLOCK
        nblk = n // MOBA_BLOCK
        outs += [((nblk, MOBA_BLOCK, d_b), BF16), ((nblk, d_b, MOBA_BLOCK), BF16), ((nblk, 1, d_b), F32)]
        specs += [pl.BlockSpec((1, MOBA_BLOCK, d_b), lambda i: (i, 0, 0)),
                  pl.BlockSpec((1, d_b, MOBA_BLOCK), lambda i: (i, 0, 0)),
                  pl.BlockSpec((1, 1, d_b), lambda i: (i, 0, 0))]
    outs += [((n, hp), F32)] * 4 + [((n, LANES), F32), ((n, 3 * d), F32)]
    specs += [pl.BlockSpec((tm, hp), row)] * 4 + [pl.BlockSpec((tm, LANES), row), pl.BlockSpec((tm, 3 * d), row)]
    body = functools.partial(_inproj_kernel, d_conv=d_conv, d_b=d_b, hp=hp, d_model=d,
                             q_scale=hd_b ** -0.5, gq_scale=dk_c ** -0.5, attn_aux=attn_aux)
    return pl.pallas_call(
        body,
        grid=grid,
        in_specs=[
            pl.BlockSpec((tm, d), row),
            pl.BlockSpec((1, r, d), grp),
            pl.BlockSpec((1, r, d), grp),
            pl.BlockSpec((1, 1, d), lay),
            _resident((1, d, nw), lay),
        ],
        out_specs=specs,
        out_shape=[jax.ShapeDtypeStruct(s, t) for s, t in outs],
        compiler_params=_params("arbitrary"),
        name="inproj",
    )(x, sc, sh, g1, w_cat)


def _conv_taps(ext_ref, start, rows, wdw_ref, bdw_ref, lng_ref, lnb_ref, width):
    acc = jnp.zeros((rows, wdw_ref.shape[-1]), F32) + bdw_ref[0]
    for k in range(width):
        acc = acc + ext_ref[pl.ds(start + k, rows), :] * wdw_ref[0, k:k + 1, :]
    mu = jnp.mean(acc, axis=-1, keepdims=True)
    cen = acc - mu
    var = jnp.mean(cen * cen, axis=-1, keepdims=True)
    y = (cen * lax.rsqrt(var + EPS)) * lng_ref[0] + lnb_ref[0]
    return y * jax.nn.sigmoid(y)


def _conv_kernel(u_ref, hist_ref, wdw_ref, bdw_ref, lng_ref, lnb_ref, y_ref, tail_ref, ext_scr, *, tt, width):
    j = pl.program_id(1)

    @pl.when(j == 0)
    def _():
        ext_scr[0:CONV_HALO, :] = hist_ref[0]

    @pl.when(j > 0)
    def _():
        ext_scr[0:CONV_HALO, :] = ext_scr[tt:tt + CONV_HALO, :]

    ext_scr[CONV_HALO:CONV_HALO + tt, :] = u_ref[0]
    start = CONV_HALO - (width - 1)
    y_ref[0] = _conv_taps(ext_scr, start, tt, wdw_ref, bdw_ref, lng_ref, lnb_ref, width).astype(BF16)

    @pl.when(j == pl.num_programs(1) - 1)
    def _():
        tail_ref[0] = ext_scr[tt:tt + CONV_HALO, :]


def _conv_prompt(u, hist, w_dw, b_dw, ln_g, ln_b, layer, *, tt):
    b, t, c = u.shape
    width = w_dw.shape[1]
    lay = lambda bi, j: (layer, 0, 0)
    return pl.pallas_call(
        functools.partial(_conv_kernel, tt=tt, width=width),
        grid=(b, t // tt),
        in_specs=[
            pl.BlockSpec((1, tt, c), lambda bi, j: (bi, j, 0)),
            pl.BlockSpec((1, CONV_HALO, c), lambda bi, j: (bi, 0, 0)),
            pl.BlockSpec((1, width, c), lay),
            pl.BlockSpec((1, 1, c), lay),
            pl.BlockSpec((1, 1, c), lay),
            pl.BlockSpec((1, 1, c), lay),
        ],
        out_specs=[pl.BlockSpec((1, tt, c), lambda bi, j: (bi, j, 0)),
                   pl.BlockSpec((1, CONV_HALO, c), lambda bi, j: (bi, 0, 0))],
        out_shape=[jax.ShapeDtypeStruct((b, t, c), BF16), jax.ShapeDtypeStruct((b, CONV_HALO, c), F32)],
        scratch_shapes=[pltpu.VMEM((CONV_HALO + tt, c), F32)],
        compiler_params=_params("arbitrary", "arbitrary"),
        name="conv_prompt",
    )(u, hist, w_dw, b_dw, ln_g, ln_b)


def _conv_small_kernel(ext_ref, wdw_ref, bdw_ref, lng_ref, lnb_ref, y_ref, *, rows, width):
    y_ref[0] = _conv_taps(ext_ref.at[0], 0, rows, wdw_ref, bdw_ref, lng_ref, lnb_ref, width).astype(BF16)


def _conv_sample(ext, w_dw, b_dw, ln_g, ln_b, layer, *, rows):
    b, r, c = ext.shape
    width = w_dw.shape[1]
    lay = lambda bi: (layer, 0, 0)
    return pl.pallas_call(
        functools.partial(_conv_small_kernel, rows=rows, width=width),
        grid=(b,),
        in_specs=[
            pl.BlockSpec((1, r, c), lambda bi: (bi, 0, 0)),
            pl.BlockSpec((1, width, c), lay),
            pl.BlockSpec((1, 1, c), lay),
            pl.BlockSpec((1, 1, c), lay),
            pl.BlockSpec((1, 1, c), lay),
        ],
        out_specs=pl.BlockSpec((1, rows, c), lambda bi: (bi, 0, 0)),
        out_shape=jax.ShapeDtypeStruct((b, rows, c), BF16),
        compiler_params=_params("arbitrary"),
        name="conv_sample",
    )(ext, w_dw, b_dw, ln_g, ln_b)


def _moba_prompt_kernel(q_ref, k16_ref, vt_ref, km_ref, bias_ref, o_ref,
                        qm_scr, sel_scr, m_scr, l_scr, acc_scr, *, nb, n_heads, hd):
    i = pl.program_id(1)
    blk = MOBA_BLOCK
    q = q_ref[0]
    km = km_ref[0]
    lane = lax.broadcasted_iota(jnp.int32, (blk, LANES), 1)
    blk_idx = lax.broadcasted_iota(jnp.int32, (nb, blk), 0)
    heads_per_slab = LANES // hd

    for h in range(n_heads):
        p, w = divmod(h, heads_per_slab)
        qm = jnp.where((lane // hd) == w, q[:, p * LANES:(p + 1) * LANES], 0.0)
        qm_scr[h] = qm.astype(BF16)
        s = _dot_nt_f32(km[:, p * LANES:(p + 1) * LANES], qm)
        s = jnp.where(blk_idx < i, s, -jnp.inf)
        sel = _top_k_mask(s, blk_idx, 0, MOBA_TOPK, nb)
        sel_scr[h] = jnp.where(sel, 0.0, NEG)

    krow = lax.broadcasted_iota(jnp.int32, (blk, blk), 0)
    qcol = lax.broadcasted_iota(jnp.int32, (blk, blk), 1)

    def block(n, kind):
        kb = k16_ref[n]
        vb = vt_ref[n]
        for h in range(n_heads):
            p = h // heads_per_slab
            s = _dot_nt(kb[:, p * LANES:(p + 1) * LANES], qm_scr[h])
            vh = vb[h * hd:(h + 1) * hd, :]
            if kind == "own":
                s = jnp.where(krow <= qcol, s + bias_ref[h, 0], NEG)
                m = jnp.max(s, axis=0, keepdims=True)
                pr = jnp.exp(s - m)
                m_scr[h:h + 1, :] = m
                l_scr[h:h + 1, :] = jnp.sum(pr, axis=0, keepdims=True)
                acc_scr[h * hd:(h + 1) * hd, :] = _dot(vh, pr.astype(BF16))
            else:
                s = s + sel_scr[h, pl.ds(n, 1), :]
                if kind == "prev":
                    s = s + bias_ref[h, 1]
                m_old = m_scr[h:h + 1, :]
                m = jnp.maximum(m_old, jnp.max(s, axis=0, keepdims=True))
                alpha = jnp.exp(m_old - m)
                pr = jnp.exp(s - m)
                m_scr[h:h + 1, :] = m
                l_scr[h:h + 1, :] = alpha * l_scr[h:h + 1, :] + jnp.sum(pr, axis=0, keepdims=True)
                acc_scr[h * hd:(h + 1) * hd, :] = (alpha * acc_scr[h * hd:(h + 1) * hd, :]
                                                  + _dot(vh, pr.astype(BF16)))

    block(i, "own")

    @pl.when(i >= 1)
    def _():
        block(i - 1, "prev")

    def far(n, carry):
        block(n, "far")
        return carry

    lax.fori_loop(0, i - 1, far, 0)

    for h in range(n_heads):
        acc_scr[h * hd:(h + 1) * hd, :] = acc_scr[h * hd:(h + 1) * hd, :] / l_scr[h:h + 1, :]
    o_ref[0] = acc_scr[...].T.astype(BF16)


def _moba_prompt(q, k16, vt, km, bias_t, *, batch, n_heads, hd):
    n, d_b = q.shape
    t = n // batch
    nb = t // MOBA_BLOCK
    blk = MOBA_BLOCK
    body = functools.partial(_moba_prompt_kernel, nb=nb, n_heads=n_heads, hd=hd)
    return pl.pallas_call(
        body,
        grid=(batch, nb),
        in_specs=[
            pl.BlockSpec((1, blk, d_b), lambda b, i: (b * nb + i, 0, 0)),
            pl.BlockSpec((nb, blk, d_b), lambda b, i: (b, 0, 0)),
            pl.BlockSpec((nb, d_b, blk), lambda b, i: (b, 0, 0)),
            pl.BlockSpec((1, nb, d_b), lambda b, i: (b, 0, 0)),
            _resident((n_heads, 2, blk, blk), lambda b, i: (0, 0, 0, 0)),
        ],
        out_specs=pl.BlockSpec((1, blk, d_b), lambda b, i: (b * nb + i, 0, 0)),
        out_shape=jax.ShapeDtypeStruct((n // blk, blk, d_b), BF16),
        scratch_shapes=[
            pltpu.VMEM((n_heads, blk, LANES), BF16),
            pltpu.VMEM((n_heads, nb, blk), F32),
            pltpu.VMEM((n_heads, blk), F32),
            pltpu.VMEM((n_heads, blk), F32),
            pltpu.VMEM((n_heads * hd, blk), F32),
        ],
        compiler_params=_params("arbitrary", "arbitrary"),
        name="moba_prompt",
    )(q.reshape(n // blk, blk, d_b), k16, vt, km.reshape(batch, nb, d_b), bias_t)


def _moba_sample_kernel(pt_ref, q_ref, kn_ref, vn_ref, bown_ref, btail_ref, *rest,
                        n_chunks, n_heads, hd, t_new, page):
    del pt_ref
    pps = PAGES_PER_STEP
    k_refs = rest[:pps]
    v_refs = rest[pps:2 * pps]
    o_ref = rest[2 * pps]
    qbd_scr, s_scr, km_scr, acc_scr, l_scr, kv_scr = rest[2 * pps + 1:]
    c = pl.program_id(1)
    rows = t_new * n_heads
    d_b = n_heads * hd
    chunk = pps * page
    blocks_per_chunk = chunk // MOBA_BLOCK
    pages_per_block = MOBA_BLOCK // page
    nbp = n_chunks * blocks_per_chunk

    row_i = lax.broadcasted_iota(jnp.int32, (rows, d_b), 0)
    lane_i = lax.broadcasted_iota(jnp.int32, (rows, d_b), 1)
    head_mask = (lane_i // hd) == (row_i % n_heads)

    @pl.when(c == 0)
    def _():
        qbd_scr[...] = jnp.where(head_mask, q_ref[0], 0.0)
        km_scr[...] = jnp.zeros(km_scr.shape, F32)

    @pl.when(c < n_chunks)
    def _():
        qb16 = qbd_scr[...].astype(BF16)
        psum = None
        for p in range(pps):
            kp = k_refs[p][0, 0]
            s_scr[c, :, p * page:(p + 1) * page] = _dot_nt(qb16, kp.astype(BF16))
            ps = jnp.sum(kp, axis=0, keepdims=True)
            psum = ps if p % pages_per_block == 0 else psum + ps
            if p % pages_per_block == pages_per_block - 1:
                km_scr[pl.ds(c * blocks_per_chunk + p // pages_per_block, 1), :] = psum * (1.0 / MOBA_BLOCK)

    @pl.when(c == n_chunks)
    def _():
        qbd = qbd_scr[...]
        bs = _dot_nt_f32(qbd, km_scr[...])
        blk_i = lax.broadcasted_iota(jnp.int32, bs.shape, 1)
        bs = jnp.where(blk_i < nbp, bs, -jnp.inf)
        sel = _top_k_mask(bs, blk_i, 1, MOBA_TOPK, LANES)
        sel16 = jnp.where(sel, 1.0, 0.0).astype(BF16)

        kv_scr[...] = jnp.zeros(kv_scr.shape, F32)
        kv_scr[0:kn_ref.shape[1], :] = kn_ref[0]
        s_own = _dot_nt(qbd.astype(BF16), kv_scr[...].astype(BF16)) + bown_ref[...]
        ro = lax.broadcasted_iota(jnp.int32, s_own.shape, 0) // n_heads
        co = lax.broadcasted_iota(jnp.int32, s_own.shape, 1)
        s_own = jnp.where((co <= ro) & (co < t_new), s_own, NEG)
        m = jnp.max(s_own, axis=-1, keepdims=True)

        er = lax.broadcasted_iota(jnp.int32, (LANES, chunk), 0)
        ec = lax.broadcasted_iota(jnp.int32, (LANES, chunk), 1) // MOBA_BLOCK
        for cc in range(n_chunks):
            expand = jnp.where(er == ec + cc * blocks_per_chunk, 1.0, 0.0).astype(BF16)
            s = s_scr[cc] + jnp.where(_dot(sel16, expand) > 0.5, 0.0, NEG)
            s_scr[cc] = s
            if cc == n_chunks - 1:
                tail = slice(chunk - MOBA_BLOCK, chunk)
                s_scr[cc, :, tail] = s[:, tail] + btail_ref[...]
            m = jnp.maximum(m, jnp.max(s_scr[cc], axis=-1, keepdims=True))

        p_own = jnp.exp(s_own - m)
        l = jnp.sum(p_own, axis=-1, keepdims=True)
        for cc in range(n_chunks):
            pr = jnp.exp(s_scr[cc] - m)
            l = l + jnp.sum(pr, axis=-1, keepdims=True)
            s_scr[cc] = pr
        kv_scr[0:vn_ref.shape[1], :] = vn_ref[0]
        acc_scr[...] = _dot(p_own.astype(BF16), kv_scr[...].astype(BF16))
        l_scr[...] = jnp.broadcast_to(l, l_scr.shape)

    @pl.when(c >= n_chunks)
    def _():
        cc = c - n_chunks
        acc = acc_scr[...]
        for p in range(pps):
            pr = s_scr[cc, :, p * page:(p + 1) * page]
            acc = acc + _dot(pr.astype(BF16), v_refs[p][0, 0].astype(BF16))
        acc_scr[...] = acc

    @pl.when(c == 2 * n_chunks - 1)
    def _():
        o = jnp.where(head_mask, acc_scr[...] / l_scr[:, 0:1], 0.0)
        o_ref[0] = jnp.sum(o.reshape(t_new, n_heads, d_b), axis=1).astype(BF16)


def _moba_sample(page_table, q_rep, k_new, v_new, bias_own, bias_tail, cache_k, cache_v, layer,
                 *, n_heads, hd, t_new):
    b, rows, d_b = q_rep.shape
    _, _, page, _ = cache_k.shape
    n_pages = page_table.shape[1]
    pps = PAGES_PER_STEP
    n_chunks = n_pages // pps
    assert n_pages % pps == 0 and (pps * page) % MOBA_BLOCK == 0 and MOBA_BLOCK % page == 0
    chunk = pps * page

    def k_map(r):
        return lambda bi, c, pt: (layer, pt[bi, jnp.minimum(c, n_chunks - 1) * pps + r], 0, 0)

    def v_map(r):
        return lambda bi, c, pt: (layer, pt[bi, jnp.maximum(c - n_chunks, 0) * pps + r], 0, 0)

    per_b = lambda bi, c, pt: (bi, 0, 0)
    const2 = lambda bi, c, pt: (0, 0)
    in_specs = [
        pl.BlockSpec((1, rows, d_b), per_b),
        pl.BlockSpec((1, k_new.shape[1], d_b), per_b),
        pl.BlockSpec((1, v_new.shape[1], d_b), per_b),
        pl.BlockSpec(bias_own.shape, const2),
        pl.BlockSpec(bias_tail.shape, const2),
    ]
    in_specs += [pl.BlockSpec((1, 1, page, d_b), k_map(r)) for r in range(pps)]
    in_specs += [pl.BlockSpec((1, 1, page, d_b), v_map(r)) for r in range(pps)]
    body = functools.partial(_moba_sample_kernel, n_chunks=n_chunks, n_heads=n_heads, hd=hd,
                             t_new=t_new, page=page)
    grid_spec = pltpu.PrefetchScalarGridSpec(
        num_scalar_prefetch=1,
        grid=(b, 2 * n_chunks),
        in_specs=in_specs,
        out_specs=pl.BlockSpec((1, t_new, d_b), per_b),
        scratch_shapes=[
            pltpu.VMEM((rows, d_b), F32),
            pltpu.VMEM((n_chunks, rows, chunk), F32),
            pltpu.VMEM((LANES, d_b), F32),
            pltpu.VMEM((rows, d_b), F32),
            pltpu.VMEM((rows, LANES), F32),
            pltpu.VMEM((LANES, d_b), F32),
        ],
    )
    return pl.pallas_call(
        body,
        grid_spec=grid_spec,
        out_shape=jax.ShapeDtypeStruct((b, t_new, d_b), BF16),
        compiler_params=_params("arbitrary", "arbitrary"),
        name="moba_sample",
    )(page_table, q_rep, k_new, v_new, bias_own, bias_tail, *([cache_k] * pps), *([cache_v] * pps))


def _gla_kernel(gq_ref, gk_ref, gv_ref, gr_ref, lr_ref, wa_ref, ba_ref, gg_ref, st0_ref,
                o_ref, stf_ref, st_scr, bc_scr, bl_scr, qd_scr, kd_scr, vt_scr, o_scr,
                *, tt, valid, n_heads):
    j = pl.program_id(1)
    c_sz = GLA_CHUNK
    hp = n_heads * LANES

    @pl.when(j == 0)
    def _():
        st_scr[...] = st0_ref[0]

    x = _dot(lr_ref[0].astype(BF16), wa_ref[0]) + ba_ref[0]
    la = _log_sigmoid(x) * (1.0 / GATE_NORM)
    if valid < tt:
        la = jnp.where(lax.broadcasted_iota(jnp.int32, (tt, hp), 0) < valid, la, 0.0)
    r_i = lax.broadcasted_iota(jnp.int32, (tt, tt), 0)
    c_i = lax.broadcasted_iota(jnp.int32, (tt, tt), 1)
    same = (r_i // c_sz) == (c_i // c_sz)
    bc = _dot_f32(jnp.where(same & (c_i <= r_i), 1.0, 0.0), la)
    bl = _dot_f32(jnp.where(same, 1.0, 0.0), la)
    bc_scr[...] = bc
    bl_scr[...] = bl
    qd_scr[...] = (gq_ref[0] * jnp.exp(bc)).astype(BF16)
    kd_scr[...] = (gk_ref[0] * jnp.exp(bl - bc)).astype(BF16)
    vt_scr[...] = gv_ref[0].T.astype(BF16)

    t_i = lax.broadcasted_iota(jnp.int32, (c_sz, LANES), 0)
    chunk_of_col = lax.broadcasted_iota(jnp.int32, (LANES, tt), 1) // c_sz

    def chunk(c, carry):
        rows = pl.ds(pl.multiple_of(c * c_sz, c_sz), c_sz)
        bc_c = bc_scr[rows, :]
        q_c = gq_ref[0, rows, :]
        k_c = gk_ref[0, rows, :]
        v_c = gv_ref[0, rows, :]
        decay = jnp.exp(bl_scr[pl.ds(c * c_sz, 1), :])
        in_chunk = chunk_of_col == c
        for h in range(n_heads):
            hs = slice(h * LANES, (h + 1) * LANES)
            st = st_scr[h]
            o_h = _dot_nt(qd_scr[rows, hs], st.astype(BF16))
            bch = bc_c[:, hs]
            qh = q_c[:, hs]
            for s in range(c_sz):
                dec = jnp.exp(jnp.where(t_i >= s, bch - bch[s:s + 1, :], -jnp.inf))
                a_s = jnp.sum(qh * k_c[s:s + 1, hs] * dec, axis=-1, keepdims=True)
                o_h = o_h + a_s * v_c[s:s + 1, hs]
            o_scr[rows, hs] = o_h
            vtm = jnp.where(in_chunk, vt_scr[hs, :], jnp.zeros((), BF16))
            st_scr[h] = decay[:, hs] * st + _dot(vtm, kd_scr[:, hs])
        return carry

    lax.fori_loop(0, (valid + c_sz - 1) // c_sz, chunk, 0)

    o = o_scr[...]
    r = gr_ref[0]
    for h in range(n_heads):
        hs = slice(h * LANES, (h + 1) * LANES)
        oh = o[:, hs]
        oh = oh * lax.rsqrt(jnp.mean(oh * oh, axis=-1, keepdims=True) + EPS)
        rh = r[:, hs]
        o_ref[0, :, hs] = ((oh * gg_ref[0, :, hs]) * (rh * jax.nn.sigmoid(rh))).astype(BF16)

    @pl.when(j == pl.num_programs(1) - 1)
    def _():
        stf_ref[0] = st_scr[...]


def _gla(gq, gk, gv, gr, lr, wa, ba, gg, st0, layer, *, tt, valid):
    b, t, hp = gq.shape
    n_heads = hp // LANES
    lay = lambda bi, j: (layer, 0, 0)
    tile = lambda bi, j: (bi, j, 0)
    per_b = lambda bi, j: (bi, 0, 0, 0)
    body = functools.partial(_gla_kernel, tt=tt, valid=valid, n_heads=n_heads)
    return pl.pallas_call(
        body,
        grid=(b, t // tt),
        in_specs=[pl.BlockSpec((1, tt, hp), tile)] * 4 + [
            pl.BlockSpec((1, tt, LANES), tile),
            pl.BlockSpec((1, LANES, hp), lay),
            pl.BlockSpec((1, 1, hp), lay),
            pl.BlockSpec((1, 1, hp), lay),
            pl.BlockSpec((1, n_heads, LANES, LANES), per_b),
        ],
        out_specs=[pl.BlockSpec((1, tt, hp), tile), pl.BlockSpec((1, n_heads, LANES, LANES), per_b)],
        out_shape=[jax.ShapeDtypeStruct((b, t, hp), BF16),
                   jax.ShapeDtypeStruct((b, n_heads, LANES, LANES), F32)],
        scratch_shapes=[
            pltpu.VMEM((n_heads, LANES, LANES), F32),
            pltpu.VMEM((tt, hp), F32),
            pltpu.VMEM((tt, hp), F32),
            pltpu.VMEM((tt, hp), BF16),
            pltpu.VMEM((tt, hp), BF16),
            pltpu.VMEM((hp, tt), BF16),
            pltpu.VMEM((tt, hp), F32),
        ],
        compiler_params=_params("arbitrary", "arbitrary"),
        name="gla",
    )(gq, gk, gv, gr, lr, wa, ba, gg, st0)


def _merge_mlp_kernel(x_ref, ya_ref, ob_ref, oc_ref, gate_ref, gt1_ref, sc2_ref, sh2_ref, gt2_ref,
                      g2_ref, gf_ref, wa_ref, wb_ref, wc_ref, wo_ref, wup_ref, wdn_ref, *outs, final):
    d = x_ref.shape[-1]
    m = (gate_ref[:, 0:d] * _dot(ya_ref[...], wa_ref[0])
         + gate_ref[:, d:2 * d] * _dot(ob_ref[...], wb_ref[0])
         + gate_ref[:, 2 * d:3 * d] * _dot(oc_ref[...], wc_ref[0]))
    x1 = x_ref[...] + gt1_ref[0] * _dot(m.astype(BF16), wo_ref[0])
    h2 = _rms_rows(x1, g2_ref[0]) * (1.0 + sc2_ref[0]) + sh2_ref[0]
    up = jnp.maximum(_dot(h2.astype(BF16), wup_ref[0]), 0.0)
    x2 = x1 + gt2_ref[0] * _dot((up * up).astype(BF16), wdn_ref[0])
    outs[0][...] = x2
    if final:
        outs[1][...] = _rms_rows(x2, gf_ref[...])


def _merge_mlp(x, ya, ob, oc, gates, gt1, sc2, sh2, gt2, g2, gf, wa, wb, wc, wo, wup, wdn, layer,
               *, tm, tiles_per_group, final):
    n, d = x.shape
    c = ya.shape[-1]
    dff = wup.shape[-1]
    r = gt1.shape[1]
    row = lambda i: (i, 0)
    grp = lambda i: (i // tiles_per_group, 0, 0)
    lay = lambda i: (layer, 0, 0)
    n_out = 2 if final else 1
    return pl.pallas_call(
        functools.partial(_merge_mlp_kernel, final=final),
        grid=(n // tm,),
        in_specs=[pl.BlockSpec((tm, d), row)] + [pl.BlockSpec((tm, c), row)] * 3 + [
            pl.BlockSpec((tm, 3 * d), row),
            pl.BlockSpec((1, r, d), grp), pl.BlockSpec((1, r, d), grp),
            pl.BlockSpec((1, r, d), grp), pl.BlockSpec((1, r, d), grp),
            pl.BlockSpec((1, 1, d), lay),
            pl.BlockSpec((1, d), lambda i: (0, 0)),
            _resident((1, c, d), lay), _resident((1, c, d), lay), _resident((1, c, d), lay),
            _resident((1, d, d), lay), _resident((1, d, dff), lay), _resident((1, dff, d), lay),
        ],
        out_specs=[pl.BlockSpec((tm, d), row)] * n_out,
        out_shape=[jax.ShapeDtypeStruct((n, d), F32)] * n_out,
        compiler_params=_params("arbitrary"),
        name="merge_mlp",
    )(x, ya, ob, oc, gates, gt1, sc2, sh2, gt2, g2, gf, wa, wb, wc, wo, wup, wdn)


def _t5_bucket_table(n_buckets, max_dist):
    dist = np.arange(max_dist + 1)
    max_exact = n_buckets // 2
    d = np.maximum(dist, 1).astype(np.float32)
    large = max_exact + (np.log(d / max_exact) / math.log(MAX_DISTANCE / max_exact)
                         * (n_buckets - max_exact)).astype(np.int32)
    large = np.minimum(large, n_buckets - 1)
    return np.where(dist < max_exact, dist, large).astype(np.int32)


def _pad_heads(w, n_heads):
    k = w.shape[-1] // n_heads
    w = w.reshape(w.shape[:-1] + (n_heads, k))
    w = jnp.pad(w, [(0, 0)] * (w.ndim - 1) + [(0, LANES - k)])
    return w.reshape(w.shape[:-2] + (n_heads * LANES,))


def kernel(x_prompt, x_sample, c_prompt, c_sample, cache_k, cache_v, page_table, state_conv, state_gla,
           w_ada, b_ada, g_norm1, w_in, w_dw, b_dw, ln_g, ln_b, w_pw2, w_pb, rel_bias, w_a2, b_a, g_gla,
           w_pc, w_o, g_norm2, w_up, w_down, g_final):
    bp, t, d = x_prompt.shape
    bs, ts, _ = x_sample.shape
    depth = w_in.shape[0]
    _, n_pool, page, h_b, hd_b = cache_k.shape
    d_b = h_b * hd_b
    d_conv = w_dw.shape[-1]
    width = w_dw.shape[1]
    rank = w_a2.shape[1]
    h_c = GLA_HEADS
    dk_c = w_a2.shape[-1] // h_c
    dv_c = state_gla.shape[-1]
    hp = h_c * LANES
    n_buckets = rel_bias.shape[0]
    n_pages = page_table.shape[1]
    past_len = n_pages * page
    assert dv_c == LANES and h_c * dv_c == hp and width - 1 <= CONV_HALO

    sizes = (2 * d_conv, d_b, d_b, d_b, h_c * dk_c, h_c * dk_c, h_c * dv_c, h_c * dv_c, rank, d, d, d)
    offs = np.cumsum((0,) + sizes)
    part = [w_in[:, :, offs[i]:offs[i + 1]] for i in range(len(sizes))]
    w_cat = jnp.concatenate([
        part[0], part[1], part[2], part[3],
        _pad_heads(part[4], h_c), _pad_heads(part[5], h_c), part[6], part[7],
        jnp.pad(part[8], ((0, 0), (0, 0), (0, LANES - rank))),
        part[9], part[10], part[11]], axis=-1).astype(BF16)
    wa2 = jnp.pad(_pad_heads(w_a2, h_c), ((0, 0), (0, LANES - rank), (0, 0))).astype(BF16)
    ba2 = _pad_heads(b_a, h_c).reshape(depth, 1, hp)
    gg = g_gla.reshape(depth, 1, hp)
    wpw2, wpb, wpc, wo, wup, wdn = (w.astype(BF16) for w in (w_pw2, w_pb, w_pc, w_o, w_up, w_down))
    g1 = g_norm1.reshape(depth, 1, d)
    g2 = g_norm2.reshape(depth, 1, d)
    gf = g_final.reshape(1, d)
    bdw = b_dw.reshape(depth, 1, d_conv)
    lng = ln_g.reshape(depth, 1, d_conv)
    lnb = ln_b.reshape(depth, 1, d_conv)

    n_c = bp + bs
    rows_c = -(-n_c // 8) * 8
    c_all = jnp.pad(jnp.concatenate([c_prompt, c_sample], axis=0), ((0, rows_c - n_c), (0, 0)))
    mods = _adaln(c_all, w_ada, b_ada).reshape(depth, rows_c, 6, d)

    bucket = _t5_bucket_table(n_buckets, past_len + MOBA_BLOCK + t)
    rb = rel_bias - rel_bias[n_buckets - 1:n_buckets, :]
    blk = MOBA_BLOCK
    ki = np.arange(blk)[:, None]
    qi = np.arange(blk)[None, :]
    idx = np.stack([bucket[np.maximum(qi - ki, 0)], bucket[blk + qi - ki]])
    bias_t = jnp.transpose(rb[idx], (3, 0, 1, 2))
    rows_s = ts * h_b
    tq = np.arange(rows_s)[:, None] // h_b
    hq = np.arange(rows_s)[:, None] % h_b
    own_cols = np.arange(LANES)[None, :]
    bias_own = rb[bucket[np.clip(tq - own_cols, 0, None)], hq]
    tail_pos = past_len - blk + np.arange(blk)[None, :]
    bias_tail = rb[bucket[past_len + tq - tail_pos], hq]

    ck = cache_k.reshape(depth, n_pool, page, d_b)
    cv = cache_v.reshape(depth, n_pool, page, d_b)

    xp = x_prompt.reshape(bp * t, d)
    xs = x_sample.reshape(bs * ts, d)
    n_s = bs * ts
    tm = TOKEN_TILE
    tpb = t // tm
    dims = (d_conv, d_b, hp, hd_b, dk_c)
    ts_pad = 8
    gla_rows_s = LANES

    kp_l, vp_l, cp_l, sp_l, ks_l, vs_l, cs_l, ss_l = ([] for _ in range(8))
    y_prompt = y_sample = None
    zeros_hist = jnp.zeros((bp, CONV_HALO, d_conv), F32)
    zeros_state = jnp.zeros((bp, h_c, LANES, LANES), F32)
    for l in range(depth):
        final = l == depth - 1
        mp = [mods[l, :bp, i].reshape(bp, 1, d) for i in range(6)]
        ms = [jnp.repeat(mods[l, bp:n_c, i], ts, axis=0).reshape(1, n_s, d) for i in range(6)]

        (u, qb, kb, vb, k16, vt, km, gq, gk, gv, gr, lr, gates) = _inproj(
            xp, mp[1], mp[0], g1, w_cat, l, tm=tm, tiles_per_group=tpb, dims=dims, attn_aux=True)
        ya, tail = _conv_prompt(u.reshape(bp, t, d_conv), zeros_hist, w_dw, bdw, lng, lnb, l, tt=tm)
        ob = _moba_prompt(qb, k16, vt, km, bias_t, batch=bp, n_heads=h_b, hd=hd_b)
        oc, stf = _gla(*(a.reshape(bp, t, -1) for a in (gq, gk, gv, gr, lr)), wa2, ba2, gg, zeros_state, l,
                       tt=tm, valid=tm)
        res = _merge_mlp(xp, ya.reshape(bp * t, d_conv), ob.reshape(bp * t, d_b), oc.reshape(bp * t, hp),
                         gates, mp[2], mp[4], mp[3], mp[5], g2, gf, wpw2, wpb, wpc, wo, wup, wdn, l,
                         tm=tm, tiles_per_group=tpb, final=final)
        xp = res[0]
        if final:
            y_prompt = res[1].reshape(bp, t, d)
        kp_l.append(kb.reshape(bp, t, h_b, hd_b))
        vp_l.append(vb.reshape(bp, t, h_b, hd_b))
        cp_l.append(tail[:, CONV_HALO - (width - 1):])
        sp_l.append(jnp.swapaxes(stf[..., :dk_c], -1, -2))

        (u, qb, kb, vb, gq, gk, gv, gr, lr, gates) = _inproj(
            xs, ms[1], ms[0], g1, w_cat, l, tm=n_s, tiles_per_group=1, dims=dims, attn_aux=False)
        ext = jnp.concatenate([state_conv[l], u.reshape(bs, ts, d_conv)], axis=1)
        ext_p = jnp.pad(ext, ((0, 0), (0, ts_pad + width - 1 - ext.shape[1] + (-(ts_pad + width - 1)) % 8), (0, 0)))
        ya = _conv_sample(ext_p, w_dw, bdw, lng, lnb, l, rows=ts_pad)[:, :ts]
        q_rep = jnp.repeat(qb.reshape(bs, ts, d_b), h_b, axis=1)
        pad_new = lambda a: jnp.pad(a.reshape(bs, ts, d_b), ((0, 0), (0, ts_pad - ts), (0, 0)))
        ob = _moba_sample(page_table, q_rep, pad_new(kb), pad_new(vb), bias_own, bias_tail, ck, cv, l,
                          n_heads=h_b, hd=hd_b, t_new=ts)
        pad_t = lambda a: jnp.pad(a.reshape(bs, ts, -1), ((0, 0), (0, gla_rows_s - ts), (0, 0)))
        st0 = jnp.pad(jnp.swapaxes(state_gla[l], -1, -2), ((0, 0), (0, 0), (0, 0), (0, LANES - dk_c)))
        oc, stf = _gla(pad_t(gq), pad_t(gk), pad_t(gv), pad_t(gr), pad_t(lr), wa2, ba2, gg, st0, l,
                       tt=gla_rows_s, valid=ts)
        res = _merge_mlp(xs, ya.reshape(n_s, d_conv), ob.reshape(n_s, d_b), oc[:, :ts].reshape(n_s, hp),
                         gates, ms[2], ms[4], ms[3], ms[5], g2, gf, wpw2, wpb, wpc, wo, wup, wdn, l,
                         tm=n_s, tiles_per_group=1, final=final)
        xs = res[0]
        if final:
            y_sample = res[1].reshape(bs, ts, d)
        ks_l.append(kb.reshape(bs, ts, h_b, hd_b))
        vs_l.append(vb.reshape(bs, ts, h_b, hd_b))
        cs_l.append(ext[:, -(width - 1):])
        ss_l.append(jnp.swapaxes(stf[..., :dk_c], -1, -2))

    return (y_prompt, y_sample, jnp.stack(kp_l), jnp.stack(vp_l), jnp.stack(cp_l), jnp.stack(sp_l),
            jnp.stack(ks_l), jnp.stack(vs_l), jnp.stack(cs_l), jnp.stack(ss_l))
```

```python
import functools
import math

import numpy as np
import jax
import jax.numpy as jnp
from jax import lax
from jax.experimental import pallas as pl
from jax.experimental.pallas import tpu as pltpu

F32 = jnp.float32
BF16 = jnp.bfloat16

MOBA_BLOCK = 256
MOBA_TOPK = 3
MAX_DISTANCE = 128
GLA_HEADS = 4
GLA_CHUNK = 16
GATE_NORM = 16.0
EPS = 1e-6

LANES = 128
VMEM_LIMIT = 56 * 1024 * 1024

NEG = -1e30
TOKEN_TILE = 256
PAGES_PER_STEP = 8
CONV_HALO = 32

_NT = (((1,), (1,)), ((), ()))


def _dot(a, b):
    return jnp.dot(a, b, preferred_element_type=F32)


def _dot_nt(a, b):
    return lax.dot_general(a, b, _NT, preferred_element_type=F32)


def _dot_f32(a, b):
    return jnp.dot(a, b, precision=lax.Precision.HIGHEST, preferred_element_type=F32)


def _dot_nt_f32(a, b):
    return lax.dot_general(a, b, _NT, precision=lax.Precision.HIGHEST, preferred_element_type=F32)


def _params(*sem):
    return pltpu.CompilerParams(dimension_semantics=sem, vmem_limit_bytes=VMEM_LIMIT)


def _resident(shape, index_map):
    return pl.BlockSpec(shape, index_map, pipeline_mode=pl.Buffered(1))


def _rms_rows(x, g):
    return (x * lax.rsqrt(jnp.mean(x * x, axis=-1, keepdims=True) + EPS)) * g


def _log_sigmoid(x):
    return jnp.minimum(x, 0.0) - jnp.log1p(jnp.exp(-jnp.abs(x)))


def _top_k_mask(s, idx, axis, k, sentinel):
    sel = jnp.zeros(s.shape, dtype=jnp.bool_)
    for _ in range(k):
        mx = jnp.max(s, axis=axis, keepdims=True)
        first = jnp.min(jnp.where(s == mx, idx, sentinel), axis=axis, keepdims=True)
        pick = (idx == first) & (mx > -jnp.inf)
        sel = sel | pick
        s = jnp.where(pick, -jnp.inf, s)
    return sel


def _adaln_kernel(c_ref, w_ref, b_ref, o_ref):
    c = c_ref[...]
    s = c * jax.nn.sigmoid(c)
    o_ref[0] = _dot(s.astype(BF16), w_ref[0].astype(BF16)) + b_ref[0]


def _adaln(c_all, w_ada, b_ada):
    depth, d, n = w_ada.shape
    rows = c_all.shape[0]
    tn = 1536
    return pl.pallas_call(
        _adaln_kernel,
        grid=(depth, n // tn),
        in_specs=[
            pl.BlockSpec((rows, d), lambda l, j: (0, 0)),
            pl.BlockSpec((1, d, tn), lambda l, j: (l, 0, j)),
            pl.BlockSpec((1, 1, tn), lambda l, j: (l, 0, j)),
        ],
        out_specs=pl.BlockSpec((1, rows, tn), lambda l, j: (l, 0, j)),
        out_shape=jax.ShapeDtypeStruct((depth, rows, n), F32),
        compiler_params=_params("arbitrary", "arbitrary"),
        name="adaln",
    )(c_all, w_ada, b_ada.reshape(depth, 1, n))


def _inproj_kernel(x_ref, sc_ref, sh_ref, g_ref, w_ref, *outs, d_conv, d_b, hp, d_model,
                   q_scale, gq_scale, attn_aux):
    if attn_aux:
        (u_ref, q_ref, k_ref, v_ref, k16_ref, vt_ref, km_ref,
         gq_ref, gk_ref, gv_ref, gr_ref, lr_ref, gate_ref) = outs
    else:
        (u_ref, q_ref, k_ref, v_ref, gq_ref, gk_ref, gv_ref, gr_ref, lr_ref, gate_ref) = outs
    x = x_ref[...]
    h = _rms_rows(x, g_ref[0]) * (1.0 + sc_ref[0]) + sh_ref[0]
    hb = h.astype(BF16)

    pos = [0]

    def seg(width):
        lo = pos[0]
        pos[0] = lo + width
        return _dot(hb, w_ref[0, :, lo:lo + width])

    a = seg(d_conv)
    g = seg(d_conv)
    u_ref[...] = a * jax.nn.sigmoid(g)
    q_ref[...] = seg(d_b) * q_scale
    k = seg(d_b)
    k_ref[...] = k
    v = seg(d_b)
    v_ref[...] = v
    if attn_aux:
        k16_ref[0] = k.astype(BF16)
        vt_ref[0] = v.T.astype(BF16)
        km_ref[0] = jnp.mean(k, axis=0, keepdims=True)
    gq_ref[...] = seg(hp) * gq_scale
    gk_ref[...] = seg(hp)
    gv_ref[...] = seg(hp)
    gr_ref[...] = seg(hp)
    lr_ref[...] = seg(LANES)
    for i in range(3):
        gate_ref[:, i * d_model:(i + 1) * d_model] = jax.nn.sigmoid(seg(d_model))


def _inproj(x, sc, sh, g1, w_cat, layer, *, tm, tiles_per_group, dims, attn_aux):
    n, d = x.shape
    d_conv, d_b, hp, hd_b, dk_c = dims
    nw = w_cat.shape[-1]
    r = sc.shape[1]
    grid = (n // tm,)
    row = lambda i: (i, 0)
    grp = lambda i: (i // tiles_per_group, 0, 0)
    lay = lambda i: (layer, 0, 0)
    outs = [((n, d_conv), F32), ((n, d_b), F32), ((n, d_b), F32), ((n, d_b), F32)]
    specs = [pl.BlockSpec((tm, d_conv), row)] + [pl.BlockSpec((tm, d_b), row)] * 3
    if attn_aux:
        assert tm == MOBA_BLOCK
        nblk = n // MOBA_BLOCK
        outs += [((nblk, MOBA_BLOCK, d_b), BF16), ((nblk, d_b, MOBA_BLOCK), BF16), ((nblk, 1, d_b), F32)]
        specs += [pl.BlockSpec((1, MOBA_BLOCK, d_b), lambda i: (i, 0, 0)),
                  pl.BlockSpec((1, d_b, MOBA_BLOCK), lambda i: (i, 0, 0)),
                  pl.BlockSpec((1, 1, d_b), lambda i: (i, 0, 0))]
    outs += [((n, hp), F32)] * 4 + [((n, LANES), F32), ((n, 3 * d), F32)]
    specs += [pl.BlockSpec((tm, hp), row)] * 4 + [pl.BlockSpec((tm, LANES), row), pl.BlockSpec((tm, 3 * d), row)]
    body = functools.partial(_inproj_kernel, d_conv=d_conv, d_b=d_b, hp=hp, d_model=d,
                             q_scale=hd_b ** -0.5, gq_scale=dk_c ** -0.5, attn_aux=attn_aux)
    return pl.pallas_call(
        body,
        grid=grid,
        in_specs=[
            pl.BlockSpec((tm, d), row),
            pl.BlockSpec((1, r, d), grp),
            pl.BlockSpec((1, r, d), grp),
            pl.BlockSpec((1, 1, d), lay),
            _resident((1, d, nw), lay),
        ],
        out_specs=specs,
        out_shape=[jax.ShapeDtypeStruct(s, t) for s, t in outs],
        compiler_params=_params("arbitrary"),
        name="inproj",
    )(x, sc, sh, g1, w_cat)


def _conv_taps(ext_ref, start, rows, wdw_ref, bdw_ref, lng_ref, lnb_ref, width):
    acc = jnp.zeros((rows, wdw_ref.shape[-1]), F32) + bdw_ref[0]
    for k in range(width):
        acc = acc + ext_ref[pl.ds(start + k, rows), :] * wdw_ref[0, k:k + 1, :]
    mu = jnp.mean(acc, axis=-1, keepdims=True)
    cen = acc - mu
    var = jnp.mean(cen * cen, axis=-1, keepdims=True)
    y = (cen * lax.rsqrt(var + EPS)) * lng_ref[0] + lnb_ref[0]
    return y * jax.nn.sigmoid(y)


def _conv_kernel(u_ref, hist_ref, wdw_ref, bdw_ref, lng_ref, lnb_ref, y_ref, tail_ref, ext_scr, *, tt, width):
    j = pl.program_id(1)

    @pl.when(j == 0)
    def _():
        ext_scr[0:CONV_HALO, :] = hist_ref[0]

    @pl.when(j > 0)
    def _():
        ext_scr[0:CONV_HALO, :] = ext_scr[tt:tt + CONV_HALO, :]

    ext_scr[CONV_HALO:CONV_HALO + tt, :] = u_ref[0]
    start = CONV_HALO - (width - 1)
    y_ref[0] = _conv_taps(ext_scr, start, tt, wdw_ref, bdw_ref, lng_ref, lnb_ref, width).astype(BF16)

    @pl.when(j == pl.num_programs(1) - 1)
    def _():
        tail_ref[0] = ext_scr[tt:tt + CONV_HALO, :]


def _conv_prompt(u, hist, w_dw, b_dw, ln_g, ln_b, layer, *, tt):
    b, t, c = u.shape
    width = w_dw.shape[1]
    lay = lambda bi, j: (layer, 0, 0)
    return pl.pallas_call(
        functools.partial(_conv_kernel, tt=tt, width=width),
        grid=(b, t // tt),
        in_specs=[
            pl.BlockSpec((1, tt, c), lambda bi, j: (bi, j, 0)),
            pl.BlockSpec((1, CONV_HALO, c), lambda bi, j: (bi, 0, 0)),
            pl.BlockSpec((1, width, c), lay),
            pl.BlockSpec((1, 1, c), lay),
            pl.BlockSpec((1, 1, c), lay),
            pl.BlockSpec((1, 1, c), lay),
        ],
        out_specs=[pl.BlockSpec((1, tt, c), lambda bi, j: (bi, j, 0)),
                   pl.BlockSpec((1, CONV_HALO, c), lambda bi, j: (bi, 0, 0))],
        out_shape=[jax.ShapeDtypeStruct((b, t, c), BF16), jax.ShapeDtypeStruct((b, CONV_HALO, c), F32)],
        scratch_shapes=[pltpu.VMEM((CONV_HALO + tt, c), F32)],
        compiler_params=_params("arbitrary", "arbitrary"),
        name="conv_prompt",
    )(u, hist, w_dw, b_dw, ln_g, ln_b)


def _conv_small_kernel(ext_ref, wdw_ref, bdw_ref, lng_ref, lnb_ref, y_ref, *, rows, width):
    y_ref[0] = _conv_taps(ext_ref.at[0], 0, rows, wdw_ref, bdw_ref, lng_ref, lnb_ref, width).astype(BF16)


def _conv_sample(ext, w_dw, b_dw, ln_g, ln_b, layer, *, rows):
    b, r, c = ext.shape
    width = w_dw.shape[1]
    lay = lambda bi: (layer, 0, 0)
    return pl.pallas_call(
        functools.partial(_conv_small_kernel, rows=rows, width=width),
        grid=(b,),
        in_specs=[
            pl.BlockSpec((1, r, c), lambda bi: (bi, 0, 0)),
            pl.BlockSpec((1, width, c), lay),
            pl.BlockSpec((1, 1, c), lay),
            pl.BlockSpec((1, 1, c), lay),
            pl.BlockSpec((1, 1, c), lay),
        ],
        out_specs=pl.BlockSpec((1, rows, c), lambda bi: (bi, 0, 0)),
        out_shape=jax.ShapeDtypeStruct((b, rows, c), BF16),
        compiler_params=_params("arbitrary"),
        name="conv_sample",
    )(ext, w_dw, b_dw, ln_g, ln_b)


def _moba_prompt_kernel(q_ref, k16_ref, vt_ref, km_ref, bias_ref, o_ref,
                        qm_scr, sel_scr, m_scr, l_scr, acc_scr, *, nb, n_heads, hd):
    i = pl.program_id(1)
    blk = MOBA_BLOCK
    q = q_ref[0]
    km = km_ref[0]
    lane = lax.broadcasted_iota(jnp.int32, (blk, LANES), 1)
    blk_idx = lax.broadcasted_iota(jnp.int32, (nb, blk), 0)
    heads_per_slab = LANES // hd

    for h in range(n_heads):
        p, w = divmod(h, heads_per_slab)
        qm = jnp.where((lane // hd) == w, q[:, p * LANES:(p + 1) * LANES], 0.0)
        qm_scr[h] = qm.astype(BF16)
        s = _dot_nt_f32(km[:, p * LANES:(p + 1) * LANES], qm)
        s = jnp.where(blk_idx < i, s, -jnp.inf)
        sel = _top_k_mask(s, blk_idx, 0, MOBA_TOPK, nb)
        sel_scr[h] = jnp.where(sel, 0.0, NEG)

    krow = lax.broadcasted_iota(jnp.int32, (blk, blk), 0)
    qcol = lax.broadcasted_iota(jnp.int32, (blk, blk), 1)

    def block(n, kind):
        kb = k16_ref[n]
        vb = vt_ref[n]
        for h in range(n_heads):
            p = h // heads_per_slab
            s = _dot_nt(kb[:, p * LANES:(p + 1) * LANES], qm_scr[h])
            vh = vb[h * hd:(h + 1) * hd, :]
            if kind == "own":
                s = jnp.where(krow <= qcol, s + bias_ref[h, 0], NEG)
                m = jnp.max(s, axis=0, keepdims=True)
                pr = jnp.exp(s - m)
                m_scr[h:h + 1, :] = m
                l_scr[h:h + 1, :] = jnp.sum(pr, axis=0, keepdims=True)
                acc_scr[h * hd:(h + 1) * hd, :] = _dot(vh, pr.astype(BF16))
            else:
                s = s + sel_scr[h, pl.ds(n, 1), :]
                if kind == "prev":
                    s = s + bias_ref[h, 1]
                m_old = m_scr[h:h + 1, :]
                m = jnp.maximum(m_old, jnp.max(s, axis=0, keepdims=True))
                alpha = jnp.exp(m_old - m)
                pr = jnp.exp(s - m)
                m_scr[h:h + 1, :] = m
                l_scr[h:h + 1, :] = alpha * l_scr[h:h + 1, :] + jnp.sum(pr, axis=0, keepdims=True)
                acc_scr[h * hd:(h + 1) * hd, :] = (alpha * acc_scr[h * hd:(h + 1) * hd, :]
                                                  + _dot(vh, pr.astype(BF16)))

    block(i, "own")

    @pl.when(i >= 1)
    def _():
        block(i - 1, "prev")

    def far(n, carry):
        block(n, "far")
        return carry

    lax.fori_loop(0, i - 1, far, 0)

    for h in range(n_heads):
        acc_scr[h * hd:(h + 1) * hd, :] = acc_scr[h * hd:(h + 1) * hd, :] / l_scr[h:h + 1, :]
    o_ref[0] = acc_scr[...].T.astype(BF16)


def _moba_prompt(q, k16, vt, km, bias_t, *, batch, n_heads, hd):
    n, d_b = q.shape
    t = n // batch
    nb = t // MOBA_BLOCK
    blk = MOBA_BLOCK
    body = functools.partial(_moba_prompt_kernel, nb=nb, n_heads=n_heads, hd=hd)
    return pl.pallas_call(
        body,
        grid=(batch, nb),
        in_specs=[
            pl.BlockSpec((1, blk, d_b), lambda b, i: (b * nb + i, 0, 0)),
            pl.BlockSpec((nb, blk, d_b), lambda b, i: (b, 0, 0)),
            pl.BlockSpec((nb, d_b, blk), lambda b, i: (b, 0, 0)),
            pl.BlockSpec((1, nb, d_b), lambda b, i: (b, 0, 0)),
            _resident((n_heads, 2, blk, blk), lambda b, i: (0, 0, 0, 0)),
        ],
        out_specs=pl.BlockSpec((1, blk, d_b), lambda b, i: (b * nb + i, 0, 0)),
        out_shape=jax.ShapeDtypeStruct((n // blk, blk, d_b), BF16),
        scratch_shapes=[
            pltpu.VMEM((n_heads, blk, LANES), BF16),
            pltpu.VMEM((n_heads, nb, blk), F32),
            pltpu.VMEM((n_heads, blk), F32),
            pltpu.VMEM((n_heads, blk), F32),
            pltpu.VMEM((n_heads * hd, blk), F32),
        ],
        compiler_params=_params("arbitrary", "arbitrary"),
        name="moba_prompt",
    )(q.reshape(n // blk, blk, d_b), k16, vt, km.reshape(batch, nb, d_b), bias_t)


def _moba_sample_kernel(pt_ref, q_ref, kn_ref, vn_ref, bown_ref, btail_ref, hmask_ref, gsum_ref, *rest,
                        n_chunks, n_heads, t_new, page):
    del pt_ref
    pps = PAGES_PER_STEP
    k_refs = rest[:pps]
    v_refs = rest[pps:2 * pps]
    o_ref = rest[2 * pps]
    s_scr, km_scr, acc_scr, l_scr, kv_scr = rest[2 * pps + 1:]
    c = pl.program_id(1)
    cols_page = page * n_heads
    blocks_per_chunk = pps * page // MOBA_BLOCK
    pages_per_block = MOBA_BLOCK // page
    cols_block = MOBA_BLOCK * n_heads
    nbp = n_chunks * blocks_per_chunk
    hd = q_ref.shape[-1]

    @pl.when(c == 0)
    def _():
        km_scr[...] = jnp.zeros(km_scr.shape, F32)

    @pl.when(c < n_chunks)
    def _():
        q16 = q_ref[0].astype(BF16)
        psum = None
        for p in range(pps):
            kp = k_refs[p][0, 0]
            k2 = kp.reshape(cols_page, hd)
            s_scr[c, :, p * cols_page:(p + 1) * cols_page] = _dot_nt(q16, k2.astype(BF16))
            ps = jnp.sum(kp, axis=0)
            psum = ps if p % pages_per_block == 0 else psum + ps
            if p % pages_per_block == pages_per_block - 1:
                blk = c * blocks_per_chunk + p // pages_per_block
                km_scr[pl.ds(pl.multiple_of(blk * n_heads, n_heads), n_heads), :] = psum * (1.0 / MOBA_BLOCK)

    @pl.when(c == n_chunks)
    def _():
        q = q_ref[0]
        bs = _dot_nt_f32(q, km_scr[...])
        row_h = lax.broadcasted_iota(jnp.int32, bs.shape, 0) % n_heads
        col = lax.broadcasted_iota(jnp.int32, bs.shape, 1)
        bs = jnp.where((col % n_heads == row_h) & (col < nbp * n_heads), bs, -jnp.inf)
        sel = _top_k_mask(bs, col, 1, MOBA_TOPK, bs.shape[1])
        picked = _dot(jnp.where(sel, 1.0, 0.0).astype(BF16), gsum_ref[...])
        blk_bias = jnp.where(picked > 0.5, 0.0, NEG)

        kv_scr[...] = jnp.zeros(kv_scr.shape, F32)
        kv_scr[0:kn_ref.shape[1], :] = kn_ref[0]
        s_own = _dot_nt(q.astype(BF16), kv_scr[...].astype(BF16)) + bown_ref[...]
        m = jnp.max(s_own, axis=-1, keepdims=True)

        hmask = hmask_ref[...]
        for b in range(nbp):
            cc, off = divmod(b, blocks_per_chunk)
            cols = slice(off * cols_block, (off + 1) * cols_block)
            s = s_scr[cc, :, cols] + hmask + blk_bias[:, b:b + 1]
            if b == nbp - 1:
                s = s + btail_ref[...]
            s_scr[cc, :, cols] = s
            m = jnp.maximum(m, jnp.max(s, axis=-1, keepdims=True))

        p_own = jnp.exp(s_own - m)
        l = jnp.sum(p_own, axis=-1, keepdims=True)
        for b in range(nbp):
            cc, off = divmod(b, blocks_per_chunk)
            cols = slice(off * cols_block, (off + 1) * cols_block)
            pr = jnp.exp(s_scr[cc, :, cols] - m)
            l = l + jnp.sum(pr, axis=-1, keepdims=True)
            s_scr[cc, :, cols] = pr
        kv_scr[0:vn_ref.shape[1], :] = vn_ref[0]
        acc_scr[...] = _dot(p_own.astype(BF16), kv_scr[...].astype(BF16))
        l_scr[...] = jnp.broadcast_to(l, l_scr.shape)

    @pl.when(c >= n_chunks)
    def _():
        cc = c - n_chunks
        acc = acc_scr[...]
        for p in range(pps):
            pr = s_scr[cc, :, p * cols_page:(p + 1) * cols_page]
            v2 = v_refs[p][0, 0].reshape(cols_page, hd)
            acc = acc + _dot(pr.astype(BF16), v2.astype(BF16))
        acc_scr[...] = acc

    @pl.when(c == 2 * n_chunks - 1)
    def _():
        o_ref[0] = (acc_scr[...] / l_scr[:, 0:1]).astype(BF16)


def _moba_sample(page_table, q, k_new, v_new, bias_own, bias_tail, head_mask, block_sum,
                 cache_k, cache_v, layer, *, t_new):
    b, rows, hd = q.shape
    _, _, page, n_heads, _ = cache_k.shape
    n_pages = page_table.shape[1]
    pps = PAGES_PER_STEP
    n_chunks = n_pages // pps
    assert n_pages % pps == 0 and (pps * page) % MOBA_BLOCK == 0 and MOBA_BLOCK % page == 0
    cols_chunk = pps * page * n_heads

    def k_map(r):
        return lambda bi, c, pt: (layer, pt[bi, jnp.minimum(c, n_chunks - 1) * pps + r], 0, 0, 0)

    def v_map(r):
        return lambda bi, c, pt: (layer, pt[bi, jnp.maximum(c - n_chunks, 0) * pps + r], 0, 0, 0)

    per_b = lambda bi, c, pt: (bi, 0, 0)
    const2 = lambda bi, c, pt: (0, 0)
    in_specs = [pl.BlockSpec((1, rows, hd), per_b)] * 3
    in_specs += [pl.BlockSpec(a.shape, const2) for a in (bias_own, bias_tail, head_mask, block_sum)]
    in_specs += [pl.BlockSpec((1, 1, page, n_heads, hd), k_map(r)) for r in range(pps)]
    in_specs += [pl.BlockSpec((1, 1, page, n_heads, hd), v_map(r)) for r in range(pps)]
    body = functools.partial(_moba_sample_kernel, n_chunks=n_chunks, n_heads=n_heads, t_new=t_new, page=page)
    grid_spec = pltpu.PrefetchScalarGridSpec(
        num_scalar_prefetch=1,
        grid=(b, 2 * n_chunks),
        in_specs=in_specs,
        out_specs=pl.BlockSpec((1, rows, hd), per_b),
        scratch_shapes=[
            pltpu.VMEM((n_chunks, rows, cols_chunk), F32),
            pltpu.VMEM((block_sum.shape[0], hd), F32),
            pltpu.VMEM((rows, hd), F32),
            pltpu.VMEM((rows, LANES), F32),
            pltpu.VMEM((LANES, hd), F32),
        ],
    )
    return pl.pallas_call(
        body,
        grid_spec=grid_spec,
        out_shape=jax.ShapeDtypeStruct((b, rows, hd), BF16),
        compiler_params=_params("arbitrary", "arbitrary"),
        name="moba_sample",
    )(page_table, q, k_new, v_new, bias_own, bias_tail, head_mask, block_sum,
      *([cache_k] * pps), *([cache_v] * pps))


def _gla_kernel(gq_ref, gk_ref, gv_ref, gr_ref, lr_ref, wa_ref, ba_ref, gg_ref, st0_ref,
                o_ref, stf_ref, st_scr, bc_scr, bl_scr, qd_scr, kd_scr, vt_scr, o_scr,
                *, tt, valid, n_heads):
    j = pl.program_id(1)
    c_sz = GLA_CHUNK
    hp = n_heads * LANES

    @pl.when(j == 0)
    def _():
        st_scr[...] = st0_ref[0]

    x = _dot(lr_ref[0].astype(BF16), wa_ref[0]) + ba_ref[0]
    la = _log_sigmoid(x) * (1.0 / GATE_NORM)
    if valid < tt:
        la = jnp.where(lax.broadcasted_iota(jnp.int32, (tt, hp), 0) < valid, la, 0.0)
    r_i = lax.broadcasted_iota(jnp.int32, (tt, tt), 0)
    c_i = lax.broadcasted_iota(jnp.int32, (tt, tt), 1)
    same = (r_i // c_sz) == (c_i // c_sz)
    bc = _dot_f32(jnp.where(same & (c_i <= r_i), 1.0, 0.0), la)
    bl = _dot_f32(jnp.where(same, 1.0, 0.0), la)
    bc_scr[...] = bc
    bl_scr[...] = bl
    qd_scr[...] = (gq_ref[0] * jnp.exp(bc)).astype(BF16)
    kd_scr[...] = (gk_ref[0] * jnp.exp(bl - bc)).astype(BF16)
    vt_scr[...] = gv_ref[0].T.astype(BF16)

    t_i = lax.broadcasted_iota(jnp.int32, (c_sz, LANES), 0)
    chunk_of_col = lax.broadcasted_iota(jnp.int32, (LANES, tt), 1) // c_sz

    def chunk(c, carry):
        rows = pl.ds(pl.multiple_of(c * c_sz, c_sz), c_sz)
        bc_c = bc_scr[rows, :]
        q_c = gq_ref[0, rows, :]
        k_c = gk_ref[0, rows, :]
        v_c = gv_ref[0, rows, :]
        decay = jnp.exp(bl_scr[pl.ds(c * c_sz, 1), :])
        in_chunk = chunk_of_col == c
        for h in range(n_heads):
            hs = slice(h * LANES, (h + 1) * LANES)
            st = st_scr[h]
            o_h = _dot_nt(qd_scr[rows, hs], st.astype(BF16))
            bch = bc_c[:, hs]
            qh = q_c[:, hs]
            for s in range(c_sz):
                dec = jnp.exp(jnp.where(t_i >= s, bch - bch[s:s + 1, :], -jnp.inf))
                a_s = jnp.sum(qh * k_c[s:s + 1, hs] * dec, axis=-1, keepdims=True)
                o_h = o_h + a_s * v_c[s:s + 1, hs]
            o_scr[rows, hs] = o_h
            vtm = jnp.where(in_chunk, vt_scr[hs, :], jnp.zeros((), BF16))
            st_scr[h] = decay[:, hs] * st + _dot(vtm, kd_scr[:, hs])
        return carry

    lax.fori_loop(0, (valid + c_sz - 1) // c_sz, chunk, 0)

    o = o_scr[...]
    r = gr_ref[0]
    for h in range(n_heads):
        hs = slice(h * LANES, (h + 1) * LANES)
        oh = o[:, hs]
        oh = oh * lax.rsqrt(jnp.mean(oh * oh, axis=-1, keepdims=True) + EPS)
        rh = r[:, hs]
        o_ref[0, :, hs] = ((oh * gg_ref[0, :, hs]) * (rh * jax.nn.sigmoid(rh))).astype(BF16)

    @pl.when(j == pl.num_programs(1) - 1)
    def _():
        stf_ref[0] = st_scr[...]


def _gla(gq, gk, gv, gr, lr, wa, ba, gg, st0, layer, *, tt, valid):
    b, t, hp = gq.shape
    n_heads = hp // LANES
    lay = lambda bi, j: (layer, 0, 0)
    tile = lambda bi, j: (bi, j, 0)
    per_b = lambda bi, j: (bi, 0, 0, 0)
    body = functools.partial(_gla_kernel, tt=tt, valid=valid, n_heads=n_heads)
    return pl.pallas_call(
        body,
        grid=(b, t // tt),
        in_specs=[pl.BlockSpec((1, tt, hp), tile)] * 4 + [
            pl.BlockSpec((1, tt, LANES), tile),
            pl.BlockSpec((1, LANES, hp), lay),
            pl.BlockSpec((1, 1, hp), lay),
            pl.BlockSpec((1, 1, hp), lay),
            pl.BlockSpec((1, n_heads, LANES, LANES), per_b),
        ],
        out_specs=[pl.BlockSpec((1, tt, hp), tile), pl.BlockSpec((1, n_heads, LANES, LANES), per_b)],
        out_shape=[jax.ShapeDtypeStruct((b, t, hp), BF16),
                   jax.ShapeDtypeStruct((b, n_heads, LANES, LANES), F32)],
        scratch_shapes=[
            pltpu.VMEM((n_heads, LANES, LANES), F32),
            pltpu.VMEM((tt, hp), F32),
            pltpu.VMEM((tt, hp), F32),
            pltpu.VMEM((tt, hp), BF16),
            pltpu.VMEM((tt, hp), BF16),
            pltpu.VMEM((hp, tt), BF16),
            pltpu.VMEM((tt, hp), F32),
        ],
        compiler_params=_params("arbitrary", "arbitrary"),
        name="gla",
    )(gq, gk, gv, gr, lr, wa, ba, gg, st0)


def _merge_mlp_kernel(x_ref, ya_ref, ob_ref, oc_ref, gate_ref, gt1_ref, sc2_ref, sh2_ref, gt2_ref,
                      g2_ref, gf_ref, wa_ref, wb_ref, wc_ref, wo_ref, wup_ref, wdn_ref, *outs, final):
    d = x_ref.shape[-1]
    m = (gate_ref[:, 0:d] * _dot(ya_ref[...], wa_ref[0])
         + gate_ref[:, d:2 * d] * _dot(ob_ref[...], wb_ref[0])
         + gate_ref[:, 2 * d:3 * d] * _dot(oc_ref[...], wc_ref[0]))
    x1 = x_ref[...] + gt1_ref[0] * _dot(m.astype(BF16), wo_ref[0])
    h2 = _rms_rows(x1, g2_ref[0]) * (1.0 + sc2_ref[0]) + sh2_ref[0]
    up = jnp.maximum(_dot(h2.astype(BF16), wup_ref[0]), 0.0)
    x2 = x1 + gt2_ref[0] * _dot((up * up).astype(BF16), wdn_ref[0])
    outs[0][...] = x2
    if final:
        outs[1][...] = _rms_rows(x2, gf_ref[...])


def _merge_mlp(x, ya, ob, oc, gates, gt1, sc2, sh2, gt2, g2, gf, wa, wb, wc, wo, wup, wdn, layer,
               *, tm, tiles_per_group, final):
    n, d = x.shape
    c = ya.shape[-1]
    dff = wup.shape[-1]
    r = gt1.shape[1]
    row = lambda i: (i, 0)
    grp = lambda i: (i // tiles_per_group, 0, 0)
    lay = lambda i: (layer, 0, 0)
    n_out = 2 if final else 1
    return pl.pallas_call(
        functools.partial(_merge_mlp_kernel, final=final),
        grid=(n // tm,),
        in_specs=[pl.BlockSpec((tm, d), row)] + [pl.BlockSpec((tm, c), row)] * 3 + [
            pl.BlockSpec((tm, 3 * d), row),
            pl.BlockSpec((1, r, d), grp), pl.BlockSpec((1, r, d), grp),
            pl.BlockSpec((1, r, d), grp), pl.BlockSpec((1, r, d), grp),
            pl.BlockSpec((1, 1, d), lay),
            pl.BlockSpec((1, d), lambda i: (0, 0)),
            _resident((1, c, d), lay), _resident((1, c, d), lay), _resident((1, c, d), lay),
            _resident((1, d, d), lay), _resident((1, d, dff), lay), _resident((1, dff, d), lay),
        ],
        out_specs=[pl.BlockSpec((tm, d), row)] * n_out,
        out_shape=[jax.ShapeDtypeStruct((n, d), F32)] * n_out,
        compiler_params=_params("arbitrary"),
        name="merge_mlp",
    )(x, ya, ob, oc, gates, gt1, sc2, sh2, gt2, g2, gf, wa, wb, wc, wo, wup, wdn)


def _t5_bucket_table(n_buckets, max_dist):
    dist = np.arange(max_dist + 1)
    max_exact = n_buckets // 2
    d = np.maximum(dist, 1).astype(np.float32)
    large = max_exact + (np.log(d / max_exact) / math.log(MAX_DISTANCE / max_exact)
                         * (n_buckets - max_exact)).astype(np.int32)
    large = np.minimum(large, n_buckets - 1)
    return np.where(dist < max_exact, dist, large).astype(np.int32)


def _pad_heads(w, n_heads):
    k = w.shape[-1] // n_heads
    w = w.reshape(w.shape[:-1] + (n_heads, k))
    w = jnp.pad(w, [(0, 0)] * (w.ndim - 1) + [(0, LANES - k)])
    return w.reshape(w.shape[:-2] + (n_heads * LANES,))


def kernel(x_prompt, x_sample, c_prompt, c_sample, cache_k, cache_v, page_table, state_conv, state_gla,
           w_ada, b_ada, g_norm1, w_in, w_dw, b_dw, ln_g, ln_b, w_pw2, w_pb, rel_bias, w_a2, b_a, g_gla,
           w_pc, w_o, g_norm2, w_up, w_down, g_final):
    bp, t, d = x_prompt.shape
    bs, ts, _ = x_sample.shape
    depth = w_in.shape[0]
    _, n_pool, page, h_b, hd_b = cache_k.shape
    d_b = h_b * hd_b
    d_conv = w_dw.shape[-1]
    width = w_dw.shape[1]
    rank = w_a2.shape[1]
    h_c = GLA_HEADS
    dk_c = w_a2.shape[-1] // h_c
    dv_c = state_gla.shape[-1]
    hp = h_c * LANES
    n_buckets = rel_bias.shape[0]
    n_pages = page_table.shape[1]
    past_len = n_pages * page
    assert dv_c == LANES and h_c * dv_c == hp and width - 1 <= CONV_HALO

    sizes = (2 * d_conv, d_b, d_b, d_b, h_c * dk_c, h_c * dk_c, h_c * dv_c, h_c * dv_c, rank, d, d, d)
    offs = np.cumsum((0,) + sizes)
    part = [w_in[:, :, offs[i]:offs[i + 1]] for i in range(len(sizes))]
    w_cat = jnp.concatenate([
        part[0], part[1], part[2], part[3],
        _pad_heads(part[4], h_c), _pad_heads(part[5], h_c), part[6], part[7],
        jnp.pad(part[8], ((0, 0), (0, 0), (0, LANES - rank))),
        part[9], part[10], part[11]], axis=-1).astype(BF16)
    wa2 = jnp.pad(_pad_heads(w_a2, h_c), ((0, 0), (0, LANES - rank), (0, 0))).astype(BF16)
    ba2 = _pad_heads(b_a, h_c).reshape(depth, 1, hp)
    gg = g_gla.reshape(depth, 1, hp)
    wpw2, wpb, wpc, wo, wup, wdn = (w.astype(BF16) for w in (w_pw2, w_pb, w_pc, w_o, w_up, w_down))
    g1 = g_norm1.reshape(depth, 1, d)
    g2 = g_norm2.reshape(depth, 1, d)
    gf = g_final.reshape(1, d)
    bdw = b_dw.reshape(depth, 1, d_conv)
    lng = ln_g.reshape(depth, 1, d_conv)
    lnb = ln_b.reshape(depth, 1, d_conv)

    n_c = bp + bs
    rows_c = -(-n_c // 8) * 8
    c_all = jnp.pad(jnp.concatenate([c_prompt, c_sample], axis=0), ((0, rows_c - n_c), (0, 0)))
    mods = _adaln(c_all, w_ada, b_ada).reshape(depth, rows_c, 6, d)

    blk = MOBA_BLOCK
    assert ts <= blk and past_len % blk == 0
    bucket = _t5_bucket_table(n_buckets, 2 * blk - 1)
    onehot = np.zeros((4 * blk, n_buckets), np.float32)
    onehot[np.arange(2 * blk), bucket] = 1.0
    by_dist = jnp.dot(onehot, rel_bias - rel_bias[n_buckets - 1:n_buckets, :],
                      precision=lax.Precision.HIGHEST).T
    toep = jnp.tile(by_dist, (1, blk))[:, :blk * (4 * blk - 1)].reshape(h_b, blk, 4 * blk - 1)
    bias_t = jnp.stack([toep[:, :, :blk], toep[:, :, blk:2 * blk]], axis=1)
    rows_s = ts * h_b
    same_head = np.eye(h_b, dtype=np.float32)
    own = jnp.transpose(toep[:, :ts, :ts], (2, 0, 1))
    own = (own[:, :, :, None] * same_head[None, :, None, :]).reshape(rows_s, rows_s)
    own_ok = (np.arange(ts)[:, None, None, None] >= np.arange(ts)[None, None, :, None]) & (
        same_head[None, :, None, :] > 0)
    own_ok = np.pad(own_ok.reshape(rows_s, rows_s), ((0, 0), (0, LANES - rows_s)))
    bias_own = jnp.where(own_ok, jnp.pad(own, ((0, 0), (0, LANES - rows_s))), NEG)
    tail = jnp.transpose(toep[:, :, blk:blk + ts], (2, 0, 1))
    bias_tail = (tail[:, :, :, None] * same_head[None, :, None, :]).reshape(rows_s, blk * h_b)
    head_mask = np.where(np.arange(rows_s)[:, None] % h_b == np.arange(blk * h_b)[None, :] % h_b,
                         0.0, NEG).astype(np.float32)
    nbp = past_len // blk
    km_rows = max(nbp * h_b, LANES)
    block_sum = (np.arange(km_rows)[:, None] // h_b == np.arange(LANES)[None, :]).astype(np.float32)
    block_sum = jnp.asarray(block_sum, BF16)
    assert nbp <= LANES and rows_s <= LANES

    xp = x_prompt.reshape(bp * t, d)
    xs = x_sample.reshape(bs * ts, d)
    n_s = bs * ts
    tm = TOKEN_TILE
    tpb = t // tm
    dims = (d_conv, d_b, hp, hd_b, dk_c)
    ts_pad = 8
    gla_rows_s = LANES

    kp_l, vp_l, cp_l, sp_l, ks_l, vs_l, cs_l, ss_l = ([] for _ in range(8))
    y_prompt = y_sample = None
    zeros_hist = jnp.zeros((bp, CONV_HALO, d_conv), F32)
    zeros_state = jnp.zeros((bp, h_c, LANES, LANES), F32)
    for l in range(depth):
        final = l == depth - 1
        mp = [mods[l, :bp, i].reshape(bp, 1, d) for i in range(6)]
        ms = [jnp.repeat(mods[l, bp:n_c, i], ts, axis=0).reshape(1, n_s, d) for i in range(6)]

        (u, qb, kb, vb, k16, vt, km, gq, gk, gv, gr, lr, gates) = _inproj(
            xp, mp[1], mp[0], g1, w_cat, l, tm=tm, tiles_per_group=tpb, dims=dims, attn_aux=True)
        ya, tail = _conv_prompt(u.reshape(bp, t, d_conv), zeros_hist, w_dw, bdw, lng, lnb, l, tt=tm)
        ob = _moba_prompt(qb, k16, vt, km, bias_t, batch=bp, n_heads=h_b, hd=hd_b)
        oc, stf = _gla(*(a.reshape(bp, t, -1) for a in (gq, gk, gv, gr, lr)), wa2, ba2, gg, zeros_state, l,
                       tt=tm, valid=tm)
        res = _merge_mlp(xp, ya.reshape(bp * t, d_conv), ob.reshape(bp * t, d_b), oc.reshape(bp * t, hp),
                         gates, mp[2], mp[4], mp[3], mp[5], g2, gf, wpw2, wpb, wpc, wo, wup, wdn, l,
                         tm=tm, tiles_per_group=tpb, final=final)
        xp = res[0]
        if final:
            y_prompt = res[1].reshape(bp, t, d)
        kp_l.append(kb.reshape(bp, t, h_b, hd_b))
        vp_l.append(vb.reshape(bp, t, h_b, hd_b))
        cp_l.append(tail[:, CONV_HALO - (width - 1):])
        sp_l.append(jnp.swapaxes(stf[..., :dk_c], -1, -2))

        (u, qb, kb, vb, gq, gk, gv, gr, lr, gates) = _inproj(
            xs, ms[1], ms[0], g1, w_cat, l, tm=n_s, tiles_per_group=1, dims=dims, attn_aux=False)
        ext = jnp.concatenate([state_conv[l], u.reshape(bs, ts, d_conv)], axis=1)
        ext_p = jnp.pad(ext, ((0, 0), (0, ts_pad + width - 1 - ext.shape[1] + (-(ts_pad + width - 1)) % 8), (0, 0)))
        ya = _conv_sample(ext_p, w_dw, bdw, lng, lnb, l, rows=ts_pad)[:, :ts]
        by_head = lambda a: a.reshape(bs, rows_s, hd_b)
        ob = _moba_sample(page_table, by_head(qb), by_head(kb), by_head(vb), bias_own, bias_tail,
                          head_mask, block_sum, cache_k, cache_v, l, t_new=ts)
        pad_t = lambda a: jnp.pad(a.reshape(bs, ts, -1), ((0, 0), (0, gla_rows_s - ts), (0, 0)))
        st0 = jnp.pad(jnp.swapaxes(state_gla[l], -1, -2), ((0, 0), (0, 0), (0, 0), (0, LANES - dk_c)))
        oc, stf = _gla(pad_t(gq), pad_t(gk), pad_t(gv), pad_t(gr), pad_t(lr), wa2, ba2, gg, st0, l,
                       tt=gla_rows_s, valid=ts)
        res = _merge_mlp(xs, ya.reshape(n_s, d_conv), ob.reshape(n_s, d_b), oc[:, :ts].reshape(n_s, hp),
                         gates, ms[2], ms[4], ms[3], ms[5], g2, gf, wpw2, wpb, wpc, wo, wup, wdn, l,
                         tm=n_s, tiles_per_group=1, final=final)
        xs = res[0]
        if final:
            y_sample = res[1].reshape(bs, ts, d)
        ks_l.append(kb.reshape(bs, ts, h_b, hd_b))
        vs_l.append(vb.reshape(bs, ts, h_b, hd_b))
        cs_l.append(ext[:, -(width - 1):])
        ss_l.append(jnp.swapaxes(stf[..., :dk_c], -1, -2))

    return (y_prompt, y_sample, jnp.stack(kp_l), jnp.stack(vp_l), jnp.stack(cp_l), jnp.stack(sp_l),
            jnp.stack(ks_l), jnp.stack(vs_l), jnp.stack(cs_l), jnp.stack(ss_l))
```

```python
import functools
import math

import numpy as np
import jax
import jax.numpy as jnp
from jax import lax
from jax.experimental import pallas as pl
from jax.experimental.pallas import tpu as pltpu

F32 = jnp.float32
BF16 = jnp.bfloat16

MOBA_BLOCK = 256
MOBA_TOPK = 3
MAX_DISTANCE = 128
GLA_HEADS = 4
GLA_CHUNK = 16
GATE_NORM = 16.0
EPS = 1e-6

LANES = 128
VMEM_LIMIT = 56 * 1024 * 1024

NEG = -1e30
TOKEN_TILE = 256
PAGES_PER_STEP = 16
CONV_HALO = 32

_NT = (((1,), (1,)), ((), ()))


def _dot(a, b):
    return jnp.dot(a, b, preferred_element_type=F32)


def _dot_nt(a, b):
    return lax.dot_general(a, b, _NT, preferred_element_type=F32)


def _dot_f32(a, b):
    return jnp.dot(a, b, precision=lax.Precision.HIGHEST, preferred_element_type=F32)


def _dot_nt_f32(a, b):
    return lax.dot_general(a, b, _NT, precision=lax.Precision.HIGHEST, preferred_element_type=F32)


def _params(*sem):
    return pltpu.CompilerParams(dimension_semantics=sem, vmem_limit_bytes=VMEM_LIMIT)


def _resident(shape, index_map):
    return pl.BlockSpec(shape, index_map, pipeline_mode=pl.Buffered(1))


def _rms_rows(x, g):
    return (x * lax.rsqrt(jnp.mean(x * x, axis=-1, keepdims=True) + EPS)) * g


def _log_sigmoid(x):
    return jnp.minimum(x, 0.0) - jnp.log1p(jnp.exp(-jnp.abs(x)))


def _top_k_mask(s, idx, axis, k, sentinel):
    sel = jnp.zeros(s.shape, dtype=jnp.bool_)
    for _ in range(k):
        mx = jnp.max(s, axis=axis, keepdims=True)
        first = jnp.min(jnp.where(s == mx, idx, sentinel), axis=axis, keepdims=True)
        pick = (idx == first) & (mx > -jnp.inf)
        sel = sel | pick
        s = jnp.where(pick, -jnp.inf, s)
    return sel


def _adaln_kernel(c_ref, w_ref, b_ref, o_ref):
    c = c_ref[...]
    s = c * jax.nn.sigmoid(c)
    o_ref[0] = _dot(s.astype(BF16), w_ref[0].astype(BF16)) + b_ref[0]


def _adaln(c_all, w_ada, b_ada):
    depth, d, n = w_ada.shape
    rows = c_all.shape[0]
    tn = 1536
    return pl.pallas_call(
        _adaln_kernel,
        grid=(depth, n // tn),
        in_specs=[
            pl.BlockSpec((rows, d), lambda l, j: (0, 0)),
            pl.BlockSpec((1, d, tn), lambda l, j: (l, 0, j)),
            pl.BlockSpec((1, 1, tn), lambda l, j: (l, 0, j)),
        ],
        out_specs=pl.BlockSpec((1, rows, tn), lambda l, j: (l, 0, j)),
        out_shape=jax.ShapeDtypeStruct((depth, rows, n), F32),
        compiler_params=_params("arbitrary", "arbitrary"),
        name="adaln",
    )(c_all, w_ada, b_ada.reshape(depth, 1, n))


def _inproj_kernel(x_ref, sc_ref, sh_ref, g_ref, w_ref, *outs, d_conv, d_b, hp, d_model,
                   q_scale, gq_scale, attn_aux):
    if attn_aux:
        (u_ref, q_ref, k_ref, v_ref, k16_ref, vt_ref, km_ref,
         gq_ref, gk_ref, gv_ref, gr_ref, lr_ref, gate_ref) = outs
    else:
        (u_ref, q_ref, k_ref, v_ref, gq_ref, gk_ref, gv_ref, gr_ref, lr_ref, gate_ref) = outs
    x = x_ref[...]
    h = _rms_rows(x, g_ref[0]) * (1.0 + sc_ref[0]) + sh_ref[0]
    hb = h.astype(BF16)

    pos = [0]

    def seg(width):
        lo = pos[0]
        pos[0] = lo + width
        return _dot(hb, w_ref[0, :, lo:lo + width])

    a = seg(d_conv)
    g = seg(d_conv)
    u_ref[...] = a * jax.nn.sigmoid(g)
    q_ref[...] = seg(d_b) * q_scale
    k = seg(d_b)
    k_ref[...] = k
    v = seg(d_b)
    v_ref[...] = v
    if attn_aux:
        k16_ref[0] = k.astype(BF16)
        vt_ref[0] = v.T.astype(BF16)
        km_ref[0] = jnp.mean(k, axis=0, keepdims=True)
    gq_ref[...] = seg(hp) * gq_scale
    gk_ref[...] = seg(hp)
    gv_ref[...] = seg(hp)
    gr_ref[...] = seg(hp)
    lr_ref[...] = seg(LANES)
    for i in range(3):
        gate_ref[:, i * d_model:(i + 1) * d_model] = jax.nn.sigmoid(seg(d_model))


def _inproj(x, sc, sh, g1, w_cat, layer, *, tm, tiles_per_group, dims, attn_aux):
    n, d = x.shape
    d_conv, d_b, hp, hd_b, dk_c = dims
    nw = w_cat.shape[-1]
    r = sc.shape[1]
    grid = (n // tm,)
    row = lambda i: (i, 0)
    grp = lambda i: (i // tiles_per_group, 0, 0)
    lay = lambda i: (layer, 0, 0)
    outs = [((n, d_conv), F32), ((n, d_b), F32), ((n, d_b), F32), ((n, d_b), F32)]
    specs = [pl.BlockSpec((tm, d_conv), row)] + [pl.BlockSpec((tm, d_b), row)] * 3
    if attn_aux:
        assert tm == MOBA_BLOCK
        nblk = n // MOBA_BLOCK
        outs += [((nblk, MOBA_BLOCK, d_b), BF16), ((nblk, d_b, MOBA_BLOCK), BF16), ((nblk, 1, d_b), F32)]
        specs += [pl.BlockSpec((1, MOBA_BLOCK, d_b), lambda i: (i, 0, 0)),
                  pl.BlockSpec((1, d_b, MOBA_BLOCK), lambda i: (i, 0, 0)),
                  pl.BlockSpec((1, 1, d_b), lambda i: (i, 0, 0))]
    outs += [((n, hp), F32)] * 4 + [((n, LANES), F32), ((n, 3 * d), F32)]
    specs += [pl.BlockSpec((tm, hp), row)] * 4 + [pl.BlockSpec((tm, LANES), row), pl.BlockSpec((tm, 3 * d), row)]
    body = functools.partial(_inproj_kernel, d_conv=d_conv, d_b=d_b, hp=hp, d_model=d,
                             q_scale=hd_b ** -0.5, gq_scale=dk_c ** -0.5, attn_aux=attn_aux)
    return pl.pallas_call(
        body,
        grid=grid,
        in_specs=[
            pl.BlockSpec((tm, d), row),
            pl.BlockSpec((1, r, d), grp),
            pl.BlockSpec((1, r, d), grp),
            pl.BlockSpec((1, 1, d), lay),
            _resident((1, d, nw), lay),
        ],
        out_specs=specs,
        out_shape=[jax.ShapeDtypeStruct(s, t) for s, t in outs],
        compiler_params=_params("arbitrary"),
        name="inproj",
    )(x, sc, sh, g1, w_cat)


def _conv_taps(ext_ref, start, rows, wdw_ref, bdw_ref, lng_ref, lnb_ref, width):
    acc = jnp.zeros((rows, wdw_ref.shape[-1]), F32) + bdw_ref[0]
    for k in range(width):
        acc = acc + ext_ref[pl.ds(start + k, rows), :] * wdw_ref[0, k:k + 1, :]
    mu = jnp.mean(acc, axis=-1, keepdims=True)
    cen = acc - mu
    var = jnp.mean(cen * cen, axis=-1, keepdims=True)
    y = (cen * lax.rsqrt(var + EPS)) * lng_ref[0] + lnb_ref[0]
    return y * jax.nn.sigmoid(y)


def _conv_kernel(u_ref, hist_ref, wdw_ref, bdw_ref, lng_ref, lnb_ref, y_ref, tail_ref, ext_scr, *, tt, width):
    j = pl.program_id(1)

    @pl.when(j == 0)
    def _():
        ext_scr[0:CONV_HALO, :] = hist_ref[0]

    @pl.when(j > 0)
    def _():
        ext_scr[0:CONV_HALO, :] = ext_scr[tt:tt + CONV_HALO, :]

    ext_scr[CONV_HALO:CONV_HALO + tt, :] = u_ref[0]
    start = CONV_HALO - (width - 1)
    y_ref[0] = _conv_taps(ext_scr, start, tt, wdw_ref, bdw_ref, lng_ref, lnb_ref, width).astype(BF16)

    @pl.when(j == pl.num_programs(1) - 1)
    def _():
        tail_ref[0] = ext_scr[tt:tt + CONV_HALO, :]


def _conv_prompt(u, hist, w_dw, b_dw, ln_g, ln_b, layer, *, tt):
    b, t, c = u.shape
    width = w_dw.shape[1]
    lay = lambda bi, j: (layer, 0, 0)
    return pl.pallas_call(
        functools.partial(_conv_kernel, tt=tt, width=width),
        grid=(b, t // tt),
        in_specs=[
            pl.BlockSpec((1, tt, c), lambda bi, j: (bi, j, 0)),
            pl.BlockSpec((1, CONV_HALO, c), lambda bi, j: (bi, 0, 0)),
            pl.BlockSpec((1, width, c), lay),
            pl.BlockSpec((1, 1, c), lay),
            pl.BlockSpec((1, 1, c), lay),
            pl.BlockSpec((1, 1, c), lay),
        ],
        out_specs=[pl.BlockSpec((1, tt, c), lambda bi, j: (bi, j, 0)),
                   pl.BlockSpec((1, CONV_HALO, c), lambda bi, j: (bi, 0, 0))],
        out_shape=[jax.ShapeDtypeStruct((b, t, c), BF16), jax.ShapeDtypeStruct((b, CONV_HALO, c), F32)],
        scratch_shapes=[pltpu.VMEM((CONV_HALO + tt, c), F32)],
        compiler_params=_params("arbitrary", "arbitrary"),
        name="conv_prompt",
    )(u, hist, w_dw, b_dw, ln_g, ln_b)


def _conv_small_kernel(ext_ref, wdw_ref, bdw_ref, lng_ref, lnb_ref, y_ref, *, rows, width):
    y_ref[0] = _conv_taps(ext_ref.at[0], 0, rows, wdw_ref, bdw_ref, lng_ref, lnb_ref, width).astype(BF16)


def _conv_sample(ext, w_dw, b_dw, ln_g, ln_b, layer, *, rows):
    b, r, c = ext.shape
    width = w_dw.shape[1]
    lay = lambda bi: (layer, 0, 0)
    return pl.pallas_call(
        functools.partial(_conv_small_kernel, rows=rows, width=width),
        grid=(b,),
        in_specs=[
            pl.BlockSpec((1, r, c), lambda bi: (bi, 0, 0)),
            pl.BlockSpec((1, width, c), lay),
            pl.BlockSpec((1, 1, c), lay),
            pl.BlockSpec((1, 1, c), lay),
            pl.BlockSpec((1, 1, c), lay),
        ],
        out_specs=pl.BlockSpec((1, rows, c), lambda bi: (bi, 0, 0)),
        out_shape=jax.ShapeDtypeStruct((b, rows, c), BF16),
        compiler_params=_params("arbitrary"),
        name="conv_sample",
    )(ext, w_dw, b_dw, ln_g, ln_b)


def _moba_prompt_kernel(q_ref, k16_ref, vt_ref, km_ref, bias_ref, o_ref,
                        qm_scr, sel_scr, m_scr, l_scr, acc_scr, s_scr, *, nb, n_heads, hd):
    i = pl.program_id(1)
    blk = MOBA_BLOCK
    q = q_ref[0]
    km = km_ref[0]
    lane = lax.broadcasted_iota(jnp.int32, (blk, LANES), 1)
    blk_idx = lax.broadcasted_iota(jnp.int32, (nb, blk), 0)
    heads_per_slab = LANES // hd

    for h in range(n_heads):
        p, w = divmod(h, heads_per_slab)
        qm = jnp.where((lane // hd) == w, q[:, p * LANES:(p + 1) * LANES], 0.0)
        qm_scr[h] = qm.astype(BF16)
        s = _dot_nt_f32(km[:, p * LANES:(p + 1) * LANES], qm)
        s = jnp.where(blk_idx < i, s, -jnp.inf)
        sel = _top_k_mask(s, blk_idx, 0, MOBA_TOPK, nb)
        sel_scr[h] = jnp.where(sel, 0.0, NEG)

    krow = lax.broadcasted_iota(jnp.int32, (blk, blk), 0)
    qcol = lax.broadcasted_iota(jnp.int32, (blk, blk), 1)

    def block(n, kind):
        kb = k16_ref[n]
        vb = vt_ref[n]

        for h in range(n_heads):
            p = h // heads_per_slab
            s_scr[h] = _dot_nt(kb[:, p * LANES:(p + 1) * LANES], qm_scr[h])
        for h in range(n_heads):
            s = s_scr[h]
            vh = vb[h * hd:(h + 1) * hd, :]
            if kind == "own":
                s = jnp.where(krow <= qcol, s + bias_ref[h, 0], NEG)
                m = jnp.max(s, axis=0, keepdims=True)
                pr = jnp.exp(s - m)
                m_scr[h:h + 1, :] = m
                l_scr[h:h + 1, :] = jnp.sum(pr, axis=0, keepdims=True)
                acc_scr[h * hd:(h + 1) * hd, :] = _dot(vh, pr.astype(BF16))
            else:
                s = s + sel_scr[h, pl.ds(n, 1), :]
                if kind == "prev":
                    s = s + bias_ref[h, 1]
                m_old = m_scr[h:h + 1, :]
                m = jnp.maximum(m_old, jnp.max(s, axis=0, keepdims=True))
                alpha = jnp.exp(m_old - m)
                pr = jnp.exp(s - m)
                m_scr[h:h + 1, :] = m
                l_scr[h:h + 1, :] = alpha * l_scr[h:h + 1, :] + jnp.sum(pr, axis=0, keepdims=True)
                acc_scr[h * hd:(h + 1) * hd, :] = (alpha * acc_scr[h * hd:(h + 1) * hd, :]
                                                  + _dot(vh, pr.astype(BF16)))

    block(i, "own")

    @pl.when(i >= 1)
    def _():
        block(i - 1, "prev")

    def far(n, carry):
        block(n, "far")
        return carry

    lax.fori_loop(0, i - 1, far, 0)

    for h in range(n_heads):
        acc_scr[h * hd:(h + 1) * hd, :] = acc_scr[h * hd:(h + 1) * hd, :] / l_scr[h:h + 1, :]
    o_ref[0] = acc_scr[...].T.astype(BF16)


def _moba_prompt(q, k16, vt, km, bias_t, *, batch, n_heads, hd):
    n, d_b = q.shape
    t = n // batch
    nb = t // MOBA_BLOCK
    blk = MOBA_BLOCK
    body = functools.partial(_moba_prompt_kernel, nb=nb, n_heads=n_heads, hd=hd)
    return pl.pallas_call(
        body,
        grid=(batch, nb),
        in_specs=[
            pl.BlockSpec((1, blk, d_b), lambda b, i: (b * nb + i, 0, 0)),
            pl.BlockSpec((nb, blk, d_b), lambda b, i: (b, 0, 0)),
            pl.BlockSpec((nb, d_b, blk), lambda b, i: (b, 0, 0)),
            pl.BlockSpec((1, nb, d_b), lambda b, i: (b, 0, 0)),
            _resident((n_heads, 2, blk, blk), lambda b, i: (0, 0, 0, 0)),
        ],
        out_specs=pl.BlockSpec((1, blk, d_b), lambda b, i: (b * nb + i, 0, 0)),
        out_shape=jax.ShapeDtypeStruct((n // blk, blk, d_b), BF16),
        scratch_shapes=[
            pltpu.VMEM((n_heads, blk, LANES), BF16),
            pltpu.VMEM((n_heads, nb, blk), F32),
            pltpu.VMEM((n_heads, blk), F32),
            pltpu.VMEM((n_heads, blk), F32),
            pltpu.VMEM((n_heads * hd, blk), F32),
            pltpu.VMEM((n_heads, blk, blk), F32),
        ],
        compiler_params=_params("arbitrary", "arbitrary"),
        name="moba_prompt",
    )(q.reshape(n // blk, blk, d_b), k16, vt, km.reshape(batch, nb, d_b), bias_t)


def _moba_sample_kernel(pt_ref, q_ref, kn_ref, vn_ref, bown_ref, btail_ref, *rest,
                        n_chunks, n_heads, t_new, page, pps):
    del pt_ref
    k_refs = rest[:pps]
    v_refs = rest[pps:2 * pps]
    o_ref = rest[2 * pps]
    qbd_scr, s_scr, bsum_scr, acc_scr, l_scr, kv_scr = rest[2 * pps + 1:]
    c = pl.program_id(1)
    rows, d_b = q_ref.shape[1:]
    hd = d_b // n_heads
    blocks_per_chunk = pps * page // MOBA_BLOCK
    pages_per_block = MOBA_BLOCK // page
    nbp = n_chunks * blocks_per_chunk

    row_i = lax.broadcasted_iota(jnp.int32, (rows, d_b), 0)
    lane_i = lax.broadcasted_iota(jnp.int32, (rows, d_b), 1)
    head_mask = (lane_i // hd) == (row_i % n_heads)
    blk_lane = lax.broadcasted_iota(jnp.int32, (rows, LANES), 1)

    @pl.when(c == 0)
    def _():
        qbd_scr[...] = jnp.where(head_mask, q_ref[0], 0.0).astype(BF16)
        bsum_scr[...] = jnp.zeros(bsum_scr.shape, F32)

    @pl.when(c < n_chunks)
    def _():
        qb16 = qbd_scr[...]
        bsum = bsum_scr[...]
        for p in range(pps):
            s = _dot(qb16, k_refs[p][0, 0].astype(BF16))
            s_scr[c, :, p * page:(p + 1) * page] = s
            ps = jnp.sum(s, axis=-1, keepdims=True)
            psum = ps if p % pages_per_block == 0 else psum + ps
            if p % pages_per_block == pages_per_block - 1:
                blk = c * blocks_per_chunk + p // pages_per_block
                bsum = jnp.where(blk_lane == blk, psum, bsum)
        bsum_scr[...] = bsum

    @pl.when(c == n_chunks)
    def _():
        bs = jnp.where(blk_lane < nbp, bsum_scr[...], -jnp.inf)
        sel = _top_k_mask(bs, blk_lane, 1, MOBA_TOPK, LANES)
        blk_bias = jnp.where(sel, 0.0, NEG)

        kv_scr[...] = jnp.zeros(kv_scr.shape, F32)
        kv_scr[0:kn_ref.shape[1], :] = kn_ref[0]
        s_own = _dot_nt(qbd_scr[...], kv_scr[...].astype(BF16)) + bown_ref[...]
        m = jnp.max(s_own, axis=-1, keepdims=True)

        for b in range(nbp):
            cc, off = divmod(b, blocks_per_chunk)
            cols = slice(off * MOBA_BLOCK, (off + 1) * MOBA_BLOCK)
            s = s_scr[cc, :, cols] + blk_bias[:, b:b + 1]
            if b == nbp - 1:
                s = s + btail_ref[...]
            s_scr[cc, :, cols] = s
            m = jnp.maximum(m, jnp.max(s, axis=-1, keepdims=True))

        p_own = jnp.exp(s_own - m)
        l = jnp.sum(p_own, axis=-1, keepdims=True)
        for cc in range(n_chunks):
            pr = jnp.exp(s_scr[cc] - m)
            l = l + jnp.sum(pr, axis=-1, keepdims=True)
            s_scr[cc] = pr
        kv_scr[0:vn_ref.shape[1], :] = vn_ref[0]
        acc_scr[...] = _dot(p_own.astype(BF16), kv_scr[...].astype(BF16))
        l_scr[...] = jnp.broadcast_to(l, l_scr.shape)

    @pl.when(c >= n_chunks)
    def _():
        cc = c - n_chunks
        acc = acc_scr[...]
        for p in range(pps):
            pr = s_scr[cc, :, p * page:(p + 1) * page]
            acc = acc + _dot_nt(pr.astype(BF16), v_refs[p][0, 0].astype(BF16))
        acc_scr[...] = acc

    @pl.when(c == 2 * n_chunks - 1)
    def _():
        o = jnp.where(head_mask, acc_scr[...] / l_scr[:, 0:1], 0.0)
        o_ref[0] = jnp.sum(o.reshape(t_new, n_heads, d_b), axis=1).astype(BF16)


def _moba_sample(page_table, q_rep, k_new, v_new, bias_own, bias_tail, cache_kt, cache_vt, layer,
                 *, n_heads, t_new):
    b, rows, d_b = q_rep.shape
    page = cache_kt.shape[-1]
    n_pages = page_table.shape[1]
    pps = min(PAGES_PER_STEP, n_pages)
    n_chunks = n_pages // pps
    assert n_pages % pps == 0 and (pps * page) % MOBA_BLOCK == 0 and MOBA_BLOCK % page == 0
    chunk = pps * page

    def k_map(r):
        return lambda bi, c, pt: (layer, pt[bi, jnp.minimum(c, n_chunks - 1) * pps + r], 0, 0)

    def v_map(r):
        return lambda bi, c, pt: (layer, pt[bi, jnp.maximum(c - n_chunks, 0) * pps + r], 0, 0)

    per_b = lambda bi, c, pt: (bi, 0, 0)
    const2 = lambda bi, c, pt: (0, 0)
    in_specs = [
        pl.BlockSpec((1, rows, d_b), per_b),
        pl.BlockSpec((1, k_new.shape[1], d_b), per_b),
        pl.BlockSpec((1, v_new.shape[1], d_b), per_b),
        pl.BlockSpec(bias_own.shape, const2),
        pl.BlockSpec(bias_tail.shape, const2),
    ]
    in_specs += [pl.BlockSpec((1, 1, d_b, page), k_map(r)) for r in range(pps)]
    in_specs += [pl.BlockSpec((1, 1, d_b, page), v_map(r)) for r in range(pps)]
    body = functools.partial(_moba_sample_kernel, n_chunks=n_chunks, n_heads=n_heads, t_new=t_new,
                             page=page, pps=pps)
    grid_spec = pltpu.PrefetchScalarGridSpec(
        num_scalar_prefetch=1,
        grid=(b, 2 * n_chunks),
        in_specs=in_specs,
        out_specs=pl.BlockSpec((1, t_new, d_b), per_b),
        scratch_shapes=[
            pltpu.VMEM((rows, d_b), BF16),
            pltpu.VMEM((n_chunks, rows, chunk), F32),
            pltpu.VMEM((rows, LANES), F32),
            pltpu.VMEM((rows, d_b), F32),
            pltpu.VMEM((rows, LANES), F32),
            pltpu.VMEM((LANES, d_b), F32),
        ],
    )
    return pl.pallas_call(
        body,
        grid_spec=grid_spec,
        out_shape=jax.ShapeDtypeStruct((b, t_new, d_b), BF16),
        compiler_params=_params("arbitrary", "arbitrary"),
        name="moba_sample",
    )(page_table, q_rep, k_new, v_new, bias_own, bias_tail, *([cache_kt] * pps), *([cache_vt] * pps))


def _gla_kernel(gq_ref, gk_ref, gv_ref, gr_ref, lr_ref, wa_ref, ba_ref, gg_ref, st0_ref,
                o_ref, stf_ref, st_scr, bc_scr, bl_scr, qd_scr, kd_scr, vt_scr, o_scr,
                *, tt, valid, n_heads):
    j = pl.program_id(1)
    c_sz = GLA_CHUNK
    hp = n_heads * LANES

    @pl.when(j == 0)
    def _():
        st_scr[...] = st0_ref[0]

    x = _dot(lr_ref[0].astype(BF16), wa_ref[0]) + ba_ref[0]
    la = _log_sigmoid(x) * (1.0 / GATE_NORM)
    if valid < tt:
        la = jnp.where(lax.broadcasted_iota(jnp.int32, (tt, hp), 0) < valid, la, 0.0)
    r_i = lax.broadcasted_iota(jnp.int32, (tt, tt), 0)
    c_i = lax.broadcasted_iota(jnp.int32, (tt, tt), 1)
    same = (r_i // c_sz) == (c_i // c_sz)
    bc = _dot_f32(jnp.where(same & (c_i <= r_i), 1.0, 0.0), la)
    bl = _dot_f32(jnp.where(same, 1.0, 0.0), la)
    bc_scr[...] = bc
    bl_scr[...] = bl
    qd_scr[...] = (gq_ref[0] * jnp.exp(bc)).astype(BF16)
    kd_scr[...] = (gk_ref[0] * jnp.exp(bl - bc)).astype(BF16)
    vt_scr[...] = gv_ref[0].T.astype(BF16)

    t_i = lax.broadcasted_iota(jnp.int32, (c_sz, LANES), 0)
    chunk_of_col = lax.broadcasted_iota(jnp.int32, (LANES, tt), 1) // c_sz

    def chunk(c, carry):
        rows = pl.ds(pl.multiple_of(c * c_sz, c_sz), c_sz)
        bc_c = bc_scr[rows, :]
        q_c = gq_ref[0, rows, :]
        k_c = gk_ref[0, rows, :]
        v_c = gv_ref[0, rows, :]
        decay = jnp.exp(bl_scr[pl.ds(c * c_sz, 1), :])
        in_chunk = chunk_of_col == c
        for h in range(n_heads):
            hs = slice(h * LANES, (h + 1) * LANES)
            st = st_scr[h]
            o_h = _dot_nt(qd_scr[rows, hs], st.astype(BF16))
            bch = bc_c[:, hs]
            qh = q_c[:, hs]
            for s in range(c_sz):
                dec = jnp.exp(jnp.where(t_i >= s, bch - bch[s:s + 1, :], -jnp.inf))
                a_s = jnp.sum(qh * k_c[s:s + 1, hs] * dec, axis=-1, keepdims=True)
                o_h = o_h + a_s * v_c[s:s + 1, hs]
            o_scr[rows, hs] = o_h
            vtm = jnp.where(in_chunk, vt_scr[hs, :], jnp.zeros((), BF16))
            st_scr[h] = decay[:, hs] * st + _dot(vtm, kd_scr[:, hs])
        return carry

    lax.fori_loop(0, (valid + c_sz - 1) // c_sz, chunk, 0)

    o = o_scr[...]
    r = gr_ref[0]
    for h in range(n_heads):
        hs = slice(h * LANES, (h + 1) * LANES)
        oh = o[:, hs]
        oh = oh * lax.rsqrt(jnp.mean(oh * oh, axis=-1, keepdims=True) + EPS)
        rh = r[:, hs]
        o_ref[0, :, hs] = ((oh * gg_ref[0, :, hs]) * (rh * jax.nn.sigmoid(rh))).astype(BF16)

    @pl.when(j == pl.num_programs(1) - 1)
    def _():
        stf_ref[0] = st_scr[...]


def _gla(gq, gk, gv, gr, lr, wa, ba, gg, st0, layer, *, tt, valid):
    b, t, hp = gq.shape
    n_heads = hp // LANES
    lay = lambda bi, j: (layer, 0, 0)
    tile = lambda bi, j: (bi, j, 0)
    per_b = lambda bi, j: (bi, 0, 0, 0)
    body = functools.partial(_gla_kernel, tt=tt, valid=valid, n_heads=n_heads)
    return pl.pallas_call(
        body,
        grid=(b, t // tt),
        in_specs=[pl.BlockSpec((1, tt, hp), tile)] * 4 + [
            pl.BlockSpec((1, tt, LANES), tile),
            pl.BlockSpec((1, LANES, hp), lay),
            pl.BlockSpec((1, 1, hp), lay),
            pl.BlockSpec((1, 1, hp), lay),
            pl.BlockSpec((1, n_heads, LANES, LANES), per_b),
        ],
        out_specs=[pl.BlockSpec((1, tt, hp), tile), pl.BlockSpec((1, n_heads, LANES, LANES), per_b)],
        out_shape=[jax.ShapeDtypeStruct((b, t, hp), BF16),
                   jax.ShapeDtypeStruct((b, n_heads, LANES, LANES), F32)],
        scratch_shapes=[
            pltpu.VMEM((n_heads, LANES, LANES), F32),
            pltpu.VMEM((tt, hp), F32),
            pltpu.VMEM((tt, hp), F32),
            pltpu.VMEM((tt, hp), BF16),
            pltpu.VMEM((tt, hp), BF16),
            pltpu.VMEM((hp, tt), BF16),
            pltpu.VMEM((tt, hp), F32),
        ],
        compiler_params=_params("arbitrary", "arbitrary"),
        name="gla",
    )(gq, gk, gv, gr, lr, wa, ba, gg, st0)


def _merge_mlp_kernel(x_ref, ya_ref, ob_ref, oc_ref, gate_ref, gt1_ref, sc2_ref, sh2_ref, gt2_ref,
                      g2_ref, gf_ref, wa_ref, wb_ref, wc_ref, wo_ref, wup_ref, wdn_ref, *outs, final):
    d = x_ref.shape[-1]
    m = (gate_ref[:, 0:d] * _dot(ya_ref[...], wa_ref[0])
         + gate_ref[:, d:2 * d] * _dot(ob_ref[...], wb_ref[0])
         + gate_ref[:, 2 * d:3 * d] * _dot(oc_ref[...], wc_ref[0]))
    x1 = x_ref[...] + gt1_ref[0] * _dot(m.astype(BF16), wo_ref[0])
    h2 = _rms_rows(x1, g2_ref[0]) * (1.0 + sc2_ref[0]) + sh2_ref[0]
    up = jnp.maximum(_dot(h2.astype(BF16), wup_ref[0]), 0.0)
    x2 = x1 + gt2_ref[0] * _dot((up * up).astype(BF16), wdn_ref[0])
    outs[0][...] = x2
    if final:
        outs[1][...] = _rms_rows(x2, gf_ref[...])


def _merge_mlp(x, ya, ob, oc, gates, gt1, sc2, sh2, gt2, g2, gf, wa, wb, wc, wo, wup, wdn, layer,
               *, tm, tiles_per_group, final):
    n, d = x.shape
    c = ya.shape[-1]
    dff = wup.shape[-1]
    r = gt1.shape[1]
    row = lambda i: (i, 0)
    grp = lambda i: (i // tiles_per_group, 0, 0)
    lay = lambda i: (layer, 0, 0)
    n_out = 2 if final else 1
    return pl.pallas_call(
        functools.partial(_merge_mlp_kernel, final=final),
        grid=(n // tm,),
        in_specs=[pl.BlockSpec((tm, d), row)] + [pl.BlockSpec((tm, c), row)] * 3 + [
            pl.BlockSpec((tm, 3 * d), row),
            pl.BlockSpec((1, r, d), grp), pl.BlockSpec((1, r, d), grp),
            pl.BlockSpec((1, r, d), grp), pl.BlockSpec((1, r, d), grp),
            pl.BlockSpec((1, 1, d), lay),
            pl.BlockSpec((1, d), lambda i: (0, 0)),
            _resident((1, c, d), lay), _resident((1, c, d), lay), _resident((1, c, d), lay),
            _resident((1, d, d), lay), _resident((1, d, dff), lay), _resident((1, dff, d), lay),
        ],
        out_specs=[pl.BlockSpec((tm, d), row)] * n_out,
        out_shape=[jax.ShapeDtypeStruct((n, d), F32)] * n_out,
        compiler_params=_params("arbitrary"),
        name="merge_mlp",
    )(x, ya, ob, oc, gates, gt1, sc2, sh2, gt2, g2, gf, wa, wb, wc, wo, wup, wdn)


def _t5_bucket_table(n_buckets, max_dist):
    dist = np.arange(max_dist + 1)
    max_exact = n_buckets // 2
    d = np.maximum(dist, 1).astype(np.float32)
    large = max_exact + (np.log(d / max_exact) / math.log(MAX_DISTANCE / max_exact)
                         * (n_buckets - max_exact)).astype(np.int32)
    large = np.minimum(large, n_buckets - 1)
    return np.where(dist < max_exact, dist, large).astype(np.int32)


def _pad_heads(w, n_heads):
    k = w.shape[-1] // n_heads
    w = w.reshape(w.shape[:-1] + (n_heads, k))
    w = jnp.pad(w, [(0, 0)] * (w.ndim - 1) + [(0, LANES - k)])
    return w.reshape(w.shape[:-2] + (n_heads * LANES,))


def kernel(x_prompt, x_sample, c_prompt, c_sample, cache_k, cache_v, page_table, state_conv, state_gla,
           w_ada, b_ada, g_norm1, w_in, w_dw, b_dw, ln_g, ln_b, w_pw2, w_pb, rel_bias, w_a2, b_a, g_gla,
           w_pc, w_o, g_norm2, w_up, w_down, g_final):
    bp, t, d = x_prompt.shape
    bs, ts, _ = x_sample.shape
    depth = w_in.shape[0]
    _, n_pool, page, h_b, hd_b = cache_k.shape
    d_b = h_b * hd_b
    d_conv = w_dw.shape[-1]
    width = w_dw.shape[1]
    rank = w_a2.shape[1]
    h_c = GLA_HEADS
    dk_c = w_a2.shape[-1] // h_c
    dv_c = state_gla.shape[-1]
    hp = h_c * LANES
    n_buckets = rel_bias.shape[0]
    n_pages = page_table.shape[1]
    past_len = n_pages * page
    assert dv_c == LANES and h_c * dv_c == hp and width - 1 <= CONV_HALO

    sizes = (2 * d_conv, d_b, d_b, d_b, h_c * dk_c, h_c * dk_c, h_c * dv_c, h_c * dv_c, rank, d, d, d)
    offs = np.cumsum((0,) + sizes)
    part = [w_in[:, :, offs[i]:offs[i + 1]] for i in range(len(sizes))]
    w_cat = jnp.concatenate([
        part[0], part[1], part[2], part[3],
        _pad_heads(part[4], h_c), _pad_heads(part[5], h_c), part[6], part[7],
        jnp.pad(part[8], ((0, 0), (0, 0), (0, LANES - rank))),
        part[9], part[10], part[11]], axis=-1).astype(BF16)
    wa2 = jnp.pad(_pad_heads(w_a2, h_c), ((0, 0), (0, LANES - rank), (0, 0))).astype(BF16)
    ba2 = _pad_heads(b_a, h_c).reshape(depth, 1, hp)
    gg = g_gla.reshape(depth, 1, hp)
    wpw2, wpb, wpc, wo, wup, wdn = (w.astype(BF16) for w in (w_pw2, w_pb, w_pc, w_o, w_up, w_down))
    g1 = g_norm1.reshape(depth, 1, d)
    g2 = g_norm2.reshape(depth, 1, d)
    gf = g_final.reshape(1, d)
    bdw = b_dw.reshape(depth, 1, d_conv)
    lng = ln_g.reshape(depth, 1, d_conv)
    lnb = ln_b.reshape(depth, 1, d_conv)

    n_c = bp + bs
    rows_c = -(-n_c // 8) * 8
    c_all = jnp.pad(jnp.concatenate([c_prompt, c_sample], axis=0), ((0, rows_c - n_c), (0, 0)))
    mods = _adaln(c_all, w_ada, b_ada).reshape(depth, rows_c, 6, d)

    blk = MOBA_BLOCK
    assert ts <= blk and past_len % blk == 0
    bucket = _t5_bucket_table(n_buckets, 2 * blk - 1)
    onehot = np.zeros((4 * blk, n_buckets), np.float32)
    onehot[np.arange(2 * blk), bucket] = 1.0
    by_dist = jnp.dot(onehot, rel_bias - rel_bias[n_buckets - 1:n_buckets, :],
                      precision=lax.Precision.HIGHEST).T
    toep = jnp.tile(by_dist, (1, blk))[:, :blk * (4 * blk - 1)].reshape(h_b, blk, 4 * blk - 1)
    bias_t = jnp.stack([toep[:, :, :blk], toep[:, :, blk:2 * blk]], axis=1)
    rows_s = ts * h_b
    own = jnp.transpose(toep[:, :ts, :ts], (2, 0, 1)).reshape(rows_s, ts)
    own_ok = np.repeat(np.arange(ts), h_b)[:, None] >= np.arange(LANES)[None, :]
    bias_own = jnp.where(own_ok, jnp.pad(own, ((0, 0), (0, LANES - ts))), NEG)
    bias_tail = jnp.transpose(toep[:, :, blk:blk + ts], (2, 0, 1)).reshape(rows_s, blk)
    assert past_len // blk <= LANES
    cache_kt = jnp.transpose(cache_k, (0, 1, 3, 4, 2)).reshape(depth, n_pool, d_b, page)
    cache_vt = jnp.transpose(cache_v, (0, 1, 3, 4, 2)).reshape(depth, n_pool, d_b, page)

    xp = x_prompt.reshape(bp * t, d)
    xs = x_sample.reshape(bs * ts, d)
    n_s = bs * ts
    tm = TOKEN_TILE
    tpb = t // tm
    dims = (d_conv, d_b, hp, hd_b, dk_c)
    ts_pad = 8
    gla_rows_s = LANES

    kp_l, vp_l, cp_l, sp_l, ks_l, vs_l, cs_l, ss_l = ([] for _ in range(8))
    y_prompt = y_sample = None
    zeros_hist = jnp.zeros((bp, CONV_HALO, d_conv), F32)
    zeros_state = jnp.zeros((bp, h_c, LANES, LANES), F32)
    for l in range(depth):
        final = l == depth - 1
        mp = [mods[l, :bp, i].reshape(bp, 1, d) for i in range(6)]
        ms = [jnp.repeat(mods[l, bp:n_c, i], ts, axis=0).reshape(1, n_s, d) for i in range(6)]

        (u, qb, kb, vb, k16, vt, km, gq, gk, gv, gr, lr, gates) = _inproj(
            xp, mp[1], mp[0], g1, w_cat, l, tm=tm, tiles_per_group=tpb, dims=dims, attn_aux=True)
        ya, tail = _conv_prompt(u.reshape(bp, t, d_conv), zeros_hist, w_dw, bdw, lng, lnb, l, tt=tm)
        ob = _moba_prompt(qb, k16, vt, km, bias_t, batch=bp, n_heads=h_b, hd=hd_b)
        oc, stf = _gla(*(a.reshape(bp, t, -1) for a in (gq, gk, gv, gr, lr)), wa2, ba2, gg, zeros_state, l,
                       tt=tm, valid=tm)
        res = _merge_mlp(xp, ya.reshape(bp * t, d_conv), ob.reshape(bp * t, d_b), oc.reshape(bp * t, hp),
                         gates, mp[2], mp[4], mp[3], mp[5], g2, gf, wpw2, wpb, wpc, wo, wup, wdn, l,
                         tm=tm, tiles_per_group=tpb, final=final)
        xp = res[0]
        if final:
            y_prompt = res[1].reshape(bp, t, d)
        kp_l.append(kb.reshape(bp, t, h_b, hd_b))
        vp_l.append(vb.reshape(bp, t, h_b, hd_b))
        cp_l.append(tail[:, CONV_HALO - (width - 1):])
        sp_l.append(jnp.swapaxes(stf[..., :dk_c], -1, -2))

        (u, qb, kb, vb, gq, gk, gv, gr, lr, gates) = _inproj(
            xs, ms[1], ms[0], g1, w_cat, l, tm=n_s, tiles_per_group=1, dims=dims, attn_aux=False)
        ext = jnp.concatenate([state_conv[l], u.reshape(bs, ts, d_conv)], axis=1)
        ext_p = jnp.pad(ext, ((0, 0), (0, ts_pad + width - 1 - ext.shape[1] + (-(ts_pad + width - 1)) % 8), (0, 0)))
        ya = _conv_sample(ext_p, w_dw, bdw, lng, lnb, l, rows=ts_pad)[:, :ts]
        q_rep = jnp.repeat(qb.reshape(bs, ts, d_b), h_b, axis=1)
        pad_new = lambda a: jnp.pad(a.reshape(bs, ts, d_b), ((0, 0), (0, ts_pad - ts), (0, 0)))
        ob = _moba_sample(page_table, q_rep, pad_new(kb), pad_new(vb), bias_own, bias_tail,
                          cache_kt, cache_vt, l, n_heads=h_b, t_new=ts)
        pad_t = lambda a: jnp.pad(a.reshape(bs, ts, -1), ((0, 0), (0, gla_rows_s - ts), (0, 0)))
        st0 = jnp.pad(jnp.swapaxes(state_gla[l], -1, -2), ((0, 0), (0, 0), (0, 0), (0, LANES - dk_c)))
        oc, stf = _gla(pad_t(gq), pad_t(gk), pad_t(gv), pad_t(gr), pad_t(lr), wa2, ba2, gg, st0, l,
                       tt=gla_rows_s, valid=ts)
        res = _merge_mlp(xs, ya.reshape(n_s, d_conv), ob.reshape(n_s, d_b), oc[:, :ts].reshape(n_s, hp),
                         gates, ms[2], ms[4], ms[3], ms[5], g2, gf, wpw2, wpb, wpc, wo, wup, wdn, l,
                         tm=n_s, tiles_per_group=1, final=final)
        xs = res[0]
        if final:
            y_sample = res[1].reshape(bs, ts, d)
        ks_l.append(kb.reshape(bs, ts, h_b, hd_b))
        vs_l.append(vb.reshape(bs, ts, h_b, hd_b))
        cs_l.append(ext[:, -(width - 1):])
        ss_l.append(jnp.swapaxes(stf[..., :dk_c], -1, -2))

    return (y_prompt, y_sample, jnp.stack(kp_l), jnp.stack(vp_l), jnp.stack(cp_l), jnp.stack(sp_l),
            jnp.stack(ks_l), jnp.stack(vs_l), jnp.stack(cs_l), jnp.stack(ss_l))
```

```python
import functools
import math

import numpy as np
import jax
import jax.numpy as jnp
from jax import lax
from jax.experimental import pallas as pl
from jax.experimental.pallas import tpu as pltpu

F32 = jnp.float32
BF16 = jnp.bfloat16

MOBA_BLOCK = 256
MOBA_TOPK = 3
MAX_DISTANCE = 128
GLA_HEADS = 4
GLA_CHUNK = 16
GLA_UNROLL = 4
GATE_NORM = 16.0
EPS = 1e-6

LANES = 128
SUBLANES = 8
VMEM_LIMIT = 56 * 1024 * 1024

NEG = -1e30
LOG2E = math.log2(math.e)
V_ROW_PAD = 16
TOKEN_TILE = 256
PAGES_PER_STEP = 16
CONV_HALO = 32

_NT = (((1,), (1,)), ((), ()))


def _dot(a, b):
    return jnp.dot(a, b, preferred_element_type=F32)


def _dot_nt(a, b):
    return lax.dot_general(a, b, _NT, preferred_element_type=F32)


def _dot_f32(a, b):
    return jnp.dot(a, b, precision=lax.Precision.HIGHEST, preferred_element_type=F32)


def _dot_nt_f32(a, b):
    return lax.dot_general(a, b, _NT, precision=lax.Precision.HIGHEST, preferred_element_type=F32)


def _params(*sem):
    return pltpu.CompilerParams(dimension_semantics=sem, vmem_limit_bytes=VMEM_LIMIT)


def _resident(shape, index_map):
    return pl.BlockSpec(shape, index_map, pipeline_mode=pl.Buffered(1))


def _rms_rows(x, g):
    return (x * lax.rsqrt(jnp.mean(x * x, axis=-1, keepdims=True) + EPS)) * g


def _log_sigmoid(x):
    return jnp.minimum(x, 0.0) - jnp.log1p(jnp.exp(-jnp.abs(x)))


def _top_k_mask(s, idx, axis, k, sentinel):
    sel = jnp.zeros(s.shape, dtype=jnp.bool_)
    for _ in range(k):
        mx = jnp.max(s, axis=axis, keepdims=True)
        first = jnp.min(jnp.where(s == mx, idx, sentinel), axis=axis, keepdims=True)
        pick = (idx == first) & (mx > -jnp.inf)
        sel = sel | pick
        s = jnp.where(pick, -jnp.inf, s)
    return sel


def _adaln_kernel(c_ref, w_ref, b_ref, o_ref):
    c = c_ref[...]
    s = c * jax.nn.sigmoid(c)
    o_ref[0] = _dot(s.astype(BF16), w_ref[0].astype(BF16)) + b_ref[0]


def _adaln(c_all, w_ada, b_ada):
    depth, d, n = w_ada.shape
    rows = c_all.shape[0]
    tn = 1536
    return pl.pallas_call(
        _adaln_kernel,
        grid=(depth, n // tn),
        in_specs=[
            pl.BlockSpec((rows, d), lambda l, j: (0, 0)),
            pl.BlockSpec((1, d, tn), lambda l, j: (l, 0, j)),
            pl.BlockSpec((1, 1, tn), lambda l, j: (l, 0, j)),
        ],
        out_specs=pl.BlockSpec((1, rows, tn), lambda l, j: (l, 0, j)),
        out_shape=jax.ShapeDtypeStruct((depth, rows, n), F32),
        compiler_params=_params("arbitrary", "arbitrary"),
        name="adaln",
    )(c_all, w_ada, b_ada.reshape(depth, 1, n))


def _inproj_kernel(x_ref, sc_ref, sh_ref, g_ref, w_ref, *outs, d_conv, d_b, hp, d_model,
                   q_scale, gq_scale, n_heads_b, attn_aux):
    if attn_aux:
        (u_ref, q_ref, k_ref, v_ref, k16_ref, vt_ref, km_ref,
         gq_ref, gk_ref, gv_ref, gr_ref, lr_ref, gate_ref) = outs
    else:
        (u_ref, q_ref, k_ref, v_ref, gq_ref, gk_ref, gv_ref, gr_ref, lr_ref, gate_ref) = outs
    x = x_ref[...]
    h = _rms_rows(x, g_ref[0]) * (1.0 + sc_ref[0]) + sh_ref[0]
    hb = h.astype(BF16)

    pos = [0]

    def seg(width):
        lo = pos[0]
        pos[0] = lo + width
        return _dot(hb, w_ref[0, :, lo:lo + width])

    a = seg(d_conv)
    g = seg(d_conv)
    u_ref[...] = a * jax.nn.sigmoid(g)
    q_ref[...] = seg(d_b) * q_scale
    k = seg(d_b)
    k_ref[...] = k
    v = seg(d_b)
    v_ref[...] = v
    if attn_aux:
        k16_ref[0] = k.astype(BF16)
        vt = v.T.astype(BF16)
        hd = d_b // n_heads_b
        tail = jnp.where(lax.broadcasted_iota(jnp.int32, (V_ROW_PAD, vt.shape[1]), 0) == 0, 1.0, 0.0).astype(BF16)
        for h in range(n_heads_b):
            vt_ref[0, h * (hd + V_ROW_PAD):h * (hd + V_ROW_PAD) + hd, :] = vt[h * hd:(h + 1) * hd, :]
            vt_ref[0, h * (hd + V_ROW_PAD) + hd:(h + 1) * (hd + V_ROW_PAD), :] = tail
        km_ref[0] = jnp.mean(k, axis=0, keepdims=True)
    gq_ref[...] = seg(hp) * gq_scale
    gk_ref[...] = seg(hp)
    gv_ref[...] = seg(hp)
    gr_ref[...] = seg(hp)
    lr_ref[...] = seg(LANES)
    for i in range(3):
        gate_ref[:, i * d_model:(i + 1) * d_model] = jax.nn.sigmoid(seg(d_model))


def _inproj(x, sc, sh, g1, w_cat, layer, *, tm, tiles_per_group, dims, attn_aux):
    n, d = x.shape
    d_conv, d_b, hp, hd_b, dk_c = dims
    nw = w_cat.shape[-1]
    r = sc.shape[1]
    grid = (n // tm,)
    row = lambda i: (i, 0)
    grp = lambda i: (i // tiles_per_group, 0, 0)
    lay = lambda i: (layer, 0, 0)
    outs = [((n, d_conv), F32), ((n, d_b), F32), ((n, d_b), F32), ((n, d_b), F32)]
    specs = [pl.BlockSpec((tm, d_conv), row)] + [pl.BlockSpec((tm, d_b), row)] * 3
    if attn_aux:
        assert tm == MOBA_BLOCK
        nblk = n // MOBA_BLOCK
        vt_rows = d_b + (d_b // hd_b) * V_ROW_PAD
        outs += [((nblk, MOBA_BLOCK, d_b), BF16), ((nblk, vt_rows, MOBA_BLOCK), BF16), ((nblk, 1, d_b), F32)]
        specs += [pl.BlockSpec((1, MOBA_BLOCK, d_b), lambda i: (i, 0, 0)),
                  pl.BlockSpec((1, vt_rows, MOBA_BLOCK), lambda i: (i, 0, 0)),
                  pl.BlockSpec((1, 1, d_b), lambda i: (i, 0, 0))]
    outs += [((n, hp), F32)] * 4 + [((n, LANES), F32), ((n, 3 * d), F32)]
    specs += [pl.BlockSpec((tm, hp), row)] * 4 + [pl.BlockSpec((tm, LANES), row), pl.BlockSpec((tm, 3 * d), row)]
    body = functools.partial(_inproj_kernel, d_conv=d_conv, d_b=d_b, hp=hp, d_model=d,
                             q_scale=hd_b ** -0.5, gq_scale=dk_c ** -0.5, n_heads_b=d_b // hd_b,
                             attn_aux=attn_aux)
    return pl.pallas_call(
        body,
        grid=grid,
        in_specs=[
            pl.BlockSpec((tm, d), row),
            pl.BlockSpec((1, r, d), grp),
            pl.BlockSpec((1, r, d), grp),
            pl.BlockSpec((1, 1, d), lay),
            _resident((1, d, nw), lay),
        ],
        out_specs=specs,
        out_shape=[jax.ShapeDtypeStruct(s, t) for s, t in outs],
        compiler_params=_params("arbitrary"),
        name="inproj",
    )(x, sc, sh, g1, w_cat)


def _conv_taps(ext_ref, start, rows, wdw_ref, bdw_ref, lng_ref, lnb_ref, width, shift_scr=None):
    acc = jnp.zeros((rows, wdw_ref.shape[-1]), F32) + bdw_ref[0]
    if shift_scr is None:
        for k in range(width):
            acc = acc + ext_ref[pl.ds(start + k, rows), :] * wdw_ref[0, k:k + 1, :]
    else:
        span = shift_scr.shape[1]
        for r in range(1, SUBLANES):
            shift_scr[r - 1] = ext_ref[pl.ds(r, span), :]
        for k in range(width):
            a, r = divmod(start + k, SUBLANES)
            src = ext_ref if r == 0 else shift_scr.at[r - 1]
            acc = acc + src[pl.ds(a * SUBLANES, rows), :] * wdw_ref[0, k:k + 1, :]
    mu = jnp.mean(acc, axis=-1, keepdims=True)
    cen = acc - mu
    var = jnp.mean(cen * cen, axis=-1, keepdims=True)
    y = (cen * lax.rsqrt(var + EPS)) * lng_ref[0] + lnb_ref[0]
    return y * jax.nn.sigmoid(y)


def _conv_kernel(u_ref, hist_ref, wdw_ref, bdw_ref, lng_ref, lnb_ref, y_ref, tail_ref, ext_scr, shift_scr,
                 *, tt, width):
    j = pl.program_id(1)

    @pl.when(j == 0)
    def _():
        ext_scr[0:CONV_HALO, :] = hist_ref[0]

    @pl.when(j > 0)
    def _():
        ext_scr[0:CONV_HALO, :] = ext_scr[tt:tt + CONV_HALO, :]

    ext_scr[CONV_HALO:CONV_HALO + tt, :] = u_ref[0]
    start = CONV_HALO - (width - 1)
    y_ref[0] = _conv_taps(ext_scr, start, tt, wdw_ref, bdw_ref, lng_ref, lnb_ref, width, shift_scr).astype(BF16)

    @pl.when(j == pl.num_programs(1) - 1)
    def _():
        tail_ref[0] = ext_scr[tt:tt + CONV_HALO, :]


def _conv_prompt(u, hist, w_dw, b_dw, ln_g, ln_b, layer, *, tt):
    b, t, c = u.shape
    width = w_dw.shape[1]
    lay = lambda bi, j: (layer, 0, 0)
    return pl.pallas_call(
        functools.partial(_conv_kernel, tt=tt, width=width),
        grid=(b, t // tt),
        in_specs=[
            pl.BlockSpec((1, tt, c), lambda bi, j: (bi, j, 0)),
            pl.BlockSpec((1, CONV_HALO, c), lambda bi, j: (bi, 0, 0)),
            pl.BlockSpec((1, width, c), lay),
            pl.BlockSpec((1, 1, c), lay),
            pl.BlockSpec((1, 1, c), lay),
            pl.BlockSpec((1, 1, c), lay),
        ],
        out_specs=[pl.BlockSpec((1, tt, c), lambda bi, j: (bi, j, 0)),
                   pl.BlockSpec((1, CONV_HALO, c), lambda bi, j: (bi, 0, 0))],
        out_shape=[jax.ShapeDtypeStruct((b, t, c), BF16), jax.ShapeDtypeStruct((b, CONV_HALO, c), F32)],
        scratch_shapes=[pltpu.VMEM((CONV_HALO + tt, c), F32),
                        pltpu.VMEM((SUBLANES - 1, CONV_HALO + tt - SUBLANES, c), F32)],
        compiler_params=_params("arbitrary", "arbitrary"),
        name="conv_prompt",
    )(u, hist, w_dw, b_dw, ln_g, ln_b)


def _conv_small_kernel(ext_ref, wdw_ref, bdw_ref, lng_ref, lnb_ref, y_ref, *, rows, width):
    y_ref[0] = _conv_taps(ext_ref.at[0], 0, rows, wdw_ref, bdw_ref, lng_ref, lnb_ref, width).astype(BF16)


def _conv_sample(ext, w_dw, b_dw, ln_g, ln_b, layer, *, rows):
    b, r, c = ext.shape
    width = w_dw.shape[1]
    lay = lambda bi: (layer, 0, 0)
    return pl.pallas_call(
        functools.partial(_conv_small_kernel, rows=rows, width=width),
        grid=(b,),
        in_specs=[
            pl.BlockSpec((1, r, c), lambda bi: (bi, 0, 0)),
            pl.BlockSpec((1, width, c), lay),
            pl.BlockSpec((1, 1, c), lay),
            pl.BlockSpec((1, 1, c), lay),
            pl.BlockSpec((1, 1, c), lay),
        ],
        out_specs=pl.BlockSpec((1, rows, c), lambda bi: (bi, 0, 0)),
        out_shape=jax.ShapeDtypeStruct((b, rows, c), BF16),
        compiler_params=_params("arbitrary"),
        name="conv_sample",
    )(ext, w_dw, b_dw, ln_g, ln_b)


def _moba_prompt_kernel(q_ref, k16_ref, vt_ref, km_ref, bias_ref, o_ref,
                        qm_scr, sel_scr, m_scr, acc_scr, s_scr, *, nb, n_heads, hd):
    i = pl.program_id(1)
    blk = MOBA_BLOCK
    q = q_ref[0]
    km = km_ref[0]
    lane = lax.broadcasted_iota(jnp.int32, (blk, LANES), 1)
    blk_idx = lax.broadcasted_iota(jnp.int32, (nb, blk), 0)
    heads_per_slab = LANES // hd

    for h in range(n_heads):
        p, w = divmod(h, heads_per_slab)
        qm = jnp.where((lane // hd) == w, q[:, p * LANES:(p + 1) * LANES], 0.0) * LOG2E
        qm_scr[h] = qm.astype(BF16)
        s = _dot_nt_f32(km[:, p * LANES:(p + 1) * LANES], qm)
        s = jnp.where(blk_idx < i, s, -jnp.inf)
        sel = _top_k_mask(s, blk_idx, 0, MOBA_TOPK, nb)
        sel_scr[h] = jnp.where(sel, 1.0, 0.0)

    hv = vt_ref.shape[1] // n_heads

    def block(n, kind):
        kb = k16_ref[n]
        vb = vt_ref[n]

        for h in range(n_heads):
            p = h // heads_per_slab
            s_scr[h] = _dot_nt(kb[:, p * LANES:(p + 1) * LANES], qm_scr[h])
        for h in range(n_heads):
            s = s_scr[h]
            rows = slice(h * hv, (h + 1) * hv)
            if kind == "own":
                s = s + bias_ref[h, 0]
                m = jnp.max(s, axis=0, keepdims=True)
                m_scr[h:h + 1, :] = m
                acc_scr[rows, :] = _dot(vb[rows, :], jnp.exp2(s - m).astype(BF16))
            else:
                if kind == "prev":
                    s = s + bias_ref[h, 1]
                picked = sel_scr[h, pl.ds(n, 1), :] > 0.5
                m_old = m_scr[h:h + 1, :]
                m = jnp.where(picked, jnp.maximum(m_old, jnp.max(s, axis=0, keepdims=True)), m_old)
                alpha = jnp.exp2(m_old - m)
                pr = jnp.exp2(s - jnp.where(picked, m, -NEG))
                m_scr[h:h + 1, :] = m
                acc_scr[rows, :] = alpha * acc_scr[rows, :] + _dot(vb[rows, :], pr.astype(BF16))

    block(i, "own")

    @pl.when(i >= 1)
    def _():
        block(i - 1, "prev")

    def far(n, carry):
        block(n, "far")
        return carry

    lax.fori_loop(0, i - 1, far, 0)

    out = [acc_scr[h * hv:h * hv + hd, :] / acc_scr[h * hv + hd:h * hv + hd + 1, :] for h in range(n_heads)]
    o_ref[0] = jnp.concatenate(out, axis=0).T.astype(BF16)


def _moba_prompt(q, k16, vt, km, bias_t, *, batch, n_heads, hd):
    n, d_b = q.shape
    t = n // batch
    nb = t // MOBA_BLOCK
    blk = MOBA_BLOCK
    body = functools.partial(_moba_prompt_kernel, nb=nb, n_heads=n_heads, hd=hd)
    return pl.pallas_call(
        body,
        grid=(batch, nb),
        in_specs=[
            pl.BlockSpec((1, blk, d_b), lambda b, i: (b * nb + i, 0, 0)),
            pl.BlockSpec((nb, blk, d_b), lambda b, i: (b, 0, 0)),
            pl.BlockSpec((nb, vt.shape[1], blk), lambda b, i: (b, 0, 0)),
            pl.BlockSpec((1, nb, d_b), lambda b, i: (b, 0, 0)),
            _resident((n_heads, 2, blk, blk), lambda b, i: (0, 0, 0, 0)),
        ],
        out_specs=pl.BlockSpec((1, blk, d_b), lambda b, i: (b * nb + i, 0, 0)),
        out_shape=jax.ShapeDtypeStruct((n // blk, blk, d_b), BF16),
        scratch_shapes=[
            pltpu.VMEM((n_heads, blk, LANES), BF16),
            pltpu.VMEM((n_heads, nb, blk), F32),
            pltpu.VMEM((n_heads, blk), F32),
            pltpu.VMEM((vt.shape[1], blk), F32),
            pltpu.VMEM((n_heads, blk, blk), F32),
        ],
        compiler_params=_params("arbitrary", "arbitrary"),
        name="moba_prompt",
    )(q.reshape(n // blk, blk, d_b), k16, vt, km.reshape(batch, nb, d_b), bias_t)


def _moba_sample_kernel(pt_ref, q_ref, kn_ref, vn_ref, bown_ref, btail_ref, *rest,
                        n_chunks, n_heads, t_new, page, pps):
    del pt_ref
    k_refs = rest[:pps]
    v_refs = rest[pps:2 * pps]
    o_ref = rest[2 * pps]
    qbd_scr, s_scr, bsum_scr, acc_scr, l_scr, kv_scr = rest[2 * pps + 1:]
    c = pl.program_id(1)
    rows, d_b = q_ref.shape[1:]
    hd = d_b // n_heads
    blocks_per_chunk = pps * page // MOBA_BLOCK
    pages_per_block = MOBA_BLOCK // page
    nbp = n_chunks * blocks_per_chunk

    row_i = lax.broadcasted_iota(jnp.int32, (rows, d_b), 0)
    lane_i = lax.broadcasted_iota(jnp.int32, (rows, d_b), 1)
    head_mask = (lane_i // hd) == (row_i % n_heads)
    blk_lane = lax.broadcasted_iota(jnp.int32, (rows, LANES), 1)

    @pl.when(c == 0)
    def _():
        qbd_scr[...] = jnp.where(head_mask, q_ref[0], 0.0).astype(BF16)
        bsum_scr[...] = jnp.zeros(bsum_scr.shape, F32)

    @pl.when(c < n_chunks)
    def _():
        qb16 = qbd_scr[...]
        bsum = bsum_scr[...]
        for p in range(pps):
            s = _dot(qb16, k_refs[p][0, 0].astype(BF16))
            s_scr[c, :, p * page:(p + 1) * page] = s
            ps = jnp.sum(s, axis=-1, keepdims=True)
            psum = ps if p % pages_per_block == 0 else psum + ps
            if p % pages_per_block == pages_per_block - 1:
                blk = c * blocks_per_chunk + p // pages_per_block
                bsum = jnp.where(blk_lane == blk, psum, bsum)
        bsum_scr[...] = bsum

    @pl.when(c == n_chunks)
    def _():
        bs = jnp.where(blk_lane < nbp, bsum_scr[...], -jnp.inf)
        sel = _top_k_mask(bs, blk_lane, 1, MOBA_TOPK, LANES)
        blk_bias = jnp.where(sel, 0.0, NEG)

        kv_scr[...] = jnp.zeros(kv_scr.shape, F32)
        kv_scr[0:kn_ref.shape[1], :] = kn_ref[0]
        s_own = _dot_nt(qbd_scr[...], kv_scr[...].astype(BF16)) + bown_ref[...]
        m = jnp.max(s_own, axis=-1, keepdims=True)

        for b in range(nbp):
            cc, off = divmod(b, blocks_per_chunk)
            cols = slice(off * MOBA_BLOCK, (off + 1) * MOBA_BLOCK)
            s = s_scr[cc, :, cols] + blk_bias[:, b:b + 1]
            if b == nbp - 1:
                s = s + btail_ref[...]
            s_scr[cc, :, cols] = s
            m = jnp.maximum(m, jnp.max(s, axis=-1, keepdims=True))

        p_own = jnp.exp(s_own - m)
        l = jnp.sum(p_own, axis=-1, keepdims=True)
        for cc in range(n_chunks):
            pr = jnp.exp(s_scr[cc] - m)
            l = l + jnp.sum(pr, axis=-1, keepdims=True)
            s_scr[cc] = pr
        kv_scr[0:vn_ref.shape[1], :] = vn_ref[0]
        acc_scr[...] = _dot(p_own.astype(BF16), kv_scr[...].astype(BF16))
        l_scr[...] = jnp.broadcast_to(l, l_scr.shape)

    @pl.when(c >= n_chunks)
    def _():
        cc = c - n_chunks
        acc = acc_scr[...]
        for p in range(pps):
            pr = s_scr[cc, :, p * page:(p + 1) * page]
            acc = acc + _dot_nt(pr.astype(BF16), v_refs[p][0, 0].astype(BF16))
        acc_scr[...] = acc

    @pl.when(c == 2 * n_chunks - 1)
    def _():
        o = jnp.where(head_mask, acc_scr[...] / l_scr[:, 0:1], 0.0)
        o_ref[0] = jnp.sum(o.reshape(t_new, n_heads, d_b), axis=1).astype(BF16)


def _moba_sample(page_table, q_rep, k_new, v_new, bias_own, bias_tail, cache_kt, cache_vt, layer,
                 *, n_heads, t_new):
    b, rows, d_b = q_rep.shape
    page = cache_kt.shape[-1]
    n_pages = page_table.shape[1]
    pps = min(PAGES_PER_STEP, n_pages)
    n_chunks = n_pages // pps
    assert n_pages % pps == 0 and (pps * page) % MOBA_BLOCK == 0 and MOBA_BLOCK % page == 0
    chunk = pps * page

    def k_map(r):
        return lambda bi, c, pt: (layer, pt[bi, jnp.minimum(c, n_chunks - 1) * pps + r], 0, 0)

    def v_map(r):
        def index(bi, c, pt):
            early = pt[jnp.maximum(bi - 1, 0), (n_chunks - 1) * pps + r]
            return (layer, jnp.where(c >= n_chunks, pt[bi, jnp.maximum(c - n_chunks, 0) * pps + r], early),
                    0, 0)
        return index

    per_b = lambda bi, c, pt: (bi, 0, 0)
    const2 = lambda bi, c, pt: (0, 0)
    in_specs = [
        pl.BlockSpec((1, rows, d_b), per_b),
        pl.BlockSpec((1, k_new.shape[1], d_b), per_b),
        pl.BlockSpec((1, v_new.shape[1], d_b), per_b),
        pl.BlockSpec(bias_own.shape, const2),
        pl.BlockSpec(bias_tail.shape, const2),
    ]
    in_specs += [pl.BlockSpec((1, 1, d_b, page), k_map(r)) for r in range(pps)]
    in_specs += [pl.BlockSpec((1, 1, d_b, page), v_map(r)) for r in range(pps)]
    body = functools.partial(_moba_sample_kernel, n_chunks=n_chunks, n_heads=n_heads, t_new=t_new,
                             page=page, pps=pps)
    grid_spec = pltpu.PrefetchScalarGridSpec(
        num_scalar_prefetch=1,
        grid=(b, 2 * n_chunks),
        in_specs=in_specs,
        out_specs=pl.BlockSpec((1, t_new, d_b), per_b),
        scratch_shapes=[
            pltpu.VMEM((rows, d_b), BF16),
            pltpu.VMEM((n_chunks, rows, chunk), F32),
            pltpu.VMEM((rows, LANES), F32),
            pltpu.VMEM((rows, d_b), F32),
            pltpu.VMEM((rows, LANES), F32),
            pltpu.VMEM((LANES, d_b), F32),
        ],
    )
    return pl.pallas_call(
        body,
        grid_spec=grid_spec,
        out_shape=jax.ShapeDtypeStruct((b, t_new, d_b), BF16),
        compiler_params=_params("arbitrary", "arbitrary"),
        name="moba_sample",
    )(page_table, q_rep, k_new, v_new, bias_own, bias_tail, *([cache_kt] * pps), *([cache_vt] * pps))


def _gla_kernel(gq_ref, gk_ref, gv_ref, gr_ref, lr_ref, wa_ref, ba_ref, gg_ref, st0_ref,
                o_ref, stf_ref, st_scr, bc_scr, bl_scr, qd_scr, kd_scr, vt_scr, o_scr,
                *, tt, valid, n_heads):
    j = pl.program_id(1)
    c_sz = GLA_CHUNK
    hp = n_heads * LANES

    @pl.when(j == 0)
    def _():
        st_scr[...] = st0_ref[0]

    x = _dot(lr_ref[0].astype(BF16), wa_ref[0]) + ba_ref[0]
    la = _log_sigmoid(x) * (1.0 / GATE_NORM)
    if valid < tt:
        la = jnp.where(lax.broadcasted_iota(jnp.int32, (tt, hp), 0) < valid, la, 0.0)
    r_i = lax.broadcasted_iota(jnp.int32, (tt, tt), 0)
    c_i = lax.broadcasted_iota(jnp.int32, (tt, tt), 1)
    low = jnp.where(((r_i // c_sz) == (c_i // c_sz)) & (c_i <= r_i), 1.0, 0.0).astype(BF16)
    la_hi = la.astype(BF16)
    la_lo = (la - la_hi.astype(F32)).astype(BF16)
    bc = _dot(low, la_hi) + _dot(low, la_lo)
    bc_scr[...] = bc
    n_ch = tt // c_sz
    in_ch = (lax.broadcasted_iota(jnp.int32, (n_ch, tt), 1) // c_sz
             == lax.broadcasted_iota(jnp.int32, (n_ch, tt), 0))
    tot = jnp.where(in_ch, 1.0, 0.0).astype(BF16)
    ends = _dot(tot, la_hi) + _dot(tot, la_lo)
    bl_scr[...] = ends
    bl = jnp.broadcast_to(ends[:, None, :], (n_ch, c_sz, hp)).reshape(tt, hp)
    qd_scr[...] = (gq_ref[0] * jnp.exp(bc)).astype(BF16)
    kd_scr[...] = (gk_ref[0] * jnp.exp(bl - bc)).astype(BF16)
    vt_scr[...] = gv_ref[0].T.astype(BF16)

    half = c_sz // 2
    t_half = lax.broadcasted_iota(jnp.int32, (half, LANES), 0)
    chunk_of_col = lax.broadcasted_iota(jnp.int32, (LANES, tt), 1) // c_sz

    def chunk(c, carry):
        rows = pl.ds(pl.multiple_of(c * c_sz, c_sz), c_sz)
        bc_c = bc_scr[rows, :]
        q_c = gq_ref[0, rows, :]
        k_c = gk_ref[0, rows, :]
        v_c = gv_ref[0, rows, :]
        decay = jnp.exp(bl_scr[pl.ds(c, 1), :])
        in_chunk = chunk_of_col == c
        for h in range(n_heads):
            hs = slice(h * LANES, (h + 1) * LANES)
            st = st_scr[h]
            o_h = _dot_nt(qd_scr[rows, hs], st.astype(BF16))
            bch = bc_c[:, hs]
            qh = q_c[:, hs]
            parts = [o_h[0:half], o_h[half:c_sz]]
            for s in range(c_sz):
                for hi in range(2):
                    r0 = hi * half
                    if s >= r0 + half:
                        continue
                    diff = bch[r0:r0 + half] - bch[s:s + 1, :]
                    if s > r0:
                        diff = jnp.where(t_half + r0 >= s, diff, -jnp.inf)
                    a_s = jnp.sum(qh[r0:r0 + half] * k_c[s:s + 1, hs] * jnp.exp(diff), axis=-1, keepdims=True)
                    parts[hi] = parts[hi] + a_s * v_c[s:s + 1, hs]
            o_scr[rows, hs] = jnp.concatenate(parts, axis=0)
            vtm = jnp.where(in_chunk, vt_scr[hs, :], jnp.zeros((), BF16))
            st_scr[h] = decay[:, hs] * st + _dot(vtm, kd_scr[:, hs])
        return carry

    lax.fori_loop(0, (valid + c_sz - 1) // c_sz, chunk, 0, unroll=GLA_UNROLL)

    o = o_scr[...]
    r = gr_ref[0]
    for h in range(n_heads):
        hs = slice(h * LANES, (h + 1) * LANES)
        oh = o[:, hs]
        oh = oh * lax.rsqrt(jnp.mean(oh * oh, axis=-1, keepdims=True) + EPS)
        rh = r[:, hs]
        o_ref[0, :, hs] = ((oh * gg_ref[0, :, hs]) * (rh * jax.nn.sigmoid(rh))).astype(BF16)

    @pl.when(j == pl.num_programs(1) - 1)
    def _():
        stf_ref[0] = st_scr[...]


def _gla(gq, gk, gv, gr, lr, wa, ba, gg, st0, layer, *, tt, valid):
    b, t, hp = gq.shape
    n_heads = hp // LANES
    lay = lambda bi, j: (layer, 0, 0)
    tile = lambda bi, j: (bi, j, 0)
    per_b = lambda bi, j: (bi, 0, 0, 0)
    body = functools.partial(_gla_kernel, tt=tt, valid=valid, n_heads=n_heads)
    return pl.pallas_call(
        body,
        grid=(b, t // tt),
        in_specs=[pl.BlockSpec((1, tt, hp), tile)] * 4 + [
            pl.BlockSpec((1, tt, LANES), tile),
            pl.BlockSpec((1, LANES, hp), lay),
            pl.BlockSpec((1, 1, hp), lay),
            pl.BlockSpec((1, 1, hp), lay),
            pl.BlockSpec((1, n_heads, LANES, LANES), per_b),
        ],
        out_specs=[pl.BlockSpec((1, tt, hp), tile), pl.BlockSpec((1, n_heads, LANES, LANES), per_b)],
        out_shape=[jax.ShapeDtypeStruct((b, t, hp), BF16),
                   jax.ShapeDtypeStruct((b, n_heads, LANES, LANES), F32)],
        scratch_shapes=[
            pltpu.VMEM((n_heads, LANES, LANES), F32),
            pltpu.VMEM((tt, hp), F32),
            pltpu.VMEM((tt // GLA_CHUNK, hp), F32),
            pltpu.VMEM((tt, hp), BF16),
            pltpu.VMEM((tt, hp), BF16),
            pltpu.VMEM((hp, tt), BF16),
            pltpu.VMEM((tt, hp), F32),
        ],
        compiler_params=_params("arbitrary", "arbitrary"),
        name="gla",
    )(gq, gk, gv, gr, lr, wa, ba, gg, st0)


def _merge_mlp_kernel(x_ref, ya_ref, ob_ref, oc_ref, gate_ref, gt1_ref, sc2_ref, sh2_ref, gt2_ref,
                      g2_ref, gf_ref, wa_ref, wb_ref, wc_ref, wo_ref, wup_ref, wdn_ref, *outs, final):
    d = x_ref.shape[-1]
    m = (gate_ref[:, 0:d] * _dot(ya_ref[...], wa_ref[0])
         + gate_ref[:, d:2 * d] * _dot(ob_ref[...], wb_ref[0])
         + gate_ref[:, 2 * d:3 * d] * _dot(oc_ref[...], wc_ref[0]))
    x1 = x_ref[...] + gt1_ref[0] * _dot(m.astype(BF16), wo_ref[0])
    h2 = _rms_rows(x1, g2_ref[0]) * (1.0 + sc2_ref[0]) + sh2_ref[0]
    up = jnp.maximum(_dot(h2.astype(BF16), wup_ref[0]), 0.0)
    x2 = x1 + gt2_ref[0] * _dot((up * up).astype(BF16), wdn_ref[0])
    outs[0][...] = x2
    if final:
        outs[1][...] = _rms_rows(x2, gf_ref[...])


def _merge_mlp(x, ya, ob, oc, gates, gt1, sc2, sh2, gt2, g2, gf, wa, wb, wc, wo, wup, wdn, layer,
               *, tm, tiles_per_group, final):
    n, d = x.shape
    c = ya.shape[-1]
    dff = wup.shape[-1]
    r = gt1.shape[1]
    row = lambda i: (i, 0)
    grp = lambda i: (i // tiles_per_group, 0, 0)
    lay = lambda i: (layer, 0, 0)
    n_out = 2 if final else 1
    return pl.pallas_call(
        functools.partial(_merge_mlp_kernel, final=final),
        grid=(n // tm,),
        in_specs=[pl.BlockSpec((tm, d), row)] + [pl.BlockSpec((tm, c), row)] * 3 + [
            pl.BlockSpec((tm, 3 * d), row),
            pl.BlockSpec((1, r, d), grp), pl.BlockSpec((1, r, d), grp),
            pl.BlockSpec((1, r, d), grp), pl.BlockSpec((1, r, d), grp),
            pl.BlockSpec((1, 1, d), lay),
            pl.BlockSpec((1, d), lambda i: (0, 0)),
            _resident((1, c, d), lay), _resident((1, c, d), lay), _resident((1, c, d), lay),
            _resident((1, d, d), lay), _resident((1, d, dff), lay), _resident((1, dff, d), lay),
        ],
        out_specs=[pl.BlockSpec((tm, d), row)] * n_out,
        out_shape=[jax.ShapeDtypeStruct((n, d), F32)] * n_out,
        compiler_params=_params("arbitrary"),
        name="merge_mlp",
    )(x, ya, ob, oc, gates, gt1, sc2, sh2, gt2, g2, gf, wa, wb, wc, wo, wup, wdn)


def _t5_bucket_table(n_buckets, max_dist):
    dist = np.arange(max_dist + 1)
    max_exact = n_buckets // 2
    d = np.maximum(dist, 1).astype(np.float32)
    large = max_exact + (np.log(d / max_exact) / math.log(MAX_DISTANCE / max_exact)
                         * (n_buckets - max_exact)).astype(np.int32)
    large = np.minimum(large, n_buckets - 1)
    return np.where(dist < max_exact, dist, large).astype(np.int32)


def _pad_heads(w, n_heads):
    k = w.shape[-1] // n_heads
    w = w.reshape(w.shape[:-1] + (n_heads, k))
    w = jnp.pad(w, [(0, 0)] * (w.ndim - 1) + [(0, LANES - k)])
    return w.reshape(w.shape[:-2] + (n_heads * LANES,))


def kernel(x_prompt, x_sample, c_prompt, c_sample, cache_k, cache_v, page_table, state_conv, state_gla,
           w_ada, b_ada, g_norm1, w_in, w_dw, b_dw, ln_g, ln_b, w_pw2, w_pb, rel_bias, w_a2, b_a, g_gla,
           w_pc, w_o, g_norm2, w_up, w_down, g_final):
    bp, t, d = x_prompt.shape
    bs, ts, _ = x_sample.shape
    depth = w_in.shape[0]
    _, n_pool, page, h_b, hd_b = cache_k.shape
    d_b = h_b * hd_b
    d_conv = w_dw.shape[-1]
    width = w_dw.shape[1]
    rank = w_a2.shape[1]
    h_c = GLA_HEADS
    dk_c = w_a2.shape[-1] // h_c
    dv_c = state_gla.shape[-1]
    hp = h_c * LANES
    n_buckets = rel_bias.shape[0]
    n_pages = page_table.shape[1]
    past_len = n_pages * page
    assert dv_c == LANES and h_c * dv_c == hp and width - 1 <= CONV_HALO

    sizes = (2 * d_conv, d_b, d_b, d_b, h_c * dk_c, h_c * dk_c, h_c * dv_c, h_c * dv_c, rank, d, d, d)
    offs = np.cumsum((0,) + sizes)
    part = [w_in[:, :, offs[i]:offs[i + 1]] for i in range(len(sizes))]
    w_cat = jnp.concatenate([
        part[0], part[1], part[2], part[3],
        _pad_heads(part[4], h_c), _pad_heads(part[5], h_c), part[6], part[7],
        jnp.pad(part[8], ((0, 0), (0, 0), (0, LANES - rank))),
        part[9], part[10], part[11]], axis=-1).astype(BF16)
    wa2 = jnp.pad(_pad_heads(w_a2, h_c), ((0, 0), (0, LANES - rank), (0, 0))).astype(BF16)
    ba2 = _pad_heads(b_a, h_c).reshape(depth, 1, hp)
    gg = g_gla.reshape(depth, 1, hp)
    wpw2, wpb, wpc, wo, wup, wdn = (w.astype(BF16) for w in (w_pw2, w_pb, w_pc, w_o, w_up, w_down))
    g1 = g_norm1.reshape(depth, 1, d)
    g2 = g_norm2.reshape(depth, 1, d)
    gf = g_final.reshape(1, d)
    bdw = b_dw.reshape(depth, 1, d_conv)
    lng = ln_g.reshape(depth, 1, d_conv)
    lnb = ln_b.reshape(depth, 1, d_conv)

    n_c = bp + bs
    rows_c = -(-n_c // 8) * 8
    c_all = jnp.pad(jnp.concatenate([c_prompt, c_sample], axis=0), ((0, rows_c - n_c), (0, 0)))
    mods = _adaln(c_all, w_ada, b_ada).reshape(depth, rows_c, 6, d)

    blk = MOBA_BLOCK
    assert ts <= blk and past_len % blk == 0
    bucket = _t5_bucket_table(n_buckets, 2 * blk - 1)
    onehot = np.zeros((4 * blk, n_buckets), np.float32)
    onehot[np.arange(2 * blk), bucket] = 1.0
    by_dist = jnp.dot(onehot, rel_bias - rel_bias[n_buckets - 1:n_buckets, :],
                      precision=lax.Precision.HIGHEST).T
    toep = jnp.tile(by_dist, (1, blk))[:, :blk * (4 * blk - 1)].reshape(h_b, blk, 4 * blk - 1)
    causal = np.arange(blk)[:, None] <= np.arange(blk)[None, :]
    bias_t = jnp.stack([jnp.where(causal, toep[:, :, :blk] * LOG2E, NEG), toep[:, :, blk:2 * blk] * LOG2E],
                       axis=1)
    rows_s = ts * h_b
    own = jnp.transpose(toep[:, :ts, :ts], (2, 0, 1)).reshape(rows_s, ts)
    own_ok = np.repeat(np.arange(ts), h_b)[:, None] >= np.arange(LANES)[None, :]
    bias_own = jnp.where(own_ok, jnp.pad(own, ((0, 0), (0, LANES - ts))), NEG)
    bias_tail = jnp.transpose(toep[:, :, blk:blk + ts], (2, 0, 1)).reshape(rows_s, blk)
    assert past_len // blk <= LANES
    cache_kt = jnp.transpose(cache_k, (0, 1, 3, 4, 2)).reshape(depth, n_pool, d_b, page)
    cache_vt = jnp.transpose(cache_v, (0, 1, 3, 4, 2)).reshape(depth, n_pool, d_b, page)

    xp = x_prompt.reshape(bp * t, d)
    xs = x_sample.reshape(bs * ts, d)
    n_s = bs * ts
    tm = TOKEN_TILE
    tpb = t // tm
    dims = (d_conv, d_b, hp, hd_b, dk_c)
    ts_pad = 8
    gla_rows_s = LANES

    kp_l, vp_l, cp_l, sp_l, ks_l, vs_l, cs_l, ss_l = ([] for _ in range(8))
    y_prompt = y_sample = None
    zeros_hist = jnp.zeros((bp, CONV_HALO, d_conv), F32)
    zeros_state = jnp.zeros((bp, h_c, LANES, LANES), F32)
    for l in range(depth):
        final = l == depth - 1
        mp = [mods[l, :bp, i].reshape(bp, 1, d) for i in range(6)]
        ms = [jnp.repeat(mods[l, bp:n_c, i], ts, axis=0).reshape(1, n_s, d) for i in range(6)]

        (u, qb, kb, vb, k16, vt, km, gq, gk, gv, gr, lr, gates) = _inproj(
            xp, mp[1], mp[0], g1, w_cat, l, tm=tm, tiles_per_group=tpb, dims=dims, attn_aux=True)
        ya, tail = _conv_prompt(u.reshape(bp, t, d_conv), zeros_hist, w_dw, bdw, lng, lnb, l, tt=tm)
        ob = _moba_prompt(qb, k16, vt, km, bias_t, batch=bp, n_heads=h_b, hd=hd_b)
        oc, stf = _gla(*(a.reshape(bp, t, -1) for a in (gq, gk, gv, gr, lr)), wa2, ba2, gg, zeros_state, l,
                       tt=tm, valid=tm)
        res = _merge_mlp(xp, ya.reshape(bp * t, d_conv), ob.reshape(bp * t, d_b), oc.reshape(bp * t, hp),
                         gates, mp[2], mp[4], mp[3], mp[5], g2, gf, wpw2, wpb, wpc, wo, wup, wdn, l,
                         tm=tm, tiles_per_group=tpb, final=final)
        xp = res[0]
        if final:
            y_prompt = res[1].reshape(bp, t, d)
        kp_l.append(kb.reshape(bp, t, h_b, hd_b))
        vp_l.append(vb.reshape(bp, t, h_b, hd_b))
        cp_l.append(tail[:, CONV_HALO - (width - 1):])
        sp_l.append(jnp.swapaxes(stf[..., :dk_c], -1, -2))

        (u, qb, kb, vb, gq, gk, gv, gr, lr, gates) = _inproj(
            xs, ms[1], ms[0], g1, w_cat, l, tm=n_s, tiles_per_group=1, dims=dims, attn_aux=False)
        ext = jnp.concatenate([state_conv[l], u.reshape(bs, ts, d_conv)], axis=1)
        ext_p = jnp.pad(ext, ((0, 0), (0, ts_pad + width - 1 - ext.shape[1] + (-(ts_pad + width - 1)) % 8), (0, 0)))
        ya = _conv_sample(ext_p, w_dw, bdw, lng, lnb, l, rows=ts_pad)[:, :ts]
        q_rep = jnp.repeat(qb.reshape(bs, ts, d_b), h_b, axis=1)
        pad_new = lambda a: jnp.pad(a.reshape(bs, ts, d_b), ((0, 0), (0, ts_pad - ts), (0, 0)))
        ob = _moba_sample(page_table, q_rep, pad_new(kb), pad_new(vb), bias_own, bias_tail,
                          cache_kt, cache_vt, l, n_heads=h_b, t_new=ts)
        pad_t = lambda a: jnp.pad(a.reshape(bs, ts, -1), ((0, 0), (0, gla_rows_s - ts), (0, 0)))
        st0 = jnp.pad(jnp.swapaxes(state_gla[l], -1, -2), ((0, 0), (0, 0), (0, 0), (0, LANES - dk_c)))
        oc, stf = _gla(pad_t(gq), pad_t(gk), pad_t(gv), pad_t(gr), pad_t(lr), wa2, ba2, gg, st0, l,
                       tt=gla_rows_s, valid=ts)
        res = _merge_mlp(xs, ya.reshape(n_s, d_conv), ob.reshape(n_s, d_b), oc[:, :ts].reshape(n_s, hp),
                         gates, ms[2], ms[4], ms[3], ms[5], g2, gf, wpw2, wpb, wpc, wo, wup, wdn, l,
                         tm=n_s, tiles_per_group=1, final=final)
        xs = res[0]
        if final:
            y_sample = res[1].reshape(bs, ts, d)
        ks_l.append(kb.reshape(bs, ts, h_b, hd_b))
        vs_l.append(vb.reshape(bs, ts, h_b, hd_b))
        cs_l.append(ext[:, -(width - 1):])
        ss_l.append(jnp.swapaxes(stf[..., :dk_c], -1, -2))

    return (y_prompt, y_sample, jnp.stack(kp_l), jnp.stack(vp_l), jnp.stack(cp_l), jnp.stack(sp_l),
            jnp.stack(ks_l), jnp.stack(vs_l), jnp.stack(cs_l), jnp.stack(ss_l))
```

```python
import functools
import math

import numpy as np
import jax
import jax.numpy as jnp
from jax import lax
from jax.experimental import pallas as pl
from jax.experimental.pallas import tpu as pltpu

F32 = jnp.float32
BF16 = jnp.bfloat16

MOBA_BLOCK = 256
MOBA_TOPK = 3
MAX_DISTANCE = 128
GLA_HEADS = 4
GLA_CHUNK = 16
GLA_UNROLL = 4
GATE_NORM = 16.0
EPS = 1e-6

LANES = 128
SUBLANES = 8
VMEM_LIMIT = 56 * 1024 * 1024

NEG = -1e30
LOG2E = math.log2(math.e)
V_ROW_PAD = 16
TOKEN_TILE = 256
PAGES_PER_STEP = 32
CONV_HALO = 32

_NT = (((1,), (1,)), ((), ()))


def _dot(a, b):
    return jnp.dot(a, b, preferred_element_type=F32)


def _dot_nt(a, b):
    return lax.dot_general(a, b, _NT, preferred_element_type=F32)


def _dot_f32(a, b):
    return jnp.dot(a, b, precision=lax.Precision.HIGHEST, preferred_element_type=F32)


def _dot_nt_f32(a, b):
    return lax.dot_general(a, b, _NT, precision=lax.Precision.HIGHEST, preferred_element_type=F32)


def _params(*sem):
    return pltpu.CompilerParams(dimension_semantics=sem, vmem_limit_bytes=VMEM_LIMIT)


def _resident(shape, index_map):
    return pl.BlockSpec(shape, index_map, pipeline_mode=pl.Buffered(1))


def _rms_rows(x, g):
    return (x * lax.rsqrt(jnp.mean(x * x, axis=-1, keepdims=True) + EPS)) * g


def _log_sigmoid(x):
    return jnp.minimum(x, 0.0) - jnp.log1p(jnp.exp(-jnp.abs(x)))


def _top_k_mask(s, idx, axis, k, sentinel):
    sel = jnp.zeros(s.shape, dtype=jnp.bool_)
    for _ in range(k):
        mx = jnp.max(s, axis=axis, keepdims=True)
        first = jnp.min(jnp.where(s == mx, idx, sentinel), axis=axis, keepdims=True)
        pick = (idx == first) & (mx > -jnp.inf)
        sel = sel | pick
        s = jnp.where(pick, -jnp.inf, s)
    return sel


def _adaln_kernel(c_ref, w_ref, b_ref, o_ref):
    c = c_ref[...]
    s = c * jax.nn.sigmoid(c)
    o_ref[0] = _dot(s.astype(BF16), w_ref[0].astype(BF16)) + b_ref[0]


def _adaln(c_all, w_ada, b_ada):
    depth, d, n = w_ada.shape
    rows = c_all.shape[0]
    tn = 1536
    return pl.pallas_call(
        _adaln_kernel,
        grid=(depth, n // tn),
        in_specs=[
            pl.BlockSpec((rows, d), lambda l, j: (0, 0)),
            pl.BlockSpec((1, d, tn), lambda l, j: (l, 0, j)),
            pl.BlockSpec((1, 1, tn), lambda l, j: (l, 0, j)),
        ],
        out_specs=pl.BlockSpec((1, rows, tn), lambda l, j: (l, 0, j)),
        out_shape=jax.ShapeDtypeStruct((depth, rows, n), F32),
        compiler_params=_params("arbitrary", "arbitrary"),
        name="adaln",
    )(c_all, w_ada, b_ada.reshape(depth, 1, n))


def _inproj_kernel(x_ref, sc_ref, sh_ref, g_ref, w_ref, *outs, d_conv, d_b, hp, d_model,
                   q_scale, gq_scale, n_heads_b, attn_aux):
    if attn_aux:
        (u_ref, q_ref, k_ref, v_ref, k16_ref, vt_ref, km_ref,
         gq_ref, gk_ref, gv_ref, gr_ref, lr_ref, gate_ref) = outs
    else:
        (u_ref, q_ref, k_ref, v_ref, gq_ref, gk_ref, gv_ref, gr_ref, lr_ref, gate_ref) = outs
    x = x_ref[...]
    h = _rms_rows(x, g_ref[0]) * (1.0 + sc_ref[0]) + sh_ref[0]
    hb = h.astype(BF16)

    pos = [0]

    def seg(width):
        lo = pos[0]
        pos[0] = lo + width
        return _dot(hb, w_ref[0, :, lo:lo + width])

    a = seg(d_conv)
    g = seg(d_conv)
    u_ref[...] = a * jax.nn.sigmoid(g)
    q_ref[...] = seg(d_b) * q_scale
    k = seg(d_b)
    k_ref[...] = k
    v = seg(d_b)
    v_ref[...] = v
    if attn_aux:
        k16_ref[0] = k.astype(BF16)
        vt = v.T.astype(BF16)
        hd = d_b // n_heads_b
        tail = jnp.where(lax.broadcasted_iota(jnp.int32, (V_ROW_PAD, vt.shape[1]), 0) == 0, 1.0, 0.0).astype(BF16)
        for h in range(n_heads_b):
            vt_ref[0, h * (hd + V_ROW_PAD):h * (hd + V_ROW_PAD) + hd, :] = vt[h * hd:(h + 1) * hd, :]
            vt_ref[0, h * (hd + V_ROW_PAD) + hd:(h + 1) * (hd + V_ROW_PAD), :] = tail
        km_ref[0] = jnp.mean(k, axis=0, keepdims=True)
    gq_ref[...] = seg(hp) * gq_scale
    gk_ref[...] = seg(hp)
    gv_ref[...] = seg(hp)
    gr_ref[...] = seg(hp)
    lr_ref[...] = seg(LANES)
    for i in range(3):
        gate_ref[:, i * d_model:(i + 1) * d_model] = jax.nn.sigmoid(seg(d_model))


def _inproj(x, sc, sh, g1, w_cat, layer, *, tm, tiles_per_group, dims, attn_aux):
    n, d = x.shape
    d_conv, d_b, hp, hd_b, dk_c = dims
    nw = w_cat.shape[-1]
    r = sc.shape[1]
    grid = (n // tm,)
    row = lambda i: (i, 0)
    grp = lambda i: (i // tiles_per_group, 0, 0)
    lay = lambda i: (layer, 0, 0)
    outs = [((n, d_conv), F32), ((n, d_b), F32), ((n, d_b), F32), ((n, d_b), F32)]
    specs = [pl.BlockSpec((tm, d_conv), row)] + [pl.BlockSpec((tm, d_b), row)] * 3
    if attn_aux:
        assert tm == MOBA_BLOCK
        nblk = n // MOBA_BLOCK
        vt_rows = d_b + (d_b // hd_b) * V_ROW_PAD
        outs += [((nblk, MOBA_BLOCK, d_b), BF16), ((nblk, vt_rows, MOBA_BLOCK), BF16), ((nblk, 1, d_b), F32)]
        specs += [pl.BlockSpec((1, MOBA_BLOCK, d_b), lambda i: (i, 0, 0)),
                  pl.BlockSpec((1, vt_rows, MOBA_BLOCK), lambda i: (i, 0, 0)),
                  pl.BlockSpec((1, 1, d_b), lambda i: (i, 0, 0))]
    outs += [((n, hp), F32)] * 4 + [((n, LANES), F32), ((n, 3 * d), F32)]
    specs += [pl.BlockSpec((tm, hp), row)] * 4 + [pl.BlockSpec((tm, LANES), row), pl.BlockSpec((tm, 3 * d), row)]
    body = functools.partial(_inproj_kernel, d_conv=d_conv, d_b=d_b, hp=hp, d_model=d,
                             q_scale=hd_b ** -0.5, gq_scale=dk_c ** -0.5, n_heads_b=d_b // hd_b,
                             attn_aux=attn_aux)
    return pl.pallas_call(
        body,
        grid=grid,
        in_specs=[
            pl.BlockSpec((tm, d), row),
            pl.BlockSpec((1, r, d), grp),
            pl.BlockSpec((1, r, d), grp),
            pl.BlockSpec((1, 1, d), lay),
            _resident((1, d, nw), lay),
        ],
        out_specs=specs,
        out_shape=[jax.ShapeDtypeStruct(s, t) for s, t in outs],
        compiler_params=_params("arbitrary"),
        name="inproj",
    )(x, sc, sh, g1, w_cat)


def _conv_taps(ext_ref, start, rows, wdw_ref, bdw_ref, lng_ref, lnb_ref, width, shift_scr=None):
    acc = jnp.zeros((rows, wdw_ref.shape[-1]), F32) + bdw_ref[0]
    if shift_scr is None:
        for k in range(width):
            acc = acc + ext_ref[pl.ds(start + k, rows), :] * wdw_ref[0, k:k + 1, :]
    else:
        span = shift_scr.shape[1]
        for r in range(1, SUBLANES):
            shift_scr[r - 1] = ext_ref[pl.ds(r, span), :]
        for k in range(width):
            a, r = divmod(start + k, SUBLANES)
            src = ext_ref if r == 0 else shift_scr.at[r - 1]
            acc = acc + src[pl.ds(a * SUBLANES, rows), :] * wdw_ref[0, k:k + 1, :]
    mu = jnp.mean(acc, axis=-1, keepdims=True)
    cen = acc - mu
    var = jnp.mean(cen * cen, axis=-1, keepdims=True)
    y = (cen * lax.rsqrt(var + EPS)) * lng_ref[0] + lnb_ref[0]
    return y * jax.nn.sigmoid(y)


def _conv_kernel(u_ref, hist_ref, wdw_ref, bdw_ref, lng_ref, lnb_ref, y_ref, tail_ref, ext_scr, shift_scr,
                 *, tt, width):
    j = pl.program_id(1)

    @pl.when(j == 0)
    def _():
        ext_scr[0:CONV_HALO, :] = hist_ref[0]

    @pl.when(j > 0)
    def _():
        ext_scr[0:CONV_HALO, :] = ext_scr[tt:tt + CONV_HALO, :]

    ext_scr[CONV_HALO:CONV_HALO + tt, :] = u_ref[0]
    start = CONV_HALO - (width - 1)
    y_ref[0] = _conv_taps(ext_scr, start, tt, wdw_ref, bdw_ref, lng_ref, lnb_ref, width, shift_scr).astype(BF16)

    @pl.when(j == pl.num_programs(1) - 1)
    def _():
        tail_ref[0] = ext_scr[tt:tt + CONV_HALO, :]


def _conv_prompt(u, hist, w_dw, b_dw, ln_g, ln_b, layer, *, tt):
    b, t, c = u.shape
    width = w_dw.shape[1]
    lay = lambda bi, j: (layer, 0, 0)
    return pl.pallas_call(
        functools.partial(_conv_kernel, tt=tt, width=width),
        grid=(b, t // tt),
        in_specs=[
            pl.BlockSpec((1, tt, c), lambda bi, j: (bi, j, 0)),
            pl.BlockSpec((1, CONV_HALO, c), lambda bi, j: (bi, 0, 0)),
            pl.BlockSpec((1, width, c), lay),
            pl.BlockSpec((1, 1, c), lay),
            pl.BlockSpec((1, 1, c), lay),
            pl.BlockSpec((1, 1, c), lay),
        ],
        out_specs=[pl.BlockSpec((1, tt, c), lambda bi, j: (bi, j, 0)),
                   pl.BlockSpec((1, CONV_HALO, c), lambda bi, j: (bi, 0, 0))],
        out_shape=[jax.ShapeDtypeStruct((b, t, c), BF16), jax.ShapeDtypeStruct((b, CONV_HALO, c), F32)],
        scratch_shapes=[pltpu.VMEM((CONV_HALO + tt, c), F32),
                        pltpu.VMEM((SUBLANES - 1, CONV_HALO + tt - SUBLANES, c), F32)],
        compiler_params=_params("arbitrary", "arbitrary"),
        name="conv_prompt",
    )(u, hist, w_dw, b_dw, ln_g, ln_b)


def _conv_small_kernel(ext_ref, wdw_ref, bdw_ref, lng_ref, lnb_ref, y_ref, *, rows, width):
    y_ref[0] = _conv_taps(ext_ref.at[0], 0, rows, wdw_ref, bdw_ref, lng_ref, lnb_ref, width).astype(BF16)


def _conv_sample(ext, w_dw, b_dw, ln_g, ln_b, layer, *, rows):
    b, r, c = ext.shape
    width = w_dw.shape[1]
    lay = lambda bi: (layer, 0, 0)
    return pl.pallas_call(
        functools.partial(_conv_small_kernel, rows=rows, width=width),
        grid=(b,),
        in_specs=[
            pl.BlockSpec((1, r, c), lambda bi: (bi, 0, 0)),
            pl.BlockSpec((1, width, c), lay),
            pl.BlockSpec((1, 1, c), lay),
            pl.BlockSpec((1, 1, c), lay),
            pl.BlockSpec((1, 1, c), lay),
        ],
        out_specs=pl.BlockSpec((1, rows, c), lambda bi: (bi, 0, 0)),
        out_shape=jax.ShapeDtypeStruct((b, rows, c), BF16),
        compiler_params=_params("arbitrary"),
        name="conv_sample",
    )(ext, w_dw, b_dw, ln_g, ln_b)


def _moba_prompt_kernel(q_ref, k16_ref, vt_ref, km_ref, bias_ref, o_ref,
                        qm_scr, sel_scr, m_scr, acc_scr, s2_scr, *, nb, n_heads, hd):
    i = pl.program_id(1)
    blk = MOBA_BLOCK
    q = q_ref[0]
    km = km_ref[0]
    lane = lax.broadcasted_iota(jnp.int32, (blk, LANES), 1)
    blk_idx = lax.broadcasted_iota(jnp.int32, (nb, blk), 0)
    heads_per_slab = LANES // hd

    km16 = km.astype(BF16)
    for h in range(n_heads):
        p, w = divmod(h, heads_per_slab)
        qm = jnp.where((lane // hd) == w, q[:, p * LANES:(p + 1) * LANES], 0.0)
        qm_scr[h] = (qm * LOG2E).astype(BF16)
        s = _dot_nt(km16[:, p * LANES:(p + 1) * LANES], qm.astype(BF16))
        s = jnp.where(blk_idx < i, s, -jnp.inf)
        sel = _top_k_mask(s, blk_idx, 0, MOBA_TOPK, nb)
        sel_scr[h] = jnp.where(sel, 1.0, 0.0)

    hv = vt_ref.shape[1] // n_heads

    def block(n, kind):
        kb = k16_ref[n]
        vb = vt_ref[n]

        for h in range(n_heads):
            p = h // heads_per_slab
            s2_scr[h, 0:blk, :] = _dot_nt(kb[:, p * LANES:(p + 1) * LANES], qm_scr[h])
        for h in range(n_heads):
            s = s2_scr[h, 0:blk, :]
            rows = slice(h * hv, (h + 1) * hv)
            if kind == "own":
                s = s + bias_ref[h, 0]
                m = jnp.max(s, axis=0, keepdims=True)
                m_scr[h:h + 1, :] = m
                acc_scr[rows, :] = _dot(vb[rows, :], jnp.exp2(s - m).astype(BF16))
            else:
                if kind == "prev":
                    s = s + bias_ref[h, 1]
                picked = sel_scr[h, pl.ds(n, 1), :] > 0.5
                m_old = m_scr[h:h + 1, :]
                m = jnp.where(picked, jnp.maximum(m_old, jnp.max(s, axis=0, keepdims=True)), m_old)
                alpha = jnp.exp2(m_old - m)
                pr = jnp.exp2(s - jnp.where(picked, m, -NEG))
                m_scr[h:h + 1, :] = m
                acc_scr[rows, :] = alpha * acc_scr[rows, :] + _dot(vb[rows, :], pr.astype(BF16))

    block(i, "own")

    @pl.when(i >= 1)
    def _():
        block(i - 1, "prev")

    def far_pair(n):
        kb = k16_ref[pl.ds(n, 2)].reshape(2 * blk, k16_ref.shape[-1])
        vb = jnp.concatenate([vt_ref[n], vt_ref[n + 1]], axis=1)
        for h in range(n_heads):
            p = h // heads_per_slab
            s2_scr[h] = _dot_nt(kb[:, p * LANES:(p + 1) * LANES], qm_scr[h])
        for h in range(n_heads):
            rows = slice(h * hv, (h + 1) * hv)
            s_a = s2_scr[h, 0:blk, :]
            s_b = s2_scr[h, blk:2 * blk, :]
            pick_a = sel_scr[h, pl.ds(n, 1), :] > 0.5
            pick_b = sel_scr[h, pl.ds(n + 1, 1), :] > 0.5
            m_old = m_scr[h:h + 1, :]
            m = jnp.maximum(m_old, jnp.maximum(
                jnp.where(pick_a, jnp.max(s_a, axis=0, keepdims=True), NEG),
                jnp.where(pick_b, jnp.max(s_b, axis=0, keepdims=True), NEG)))
            alpha = jnp.exp2(m_old - m)
            pr = jnp.concatenate([jnp.exp2(s_a - jnp.where(pick_a, m, -NEG)).astype(BF16),
                                  jnp.exp2(s_b - jnp.where(pick_b, m, -NEG)).astype(BF16)], axis=0)
            m_scr[h:h + 1, :] = m
            acc_scr[rows, :] = alpha * acc_scr[rows, :] + _dot(vb[rows, :], pr)

    n_far = jnp.maximum(i - 1, 0)

    def far(t, carry):
        far_pair(2 * t)
        return carry

    lax.fori_loop(0, n_far // 2, far, 0)

    @pl.when(n_far % 2 == 1)
    def _():
        block(n_far - 1, "far")

    out = [acc_scr[h * hv:h * hv + hd, :] / acc_scr[h * hv + hd:h * hv + hd + 1, :] for h in range(n_heads)]
    o_ref[0] = jnp.concatenate(out, axis=0).T.astype(BF16)


def _moba_prompt(q, k16, vt, km, bias_t, *, batch, n_heads, hd):
    n, d_b = q.shape
    t = n // batch
    nb = t // MOBA_BLOCK
    blk = MOBA_BLOCK
    body = functools.partial(_moba_prompt_kernel, nb=nb, n_heads=n_heads, hd=hd)
    return pl.pallas_call(
        body,
        grid=(batch, nb),
        in_specs=[
            pl.BlockSpec((1, blk, d_b), lambda b, i: (b * nb + i, 0, 0)),
            pl.BlockSpec((nb, blk, d_b), lambda b, i: (b, 0, 0)),
            pl.BlockSpec((nb, vt.shape[1], blk), lambda b, i: (b, 0, 0)),
            pl.BlockSpec((1, nb, d_b), lambda b, i: (b, 0, 0)),
            _resident((n_heads, 2, blk, blk), lambda b, i: (0, 0, 0, 0)),
        ],
        out_specs=pl.BlockSpec((1, blk, d_b), lambda b, i: (b * nb + i, 0, 0)),
        out_shape=jax.ShapeDtypeStruct((n // blk, blk, d_b), BF16),
        scratch_shapes=[
            pltpu.VMEM((n_heads, blk, LANES), BF16),
            pltpu.VMEM((n_heads, nb, blk), F32),
            pltpu.VMEM((n_heads, blk), F32),
            pltpu.VMEM((vt.shape[1], blk), F32),
            pltpu.VMEM((n_heads, 2 * blk, blk), F32),
        ],
        compiler_params=_params("arbitrary", "arbitrary"),
        name="moba_prompt",
    )(q.reshape(n // blk, blk, d_b), k16, vt, km.reshape(batch, nb, d_b), bias_t)


def _moba_sample_kernel(pt_ref, q_ref, kn_ref, vn_ref, bown_ref, btail_ref, *rest,
                        n_chunks, n_heads, t_new, page, pps):
    del pt_ref
    k_refs = rest[:pps]
    v_refs = rest[pps:2 * pps]
    o_ref = rest[2 * pps]
    qbd_scr, s_scr, bsum_scr, acc_scr, l_scr, kv_scr = rest[2 * pps + 1:]
    c = pl.program_id(1)
    rows, d_b = q_ref.shape[1:]
    hd = d_b // n_heads
    blocks_per_chunk = pps * page // MOBA_BLOCK
    pages_per_block = MOBA_BLOCK // page
    nbp = n_chunks * blocks_per_chunk

    row_i = lax.broadcasted_iota(jnp.int32, (rows, d_b), 0)
    lane_i = lax.broadcasted_iota(jnp.int32, (rows, d_b), 1)
    head_mask = (lane_i // hd) == (row_i % n_heads)
    blk_lane = lax.broadcasted_iota(jnp.int32, (rows, LANES), 1)

    @pl.when(c == 0)
    def _():
        qbd_scr[...] = jnp.where(head_mask, q_ref[0], 0.0).astype(BF16)
        bsum_scr[...] = jnp.zeros(bsum_scr.shape, F32)

    @pl.when(c < n_chunks)
    def _():
        qb16 = qbd_scr[...]
        bsum = bsum_scr[...]
        for p in range(pps):
            s = _dot(qb16, k_refs[p][0, 0].astype(BF16))
            s_scr[c, :, p * page:(p + 1) * page] = s
            ps = jnp.sum(s, axis=-1, keepdims=True)
            psum = ps if p % pages_per_block == 0 else psum + ps
            if p % pages_per_block == pages_per_block - 1:
                blk = c * blocks_per_chunk + p // pages_per_block
                bsum = jnp.where(blk_lane == blk, psum, bsum)
        bsum_scr[...] = bsum

    @pl.when(c == n_chunks)
    def _():
        bs = jnp.where(blk_lane < nbp, bsum_scr[...], -jnp.inf)
        sel = _top_k_mask(bs, blk_lane, 1, MOBA_TOPK, LANES)
        blk_bias = jnp.where(sel, 0.0, NEG)

        kv_scr[...] = jnp.zeros(kv_scr.shape, F32)
        kv_scr[0:kn_ref.shape[1], :] = kn_ref[0]
        s_own = _dot_nt(qbd_scr[...], kv_scr[...].astype(BF16)) + bown_ref[...]
        m = jnp.max(s_own, axis=-1, keepdims=True)

        for b in range(nbp):
            cc, off = divmod(b, blocks_per_chunk)
            cols = slice(off * MOBA_BLOCK, (off + 1) * MOBA_BLOCK)
            s = s_scr[cc, :, cols] + blk_bias[:, b:b + 1]
            if b == nbp - 1:
                s = s + btail_ref[...]
            s_scr[cc, :, cols] = s
            m = jnp.maximum(m, jnp.max(s, axis=-1, keepdims=True))

        p_own = jnp.exp(s_own - m)
        l = jnp.sum(p_own, axis=-1, keepdims=True)
        for cc in range(n_chunks):
            pr = jnp.exp(s_scr[cc] - m)
            l = l + jnp.sum(pr, axis=-1, keepdims=True)
            s_scr[cc] = pr
        kv_scr[0:vn_ref.shape[1], :] = vn_ref[0]
        acc_scr[...] = _dot(p_own.astype(BF16), kv_scr[...].astype(BF16))
        l_scr[...] = jnp.broadcast_to(l, l_scr.shape)

    @pl.when(c >= n_chunks)
    def _():
        cc = c - n_chunks
        acc = acc_scr[...]
        for p in range(pps):
            pr = s_scr[cc, :, p * page:(p + 1) * page]
            acc = acc + _dot_nt(pr.astype(BF16), v_refs[p][0, 0].astype(BF16))
        acc_scr[...] = acc

    @pl.when(c == 2 * n_chunks - 1)
    def _():
        o = jnp.where(head_mask, acc_scr[...] / l_scr[:, 0:1], 0.0)
        o_ref[0] = jnp.sum(o.reshape(t_new, n_heads, d_b), axis=1).astype(BF16)


def _moba_sample(page_table, q_rep, k_new, v_new, bias_own, bias_tail, cache_kt, cache_vt, layer,
                 *, n_heads, t_new):
    b, rows, d_b = q_rep.shape
    page = cache_kt.shape[-1]
    n_pages = page_table.shape[1]
    pps = min(PAGES_PER_STEP, n_pages)
    n_chunks = n_pages // pps
    assert n_pages % pps == 0 and (pps * page) % MOBA_BLOCK == 0 and MOBA_BLOCK % page == 0
    chunk = pps * page

    def k_map(r):
        return lambda bi, c, pt: (layer, pt[bi, jnp.minimum(c, n_chunks - 1) * pps + r], 0, 0)

    def v_map(r):
        def index(bi, c, pt):
            early = pt[jnp.maximum(bi - 1, 0), (n_chunks - 1) * pps + r]
            return (layer, jnp.where(c >= n_chunks, pt[bi, jnp.maximum(c - n_chunks, 0) * pps + r], early),
                    0, 0)
        return index

    per_b = lambda bi, c, pt: (bi, 0, 0)
    const2 = lambda bi, c, pt: (0, 0)
    in_specs = [
        pl.BlockSpec((1, rows, d_b), per_b),
        pl.BlockSpec((1, k_new.shape[1], d_b), per_b),
        pl.BlockSpec((1, v_new.shape[1], d_b), per_b),
        pl.BlockSpec(bias_own.shape, const2),
        pl.BlockSpec(bias_tail.shape, const2),
    ]
    in_specs += [pl.BlockSpec((1, 1, d_b, page), k_map(r)) for r in range(pps)]
    in_specs += [pl.BlockSpec((1, 1, d_b, page), v_map(r)) for r in range(pps)]
    body = functools.partial(_moba_sample_kernel, n_chunks=n_chunks, n_heads=n_heads, t_new=t_new,
                             page=page, pps=pps)
    grid_spec = pltpu.PrefetchScalarGridSpec(
        num_scalar_prefetch=1,
        grid=(b, 2 * n_chunks),
        in_specs=in_specs,
        out_specs=pl.BlockSpec((1, t_new, d_b), per_b),
        scratch_shapes=[
            pltpu.VMEM((rows, d_b), BF16),
            pltpu.VMEM((n_chunks, rows, chunk), F32),
            pltpu.VMEM((rows, LANES), F32),
            pltpu.VMEM((rows, d_b), F32),
            pltpu.VMEM((rows, LANES), F32),
            pltpu.VMEM((LANES, d_b), F32),
        ],
    )
    return pl.pallas_call(
        body,
        grid_spec=grid_spec,
        out_shape=jax.ShapeDtypeStruct((b, t_new, d_b), BF16),
        compiler_params=_params("arbitrary", "arbitrary"),
        name="moba_sample",
    )(page_table, q_rep, k_new, v_new, bias_own, bias_tail, *([cache_kt] * pps), *([cache_vt] * pps))


def _gla_kernel(gq_ref, gk_ref, gv_ref, gr_ref, lr_ref, wa_ref, ba_ref, gg_ref, st0_ref,
                o_ref, stf_ref, st_scr, bc_scr, bl_scr, qd_scr, kd_scr, vt_scr, o_scr,
                *, tt, valid, n_heads):
    j = pl.program_id(1)
    c_sz = GLA_CHUNK
    hp = n_heads * LANES

    @pl.when(j == 0)
    def _():
        st_scr[...] = st0_ref[0]

    x = _dot(lr_ref[0].astype(BF16), wa_ref[0]) + ba_ref[0]
    la = _log_sigmoid(x) * (1.0 / GATE_NORM)
    if valid < tt:
        la = jnp.where(lax.broadcasted_iota(jnp.int32, (tt, hp), 0) < valid, la, 0.0)
    r_i = lax.broadcasted_iota(jnp.int32, (tt, tt), 0)
    c_i = lax.broadcasted_iota(jnp.int32, (tt, tt), 1)
    low = jnp.where(((r_i // c_sz) == (c_i // c_sz)) & (c_i <= r_i), 1.0, 0.0).astype(BF16)
    la_hi = la.astype(BF16)
    la_lo = (la - la_hi.astype(F32)).astype(BF16)
    bc = _dot(low, la_hi) + _dot(low, la_lo)
    bc_scr[...] = bc
    n_ch = tt // c_sz
    in_ch = (lax.broadcasted_iota(jnp.int32, (n_ch, tt), 1) // c_sz
             == lax.broadcasted_iota(jnp.int32, (n_ch, tt), 0))
    tot = jnp.where(in_ch, 1.0, 0.0).astype(BF16)
    ends = _dot(tot, la_hi) + _dot(tot, la_lo)
    bl_scr[...] = ends
    bl = jnp.broadcast_to(ends[:, None, :], (n_ch, c_sz, hp)).reshape(tt, hp)
    qd_scr[...] = (gq_ref[0] * jnp.exp(bc)).astype(BF16)
    kd_scr[...] = (gk_ref[0] * jnp.exp(bl - bc)).astype(BF16)
    vt_scr[...] = gv_ref[0].T.astype(BF16)

    half = c_sz // 2
    t_half = lax.broadcasted_iota(jnp.int32, (half, LANES), 0)
    chunk_of_col = lax.broadcasted_iota(jnp.int32, (LANES, tt), 1) // c_sz

    def chunk(c, carry):
        rows = pl.ds(pl.multiple_of(c * c_sz, c_sz), c_sz)
        bc_c = bc_scr[rows, :]
        q_c = gq_ref[0, rows, :]
        k_c = gk_ref[0, rows, :]
        v_c = gv_ref[0, rows, :]
        decay = jnp.exp(bl_scr[pl.ds(c, 1), :])
        in_chunk = chunk_of_col == c
        for h in range(n_heads):
            hs = slice(h * LANES, (h + 1) * LANES)
            st = st_scr[h]
            o_h = _dot_nt(qd_scr[rows, hs], st.astype(BF16))
            bch = bc_c[:, hs]
            qh = q_c[:, hs]
            parts = [o_h[0:half], o_h[half:c_sz]]
            for s in range(c_sz):
                for hi in range(2):
                    r0 = hi * half
                    if s >= r0 + half:
                        continue
                    diff = bch[r0:r0 + half] - bch[s:s + 1, :]
                    if s > r0:
                        diff = jnp.where(t_half + r0 >= s, diff, -jnp.inf)
                    a_s = jnp.sum(qh[r0:r0 + half] * k_c[s:s + 1, hs] * jnp.exp(diff), axis=-1, keepdims=True)
                    parts[hi] = parts[hi] + a_s * v_c[s:s + 1, hs]
            o_scr[rows, hs] = jnp.concatenate(parts, axis=0)
            vtm = jnp.where(in_chunk, vt_scr[hs, :], jnp.zeros((), BF16))
            st_scr[h] = decay[:, hs] * st + _dot(vtm, kd_scr[:, hs])
        return carry

    lax.fori_loop(0, (valid + c_sz - 1) // c_sz, chunk, 0, unroll=GLA_UNROLL)

    o = o_scr[...]
    r = gr_ref[0]
    for h in range(n_heads):
        hs = slice(h * LANES, (h + 1) * LANES)
        oh = o[:, hs]
        oh = oh * lax.rsqrt(jnp.mean(oh * oh, axis=-1, keepdims=True) + EPS)
        rh = r[:, hs]
        o_ref[0, :, hs] = ((oh * gg_ref[0, :, hs]) * (rh * jax.nn.sigmoid(rh))).astype(BF16)

    @pl.when(j == pl.num_programs(1) - 1)
    def _():
        stf_ref[0] = st_scr[...]


def _gla(gq, gk, gv, gr, lr, wa, ba, gg, st0, layer, *, tt, valid):
    b, t, hp = gq.shape
    n_heads = hp // LANES
    lay = lambda bi, j: (layer, 0, 0)
    tile = lambda bi, j: (bi, j, 0)
    per_b = lambda bi, j: (bi, 0, 0, 0)
    body = functools.partial(_gla_kernel, tt=tt, valid=valid, n_heads=n_heads)
    return pl.pallas_call(
        body,
        grid=(b, t // tt),
        in_specs=[pl.BlockSpec((1, tt, hp), tile)] * 4 + [
            pl.BlockSpec((1, tt, LANES), tile),
            pl.BlockSpec((1, LANES, hp), lay),
            pl.BlockSpec((1, 1, hp), lay),
            pl.BlockSpec((1, 1, hp), lay),
            pl.BlockSpec((1, n_heads, LANES, LANES), per_b),
        ],
        out_specs=[pl.BlockSpec((1, tt, hp), tile), pl.BlockSpec((1, n_heads, LANES, LANES), per_b)],
        out_shape=[jax.ShapeDtypeStruct((b, t, hp), BF16),
                   jax.ShapeDtypeStruct((b, n_heads, LANES, LANES), F32)],
        scratch_shapes=[
            pltpu.VMEM((n_heads, LANES, LANES), F32),
            pltpu.VMEM((tt, hp), F32),
            pltpu.VMEM((tt // GLA_CHUNK, hp), F32),
            pltpu.VMEM((tt, hp), BF16),
            pltpu.VMEM((tt, hp), BF16),
            pltpu.VMEM((hp, tt), BF16),
            pltpu.VMEM((tt, hp), F32),
        ],
        compiler_params=_params("arbitrary", "arbitrary"),
        name="gla",
    )(gq, gk, gv, gr, lr, wa, ba, gg, st0)


def _merge_mlp_kernel(x_ref, ya_ref, ob_ref, oc_ref, gate_ref, gt1_ref, sc2_ref, sh2_ref, gt2_ref,
                      g2_ref, gf_ref, wa_ref, wb_ref, wc_ref, wo_ref, wup_ref, wdn_ref, *outs, final):
    d = x_ref.shape[-1]
    m = (gate_ref[:, 0:d] * _dot(ya_ref[...], wa_ref[0])
         + gate_ref[:, d:2 * d] * _dot(ob_ref[...], wb_ref[0])
         + gate_ref[:, 2 * d:3 * d] * _dot(oc_ref[...], wc_ref[0]))
    x1 = x_ref[...] + gt1_ref[0] * _dot(m.astype(BF16), wo_ref[0])
    h2 = _rms_rows(x1, g2_ref[0]) * (1.0 + sc2_ref[0]) + sh2_ref[0]
    up = jnp.maximum(_dot(h2.astype(BF16), wup_ref[0]), 0.0)
    x2 = x1 + gt2_ref[0] * _dot((up * up).astype(BF16), wdn_ref[0])
    outs[0][...] = x2
    if final:
        outs[1][...] = _rms_rows(x2, gf_ref[...])


def _merge_mlp(x, ya, ob, oc, gates, gt1, sc2, sh2, gt2, g2, gf, wa, wb, wc, wo, wup, wdn, layer,
               *, tm, tiles_per_group, final):
    n, d = x.shape
    c = ya.shape[-1]
    dff = wup.shape[-1]
    r = gt1.shape[1]
    row = lambda i: (i, 0)
    grp = lambda i: (i // tiles_per_group, 0, 0)
    lay = lambda i: (layer, 0, 0)
    n_out = 2 if final else 1
    return pl.pallas_call(
        functools.partial(_merge_mlp_kernel, final=final),
        grid=(n // tm,),
        in_specs=[pl.BlockSpec((tm, d), row)] + [pl.BlockSpec((tm, c), row)] * 3 + [
            pl.BlockSpec((tm, 3 * d), row),
            pl.BlockSpec((1, r, d), grp), pl.BlockSpec((1, r, d), grp),
            pl.BlockSpec((1, r, d), grp), pl.BlockSpec((1, r, d), grp),
            pl.BlockSpec((1, 1, d), lay),
            pl.BlockSpec((1, d), lambda i: (0, 0)),
            _resident((1, c, d), lay), _resident((1, c, d), lay), _resident((1, c, d), lay),
            _resident((1, d, d), lay), _resident((1, d, dff), lay), _resident((1, dff, d), lay),
        ],
        out_specs=[pl.BlockSpec((tm, d), row)] * n_out,
        out_shape=[jax.ShapeDtypeStruct((n, d), F32)] * n_out,
        compiler_params=_params("arbitrary"),
        name="merge_mlp",
    )(x, ya, ob, oc, gates, gt1, sc2, sh2, gt2, g2, gf, wa, wb, wc, wo, wup, wdn)


def _t5_bucket_table(n_buckets, max_dist):
    dist = np.arange(max_dist + 1)
    max_exact = n_buckets // 2
    d = np.maximum(dist, 1).astype(np.float32)
    large = max_exact + (np.log(d / max_exact) / math.log(MAX_DISTANCE / max_exact)
                         * (n_buckets - max_exact)).astype(np.int32)
    large = np.minimum(large, n_buckets - 1)
    return np.where(dist < max_exact, dist, large).astype(np.int32)


def _pad_heads(w, n_heads):
    k = w.shape[-1] // n_heads
    w = w.reshape(w.shape[:-1] + (n_heads, k))
    w = jnp.pad(w, [(0, 0)] * (w.ndim - 1) + [(0, LANES - k)])
    return w.reshape(w.shape[:-2] + (n_heads * LANES,))


def kernel(x_prompt, x_sample, c_prompt, c_sample, cache_k, cache_v, page_table, state_conv, state_gla,
           w_ada, b_ada, g_norm1, w_in, w_dw, b_dw, ln_g, ln_b, w_pw2, w_pb, rel_bias, w_a2, b_a, g_gla,
           w_pc, w_o, g_norm2, w_up, w_down, g_final):
    bp, t, d = x_prompt.shape
    bs, ts, _ = x_sample.shape
    depth = w_in.shape[0]
    _, n_pool, page, h_b, hd_b = cache_k.shape
    d_b = h_b * hd_b
    d_conv = w_dw.shape[-1]
    width = w_dw.shape[1]
    rank = w_a2.shape[1]
    h_c = GLA_HEADS
    dk_c = w_a2.shape[-1] // h_c
    dv_c = state_gla.shape[-1]
    hp = h_c * LANES
    n_buckets = rel_bias.shape[0]
    n_pages = page_table.shape[1]
    past_len = n_pages * page
    assert dv_c == LANES and h_c * dv_c == hp and width - 1 <= CONV_HALO

    sizes = (2 * d_conv, d_b, d_b, d_b, h_c * dk_c, h_c * dk_c, h_c * dv_c, h_c * dv_c, rank, d, d, d)
    offs = np.cumsum((0,) + sizes)
    part = [w_in[:, :, offs[i]:offs[i + 1]] for i in range(len(sizes))]
    w_cat = jnp.concatenate([
        part[0], part[1], part[2], part[3],
        _pad_heads(part[4], h_c), _pad_heads(part[5], h_c), part[6], part[7],
        jnp.pad(part[8], ((0, 0), (0, 0), (0, LANES - rank))),
        part[9], part[10], part[11]], axis=-1).astype(BF16)
    wa2 = jnp.pad(_pad_heads(w_a2, h_c), ((0, 0), (0, LANES - rank), (0, 0))).astype(BF16)
    ba2 = _pad_heads(b_a, h_c).reshape(depth, 1, hp)
    gg = g_gla.reshape(depth, 1, hp)
    wpw2, wpb, wpc, wo, wup, wdn = (w.astype(BF16) for w in (w_pw2, w_pb, w_pc, w_o, w_up, w_down))
    g1 = g_norm1.reshape(depth, 1, d)
    g2 = g_norm2.reshape(depth, 1, d)
    gf = g_final.reshape(1, d)
    bdw = b_dw.reshape(depth, 1, d_conv)
    lng = ln_g.reshape(depth, 1, d_conv)
    lnb = ln_b.reshape(depth, 1, d_conv)

    n_c = bp + bs
    rows_c = -(-n_c // 8) * 8
    c_all = jnp.pad(jnp.concatenate([c_prompt, c_sample], axis=0), ((0, rows_c - n_c), (0, 0)))
    mods = _adaln(c_all, w_ada, b_ada).reshape(depth, rows_c, 6, d)

    blk = MOBA_BLOCK
    assert ts <= blk and past_len % blk == 0
    bucket = _t5_bucket_table(n_buckets, 2 * blk - 1)
    onehot = np.zeros((4 * blk, n_buckets), np.float32)
    onehot[np.arange(2 * blk), bucket] = 1.0
    by_dist = jnp.dot(onehot, rel_bias - rel_bias[n_buckets - 1:n_buckets, :],
                      precision=lax.Precision.HIGHEST).T
    toep = jnp.tile(by_dist, (1, blk))[:, :blk * (4 * blk - 1)].reshape(h_b, blk, 4 * blk - 1)
    causal = np.arange(blk)[:, None] <= np.arange(blk)[None, :]
    bias_t = jnp.stack([jnp.where(causal, toep[:, :, :blk] * LOG2E, NEG), toep[:, :, blk:2 * blk] * LOG2E],
                       axis=1)
    rows_s = ts * h_b
    own = jnp.transpose(toep[:, :ts, :ts], (2, 0, 1)).reshape(rows_s, ts)
    own_ok = np.repeat(np.arange(ts), h_b)[:, None] >= np.arange(LANES)[None, :]
    bias_own = jnp.where(own_ok, jnp.pad(own, ((0, 0), (0, LANES - ts))), NEG)
    bias_tail = jnp.transpose(toep[:, :, blk:blk + ts], (2, 0, 1)).reshape(rows_s, blk)
    assert past_len // blk <= LANES
    cache_kt = jnp.transpose(cache_k, (0, 1, 3, 4, 2)).reshape(depth, n_pool, d_b, page)
    cache_vt = jnp.transpose(cache_v, (0, 1, 3, 4, 2)).reshape(depth, n_pool, d_b, page)

    xp = x_prompt.reshape(bp * t, d)
    xs = x_sample.reshape(bs * ts, d)
    n_s = bs * ts
    tm = TOKEN_TILE
    tpb = t // tm
    dims = (d_conv, d_b, hp, hd_b, dk_c)
    ts_pad = 8
    gla_rows_s = LANES

    kp_l, vp_l, cp_l, sp_l, ks_l, vs_l, cs_l, ss_l = ([] for _ in range(8))
    y_prompt = y_sample = None
    zeros_hist = jnp.zeros((bp, CONV_HALO, d_conv), F32)
    zeros_state = jnp.zeros((bp, h_c, LANES, LANES), F32)
    for l in range(depth):
        final = l == depth - 1
        mp = [mods[l, :bp, i].reshape(bp, 1, d) for i in range(6)]
        ms = [jnp.repeat(mods[l, bp:n_c, i], ts, axis=0).reshape(1, n_s, d) for i in range(6)]

        (u, qb, kb, vb, k16, vt, km, gq, gk, gv, gr, lr, gates) = _inproj(
            xp, mp[1], mp[0], g1, w_cat, l, tm=tm, tiles_per_group=tpb, dims=dims, attn_aux=True)
        ya, tail = _conv_prompt(u.reshape(bp, t, d_conv), zeros_hist, w_dw, bdw, lng, lnb, l, tt=tm)
        ob = _moba_prompt(qb, k16, vt, km, bias_t, batch=bp, n_heads=h_b, hd=hd_b)
        oc, stf = _gla(*(a.reshape(bp, t, -1) for a in (gq, gk, gv, gr, lr)), wa2, ba2, gg, zeros_state, l,
                       tt=tm, valid=tm)
        res = _merge_mlp(xp, ya.reshape(bp * t, d_conv), ob.reshape(bp * t, d_b), oc.reshape(bp * t, hp),
                         gates, mp[2], mp[4], mp[3], mp[5], g2, gf, wpw2, wpb, wpc, wo, wup, wdn, l,
                         tm=tm, tiles_per_group=tpb, final=final)
        xp = res[0]
        if final:
            y_prompt = res[1].reshape(bp, t, d)
        kp_l.append(kb.reshape(bp, t, h_b, hd_b))
        vp_l.append(vb.reshape(bp, t, h_b, hd_b))
        cp_l.append(tail[:, CONV_HALO - (width - 1):])
        sp_l.append(jnp.swapaxes(stf[..., :dk_c], -1, -2))

        (u, qb, kb, vb, gq, gk, gv, gr, lr, gates) = _inproj(
            xs, ms[1], ms[0], g1, w_cat, l, tm=n_s, tiles_per_group=1, dims=dims, attn_aux=False)
        ext = jnp.concatenate([state_conv[l], u.reshape(bs, ts, d_conv)], axis=1)
        ext_p = jnp.pad(ext, ((0, 0), (0, ts_pad + width - 1 - ext.shape[1] + (-(ts_pad + width - 1)) % 8), (0, 0)))
        ya = _conv_sample(ext_p, w_dw, bdw, lng, lnb, l, rows=ts_pad)[:, :ts]
        q_rep = jnp.repeat(qb.reshape(bs, ts, d_b), h_b, axis=1)
        pad_new = lambda a: jnp.pad(a.reshape(bs, ts, d_b), ((0, 0), (0, ts_pad - ts), (0, 0)))
        ob = _moba_sample(page_table, q_rep, pad_new(kb), pad_new(vb), bias_own, bias_tail,
                          cache_kt, cache_vt, l, n_heads=h_b, t_new=ts)
        pad_t = lambda a: jnp.pad(a.reshape(bs, ts, -1), ((0, 0), (0, gla_rows_s - ts), (0, 0)))
        st0 = jnp.pad(jnp.swapaxes(state_gla[l], -1, -2), ((0, 0), (0, 0), (0, 0), (0, LANES - dk_c)))
        oc, stf = _gla(pad_t(gq), pad_t(gk), pad_t(gv), pad_t(gr), pad_t(lr), wa2, ba2, gg, st0, l,
                       tt=gla_rows_s, valid=ts)
        res = _merge_mlp(xs, ya.reshape(n_s, d_conv), ob.reshape(n_s, d_b), oc[:, :ts].reshape(n_s, hp),
                         gates, ms[2], ms[4], ms[3], ms[5], g2, gf, wpw2, wpb, wpc, wo, wup, wdn, l,
                         tm=n_s, tiles_per_group=1, final=final)
        xs = res[0]
        if final:
            y_sample = res[1].reshape(bs, ts, d)
        ks_l.append(kb.reshape(bs, ts, h_b, hd_b))
        vs_l.append(vb.reshape(bs, ts, h_b, hd_b))
        cs_l.append(ext[:, -(width - 1):])
        ss_l.append(jnp.swapaxes(stf[..., :dk_c], -1, -2))

    return (y_prompt, y_sample, jnp.stack(kp_l), jnp.stack(vp_l), jnp.stack(cp_l), jnp.stack(sp_l),
            jnp.stack(ks_l), jnp.stack(vs_l), jnp.stack(cs_l), jnp.stack(ss_l))
```

```python
import functools
import math

import numpy as np
import jax
import jax.numpy as jnp
from jax import lax
from jax.experimental import pallas as pl
from jax.experimental.pallas import tpu as pltpu

F32 = jnp.float32
BF16 = jnp.bfloat16

MOBA_BLOCK = 256
MOBA_TOPK = 3
MAX_DISTANCE = 128
GLA_HEADS = 4
GLA_CHUNK = 16
GLA_UNROLL = 4
GATE_NORM = 16.0
EPS = 1e-6

LANES = 128
SUBLANES = 8
VMEM_LIMIT = 56 * 1024 * 1024

NEG = -1e30
LOG2E = math.log2(math.e)
V_ROW_PAD = 16
TOKEN_TILE = 256
PAGES_PER_STEP = 32
CONV_HALO = 32

_NT = (((1,), (1,)), ((), ()))


def _dot(a, b):
    return jnp.dot(a, b, preferred_element_type=F32)


def _dot_nt(a, b):
    return lax.dot_general(a, b, _NT, preferred_element_type=F32)


def _dot_f32(a, b):
    return jnp.dot(a, b, precision=lax.Precision.HIGHEST, preferred_element_type=F32)


def _dot_nt_f32(a, b):
    return lax.dot_general(a, b, _NT, precision=lax.Precision.HIGHEST, preferred_element_type=F32)


def _params(*sem):
    return pltpu.CompilerParams(dimension_semantics=sem, vmem_limit_bytes=VMEM_LIMIT)


def _resident(shape, index_map):
    return pl.BlockSpec(shape, index_map, pipeline_mode=pl.Buffered(1))


def _rms_rows(x, g):
    return (x * lax.rsqrt(jnp.mean(x * x, axis=-1, keepdims=True) + EPS)) * g


def _log_sigmoid(x):
    return jnp.minimum(x, 0.0) - jnp.log1p(jnp.exp(-jnp.abs(x)))


def _top_k_mask(s, idx, axis, k, sentinel):
    sel = jnp.zeros(s.shape, dtype=jnp.bool_)
    for _ in range(k):
        mx = jnp.max(s, axis=axis, keepdims=True)
        first = jnp.min(jnp.where(s == mx, idx, sentinel), axis=axis, keepdims=True)
        pick = (idx == first) & (mx > -jnp.inf)
        sel = sel | pick
        s = jnp.where(pick, -jnp.inf, s)
    return sel


def _adaln_kernel(c_ref, w_ref, b_ref, o_ref):
    c = c_ref[...]
    s = c * jax.nn.sigmoid(c)
    o_ref[0] = _dot(s.astype(BF16), w_ref[0].astype(BF16)) + b_ref[0]


def _adaln(c_all, w_ada, b_ada):
    depth, d, n = w_ada.shape
    rows = c_all.shape[0]
    tn = 1536
    return pl.pallas_call(
        _adaln_kernel,
        grid=(depth, n // tn),
        in_specs=[
            pl.BlockSpec((rows, d), lambda l, j: (0, 0)),
            pl.BlockSpec((1, d, tn), lambda l, j: (l, 0, j)),
            pl.BlockSpec((1, 1, tn), lambda l, j: (l, 0, j)),
        ],
        out_specs=pl.BlockSpec((1, rows, tn), lambda l, j: (l, 0, j)),
        out_shape=jax.ShapeDtypeStruct((depth, rows, n), F32),
        compiler_params=_params("arbitrary", "arbitrary"),
        name="adaln",
    )(c_all, w_ada, b_ada.reshape(depth, 1, n))


def _inproj_kernel(x_ref, sc_ref, sh_ref, g_ref, w_ref, *rest, d_conv, d_b, hp, d_model,
                   q_scale, gq_scale, n_heads_b, attn_aux, n_alias):
    outs = rest[n_alias:]
    if attn_aux:
        (u_ref, q_ref, k_ref, v_ref, k16_ref, vt_ref, km_ref,
         gq_ref, gk_ref, gv_ref, gr_ref, lr_ref, gate_ref) = outs
    else:
        (u_ref, q_ref, k_ref, v_ref, gq_ref, gk_ref, gv_ref, gr_ref, lr_ref, gate_ref) = outs
    x = x_ref[...]
    h = _rms_rows(x, g_ref[0]) * (1.0 + sc_ref[0, 0, 0]) + sh_ref[0, 0, 0]
    hb = h.astype(BF16)

    pos = [0]

    def seg(width):
        lo = pos[0]
        pos[0] = lo + width
        return _dot(hb, w_ref[0, :, lo:lo + width])

    a = seg(d_conv)
    g = seg(d_conv)
    u_ref[...] = a * jax.nn.sigmoid(g)
    q_ref[...] = seg(d_b) * q_scale
    k = seg(d_b)
    k_ref[0] = k
    v = seg(d_b)
    v_ref[0] = v
    if attn_aux:
        k16_ref[0] = k.astype(BF16)
        vt = v.T.astype(BF16)
        hd = d_b // n_heads_b
        tail = jnp.where(lax.broadcasted_iota(jnp.int32, (V_ROW_PAD, vt.shape[1]), 0) == 0, 1.0, 0.0).astype(BF16)
        for h in range(n_heads_b):
            vt_ref[0, h * (hd + V_ROW_PAD):h * (hd + V_ROW_PAD) + hd, :] = vt[h * hd:(h + 1) * hd, :]
            vt_ref[0, h * (hd + V_ROW_PAD) + hd:(h + 1) * (hd + V_ROW_PAD), :] = tail
        km_ref[0] = jnp.mean(k, axis=0, keepdims=True)
    gq_ref[...] = seg(hp) * gq_scale
    gk_ref[...] = seg(hp)
    gv_ref[...] = seg(hp)
    gr_ref[...] = seg(hp)
    lr_ref[...] = seg(LANES)
    for i in range(3):
        gate_ref[:, i * d_model:(i + 1) * d_model] = jax.nn.sigmoid(seg(d_model))


def _mod_spec(mods, idx, layer, tiles_per_group):
    return pl.BlockSpec((1, 1, 1) + mods.shape[3:], lambda i: (layer, idx, i // tiles_per_group, 0, 0))


def _inproj(x, mods, g1, w_cat, kv_bufs, layer, *, tm, tiles_per_group, dims, attn_aux):
    n, d = x.shape
    d_conv, d_b, hp, hd_b, dk_c = dims
    depth, _, nw = w_cat.shape
    grid = (n // tm,)
    row = lambda i: (i, 0)
    lay = lambda i: (layer, 0, 0)
    kv_spec = pl.BlockSpec((1, tm, d_b), lambda i: (layer, i, 0))
    outs = [((n, d_conv), F32), ((n, d_b), F32), ((depth, n, d_b), F32), ((depth, n, d_b), F32)]
    specs = [pl.BlockSpec((tm, d_conv), row), pl.BlockSpec((tm, d_b), row), kv_spec, kv_spec]
    if attn_aux:
        assert tm == MOBA_BLOCK
        nblk = n // MOBA_BLOCK
        vt_rows = d_b + (d_b // hd_b) * V_ROW_PAD
        outs += [((nblk, MOBA_BLOCK, d_b), BF16), ((nblk, vt_rows, MOBA_BLOCK), BF16), ((nblk, 1, d_b), F32)]
        specs += [pl.BlockSpec((1, MOBA_BLOCK, d_b), lambda i: (i, 0, 0)),
                  pl.BlockSpec((1, vt_rows, MOBA_BLOCK), lambda i: (i, 0, 0)),
                  pl.BlockSpec((1, 1, d_b), lambda i: (i, 0, 0))]
    outs += [((n, hp), F32)] * 4 + [((n, LANES), F32), ((n, 3 * d), F32)]
    specs += [pl.BlockSpec((tm, hp), row)] * 4 + [pl.BlockSpec((tm, LANES), row), pl.BlockSpec((tm, 3 * d), row)]
    body = functools.partial(_inproj_kernel, d_conv=d_conv, d_b=d_b, hp=hp, d_model=d,
                             q_scale=hd_b ** -0.5, gq_scale=dk_c ** -0.5, n_heads_b=d_b // hd_b,
                             attn_aux=attn_aux, n_alias=0 if kv_bufs is None else 2)
    in_specs = [
        pl.BlockSpec((tm, d), row),
        _mod_spec(mods, 1, layer, tiles_per_group),
        _mod_spec(mods, 0, layer, tiles_per_group),
        pl.BlockSpec((1, 1, d), lay),
        _resident((1, d, nw), lay),
    ]
    args = [x, mods, mods, g1, w_cat]
    aliases = {}
    if kv_bufs is not None:
        in_specs += [pl.BlockSpec(memory_space=pl.ANY)] * 2
        aliases = {len(args): 2, len(args) + 1: 3}
        args += list(kv_bufs)
    return pl.pallas_call(
        body,
        grid=grid,
        in_specs=in_specs,
        out_specs=specs,
        out_shape=[jax.ShapeDtypeStruct(s, t) for s, t in outs],
        input_output_aliases=aliases,
        compiler_params=_params("arbitrary"),
        name="inproj",
    )(*args)


def _conv_taps(ext_ref, start, rows, wdw_ref, bdw_ref, lng_ref, lnb_ref, width, shift_scr=None):
    acc = jnp.zeros((rows, wdw_ref.shape[-1]), F32) + bdw_ref[0]
    if shift_scr is None:
        for k in range(width):
            acc = acc + ext_ref[pl.ds(start + k, rows), :] * wdw_ref[0, k:k + 1, :]
    else:
        span = shift_scr.shape[1]
        for r in range(1, SUBLANES):
            shift_scr[r - 1] = ext_ref[pl.ds(r, span), :]
        for k in range(width):
            a, r = divmod(start + k, SUBLANES)
            src = ext_ref if r == 0 else shift_scr.at[r - 1]
            acc = acc + src[pl.ds(a * SUBLANES, rows), :] * wdw_ref[0, k:k + 1, :]
    mu = jnp.mean(acc, axis=-1, keepdims=True)
    cen = acc - mu
    var = jnp.mean(cen * cen, axis=-1, keepdims=True)
    y = (cen * lax.rsqrt(var + EPS)) * lng_ref[0] + lnb_ref[0]
    return y * jax.nn.sigmoid(y)


def _conv_kernel(u_ref, hist_ref, wdw_ref, bdw_ref, lng_ref, lnb_ref, y_ref, tail_ref, ext_scr, shift_scr,
                 *, tt, width):
    j = pl.program_id(1)

    @pl.when(j == 0)
    def _():
        ext_scr[0:CONV_HALO, :] = hist_ref[0]

    @pl.when(j > 0)
    def _():
        ext_scr[0:CONV_HALO, :] = ext_scr[tt:tt + CONV_HALO, :]

    ext_scr[CONV_HALO:CONV_HALO + tt, :] = u_ref[0]
    start = CONV_HALO - (width - 1)
    y_ref[0] = _conv_taps(ext_scr, start, tt, wdw_ref, bdw_ref, lng_ref, lnb_ref, width, shift_scr).astype(BF16)

    @pl.when(j == pl.num_programs(1) - 1)
    def _():
        tail_ref[0] = ext_scr[tt:tt + CONV_HALO, :]


def _conv_prompt(u, hist, w_dw, b_dw, ln_g, ln_b, layer, *, tt):
    b, t, c = u.shape
    width = w_dw.shape[1]
    lay = lambda bi, j: (layer, 0, 0)
    return pl.pallas_call(
        functools.partial(_conv_kernel, tt=tt, width=width),
        grid=(b, t // tt),
        in_specs=[
            pl.BlockSpec((1, tt, c), lambda bi, j: (bi, j, 0)),
            pl.BlockSpec((1, CONV_HALO, c), lambda bi, j: (bi, 0, 0)),
            pl.BlockSpec((1, width, c), lay),
            pl.BlockSpec((1, 1, c), lay),
            pl.BlockSpec((1, 1, c), lay),
            pl.BlockSpec((1, 1, c), lay),
        ],
        out_specs=[pl.BlockSpec((1, tt, c), lambda bi, j: (bi, j, 0)),
                   pl.BlockSpec((1, CONV_HALO, c), lambda bi, j: (bi, 0, 0))],
        out_shape=[jax.ShapeDtypeStruct((b, t, c), BF16), jax.ShapeDtypeStruct((b, CONV_HALO, c), F32)],
        scratch_shapes=[pltpu.VMEM((CONV_HALO + tt, c), F32),
                        pltpu.VMEM((SUBLANES - 1, CONV_HALO + tt - SUBLANES, c), F32)],
        compiler_params=_params("arbitrary", "arbitrary"),
        name="conv_prompt",
    )(u, hist, w_dw, b_dw, ln_g, ln_b)


def _conv_small_kernel(ext_ref, wdw_ref, bdw_ref, lng_ref, lnb_ref, y_ref, *, rows, width):
    y_ref[0] = _conv_taps(ext_ref.at[0], 0, rows, wdw_ref, bdw_ref, lng_ref, lnb_ref, width).astype(BF16)


def _conv_sample(ext, w_dw, b_dw, ln_g, ln_b, layer, *, rows):
    b, r, c = ext.shape
    width = w_dw.shape[1]
    lay = lambda bi: (layer, 0, 0)
    return pl.pallas_call(
        functools.partial(_conv_small_kernel, rows=rows, width=width),
        grid=(b,),
        in_specs=[
            pl.BlockSpec((1, r, c), lambda bi: (bi, 0, 0)),
            pl.BlockSpec((1, width, c), lay),
            pl.BlockSpec((1, 1, c), lay),
            pl.BlockSpec((1, 1, c), lay),
            pl.BlockSpec((1, 1, c), lay),
        ],
        out_specs=pl.BlockSpec((1, rows, c), lambda bi: (bi, 0, 0)),
        out_shape=jax.ShapeDtypeStruct((b, rows, c), BF16),
        compiler_params=_params("arbitrary"),
        name="conv_sample",
    )(ext, w_dw, b_dw, ln_g, ln_b)


def _moba_prompt_kernel(q_ref, k16_ref, vt_ref, km_ref, bias_ref, o_ref,
                        qm_scr, sel_scr, m_scr, acc_scr, s2_scr, *, nb, n_heads, hd):
    i = pl.program_id(1)
    blk = MOBA_BLOCK
    q = q_ref[0]
    km = km_ref[0]
    lane = lax.broadcasted_iota(jnp.int32, (blk, LANES), 1)
    blk_idx = lax.broadcasted_iota(jnp.int32, (nb, blk), 0)
    heads_per_slab = LANES // hd

    km16 = km.astype(BF16)
    for h in range(n_heads):
        p, w = divmod(h, heads_per_slab)
        qm = jnp.where((lane // hd) == w, q[:, p * LANES:(p + 1) * LANES], 0.0)
        qm_scr[h] = (qm * LOG2E).astype(BF16)
        s = _dot_nt(km16[:, p * LANES:(p + 1) * LANES], qm.astype(BF16))
        s = jnp.where(blk_idx < i, s, -jnp.inf)
        sel = _top_k_mask(s, blk_idx, 0, MOBA_TOPK, nb)
        sel_scr[h] = jnp.where(sel, 1.0, 0.0)

    hv = vt_ref.shape[1] // n_heads

    def block(n, kind):
        kb = k16_ref[n]
        vb = vt_ref[n]

        for h in range(n_heads):
            p = h // heads_per_slab
            s2_scr[h, 0:blk, :] = _dot_nt(kb[:, p * LANES:(p + 1) * LANES], qm_scr[h])
        for h in range(n_heads):
            s = s2_scr[h, 0:blk, :]
            rows = slice(h * hv, (h + 1) * hv)
            if kind == "own":
                s = s + bias_ref[h, 0]
                m = jnp.max(s, axis=0, keepdims=True)
                m_scr[h:h + 1, :] = m
                acc_scr[rows, :] = _dot(vb[rows, :], jnp.exp2(s - m).astype(BF16))
            else:
                if kind == "prev":
                    s = s + bias_ref[h, 1]
                picked = sel_scr[h, pl.ds(n, 1), :] > 0.5
                m_old = m_scr[h:h + 1, :]
                m = jnp.where(picked, jnp.maximum(m_old, jnp.max(s, axis=0, keepdims=True)), m_old)
                alpha = jnp.exp2(m_old - m)
                pr = jnp.exp2(s - jnp.where(picked, m, -NEG))
                m_scr[h:h + 1, :] = m
                acc_scr[rows, :] = alpha * acc_scr[rows, :] + _dot(vb[rows, :], pr.astype(BF16))

    block(i, "own")

    @pl.when(i >= 1)
    def _():
        block(i - 1, "prev")

    def far_pair(n):
        kb = k16_ref[pl.ds(n, 2)].reshape(2 * blk, k16_ref.shape[-1])
        vb = jnp.concatenate([vt_ref[n], vt_ref[n + 1]], axis=1)
        for h in range(n_heads):
            p = h // heads_per_slab
            s2_scr[h] = _dot_nt(kb[:, p * LANES:(p + 1) * LANES], qm_scr[h])
        for h in range(n_heads):
            rows = slice(h * hv, (h + 1) * hv)
            s_a = s2_scr[h, 0:blk, :]
            s_b = s2_scr[h, blk:2 * blk, :]
            pick_a = sel_scr[h, pl.ds(n, 1), :] > 0.5
            pick_b = sel_scr[h, pl.ds(n + 1, 1), :] > 0.5
            m_old = m_scr[h:h + 1, :]
            m = jnp.maximum(m_old, jnp.maximum(
                jnp.where(pick_a, jnp.max(s_a, axis=0, keepdims=True), NEG),
                jnp.where(pick_b, jnp.max(s_b, axis=0, keepdims=True), NEG)))
            alpha = jnp.exp2(m_old - m)
            pr = jnp.concatenate([jnp.exp2(s_a - jnp.where(pick_a, m, -NEG)).astype(BF16),
                                  jnp.exp2(s_b - jnp.where(pick_b, m, -NEG)).astype(BF16)], axis=0)
            m_scr[h:h + 1, :] = m
            acc_scr[rows, :] = alpha * acc_scr[rows, :] + _dot(vb[rows, :], pr)

    n_far = jnp.maximum(i - 1, 0)

    def far(t, carry):
        far_pair(2 * t)
        return carry

    lax.fori_loop(0, n_far // 2, far, 0)

    @pl.when(n_far % 2 == 1)
    def _():
        block(n_far - 1, "far")

    out = [acc_scr[h * hv:h * hv + hd, :] / acc_scr[h * hv + hd:h * hv + hd + 1, :] for h in range(n_heads)]
    o_ref[0] = jnp.concatenate(out, axis=0).T.astype(BF16)


def _moba_prompt(q, k16, vt, km, bias_t, *, batch, n_heads, hd):
    n, d_b = q.shape
    t = n // batch
    nb = t // MOBA_BLOCK
    blk = MOBA_BLOCK
    body = functools.partial(_moba_prompt_kernel, nb=nb, n_heads=n_heads, hd=hd)
    return pl.pallas_call(
        body,
        grid=(batch, nb),
        in_specs=[
            pl.BlockSpec((1, blk, d_b), lambda b, i: (b * nb + i, 0, 0)),
            pl.BlockSpec((nb, blk, d_b), lambda b, i: (b, 0, 0)),
            pl.BlockSpec((nb, vt.shape[1], blk), lambda b, i: (b, 0, 0)),
            pl.BlockSpec((1, nb, d_b), lambda b, i: (b, 0, 0)),
            _resident((n_heads, 2, blk, blk), lambda b, i: (0, 0, 0, 0)),
        ],
        out_specs=pl.BlockSpec((1, blk, d_b), lambda b, i: (b * nb + i, 0, 0)),
        out_shape=jax.ShapeDtypeStruct((n // blk, blk, d_b), BF16),
        scratch_shapes=[
            pltpu.VMEM((n_heads, blk, LANES), BF16),
            pltpu.VMEM((n_heads, nb, blk), F32),
            pltpu.VMEM((n_heads, blk), F32),
            pltpu.VMEM((vt.shape[1], blk), F32),
            pltpu.VMEM((n_heads, 2 * blk, blk), F32),
        ],
        compiler_params=_params("arbitrary", "arbitrary"),
        name="moba_prompt",
    )(q.reshape(n // blk, blk, d_b), k16, vt, km.reshape(batch, nb, d_b), bias_t)


def _moba_sample_kernel(pt_ref, q_ref, kn_ref, vn_ref, bown_ref, btail_ref, *rest,
                        n_chunks, n_heads, t_new, page, pps):
    del pt_ref
    k_refs = rest[:pps]
    v_refs = rest[pps:2 * pps]
    o_ref = rest[2 * pps]
    qbd_scr, s_scr, bsum_scr, acc_scr, l_scr, kv_scr = rest[2 * pps + 1:]
    c = pl.program_id(1)
    rows, d_b = q_ref.shape[1:]
    hd = d_b // n_heads
    blocks_per_chunk = pps * page // MOBA_BLOCK
    pages_per_block = MOBA_BLOCK // page
    nbp = n_chunks * blocks_per_chunk

    row_i = lax.broadcasted_iota(jnp.int32, (rows, d_b), 0)
    lane_i = lax.broadcasted_iota(jnp.int32, (rows, d_b), 1)
    head_mask = (lane_i // hd) == (row_i % n_heads)
    blk_lane = lax.broadcasted_iota(jnp.int32, (rows, LANES), 1)

    @pl.when(c == 0)
    def _():
        qbd_scr[...] = jnp.where(head_mask, q_ref[0], 0.0).astype(BF16)
        bsum_scr[...] = jnp.zeros(bsum_scr.shape, F32)

    @pl.when(c < n_chunks)
    def _():
        qb16 = qbd_scr[...]
        bsum = bsum_scr[...]
        for p in range(pps):
            s = _dot(qb16, k_refs[p][0, 0].astype(BF16))
            s_scr[c, :, p * page:(p + 1) * page] = s
            ps = jnp.sum(s, axis=-1, keepdims=True)
            psum = ps if p % pages_per_block == 0 else psum + ps
            if p % pages_per_block == pages_per_block - 1:
                blk = c * blocks_per_chunk + p // pages_per_block
                bsum = jnp.where(blk_lane == blk, psum, bsum)
        bsum_scr[...] = bsum

    @pl.when(c == n_chunks)
    def _():
        bs = jnp.where(blk_lane < nbp, bsum_scr[...], -jnp.inf)
        sel = _top_k_mask(bs, blk_lane, 1, MOBA_TOPK, LANES)
        blk_bias = jnp.where(sel, 0.0, NEG)

        kv_scr[...] = jnp.zeros(kv_scr.shape, F32)
        kv_scr[0:kn_ref.shape[1], :] = kn_ref[0]
        s_own = _dot_nt(qbd_scr[...], kv_scr[...].astype(BF16)) + bown_ref[...]
        m = jnp.max(s_own, axis=-1, keepdims=True)

        for b in range(nbp):
            cc, off = divmod(b, blocks_per_chunk)
            cols = slice(off * MOBA_BLOCK, (off + 1) * MOBA_BLOCK)
            s = s_scr[cc, :, cols] + blk_bias[:, b:b + 1]
            if b == nbp - 1:
                s = s + btail_ref[...]
            s_scr[cc, :, cols] = s
            m = jnp.maximum(m, jnp.max(s, axis=-1, keepdims=True))

        p_own = jnp.exp(s_own - m)
        l = jnp.sum(p_own, axis=-1, keepdims=True)
        for cc in range(n_chunks):
            pr = jnp.exp(s_scr[cc] - m)
            l = l + jnp.sum(pr, axis=-1, keepdims=True)
            s_scr[cc] = pr
        kv_scr[0:vn_ref.shape[1], :] = vn_ref[0]
        acc_scr[...] = _dot(p_own.astype(BF16), kv_scr[...].astype(BF16))
        l_scr[...] = jnp.broadcast_to(l, l_scr.shape)

    @pl.when(c >= n_chunks)
    def _():
        cc = c - n_chunks
        acc = acc_scr[...]
        for p in range(pps):
            pr = s_scr[cc, :, p * page:(p + 1) * page]
            acc = acc + _dot_nt(pr.astype(BF16), v_refs[p][0, 0].astype(BF16))
        acc_scr[...] = acc

    @pl.when(c == 2 * n_chunks - 1)
    def _():
        o = jnp.where(head_mask, acc_scr[...] / l_scr[:, 0:1], 0.0)
        o_ref[0] = jnp.sum(o.reshape(t_new, n_heads, d_b), axis=1).astype(BF16)


def _moba_sample(page_table, q_rep, k_new, v_new, bias_own, bias_tail, cache_kt, cache_vt, layer,
                 *, n_heads, t_new):
    b, rows, d_b = q_rep.shape
    page = cache_kt.shape[-1]
    n_pages = page_table.shape[1]
    pps = min(PAGES_PER_STEP, n_pages)
    n_chunks = n_pages // pps
    assert n_pages % pps == 0 and (pps * page) % MOBA_BLOCK == 0 and MOBA_BLOCK % page == 0
    chunk = pps * page

    def k_map(r):
        return lambda bi, c, pt: (layer, pt[bi, jnp.minimum(c, n_chunks - 1) * pps + r], 0, 0)

    def v_map(r):
        def index(bi, c, pt):
            early = pt[jnp.maximum(bi - 1, 0), (n_chunks - 1) * pps + r]
            return (layer, jnp.where(c >= n_chunks, pt[bi, jnp.maximum(c - n_chunks, 0) * pps + r], early),
                    0, 0)
        return index

    per_b = lambda bi, c, pt: (bi, 0, 0)
    const2 = lambda bi, c, pt: (0, 0)
    in_specs = [
        pl.BlockSpec((1, rows, d_b), per_b),
        pl.BlockSpec((1, k_new.shape[1], d_b), per_b),
        pl.BlockSpec((1, v_new.shape[1], d_b), per_b),
        pl.BlockSpec(bias_own.shape, const2),
        pl.BlockSpec(bias_tail.shape, const2),
    ]
    in_specs += [pl.BlockSpec((1, 1, d_b, page), k_map(r)) for r in range(pps)]
    in_specs += [pl.BlockSpec((1, 1, d_b, page), v_map(r)) for r in range(pps)]
    body = functools.partial(_moba_sample_kernel, n_chunks=n_chunks, n_heads=n_heads, t_new=t_new,
                             page=page, pps=pps)
    grid_spec = pltpu.PrefetchScalarGridSpec(
        num_scalar_prefetch=1,
        grid=(b, 2 * n_chunks),
        in_specs=in_specs,
        out_specs=pl.BlockSpec((1, t_new, d_b), per_b),
        scratch_shapes=[
            pltpu.VMEM((rows, d_b), BF16),
            pltpu.VMEM((n_chunks, rows, chunk), F32),
            pltpu.VMEM((rows, LANES), F32),
            pltpu.VMEM((rows, d_b), F32),
            pltpu.VMEM((rows, LANES), F32),
            pltpu.VMEM((LANES, d_b), F32),
        ],
    )
    return pl.pallas_call(
        body,
        grid_spec=grid_spec,
        out_shape=jax.ShapeDtypeStruct((b, t_new, d_b), BF16),
        compiler_params=_params("arbitrary", "arbitrary"),
        name="moba_sample",
    )(page_table, q_rep, k_new, v_new, bias_own, bias_tail, *([cache_kt] * pps), *([cache_vt] * pps))


def _gla_kernel(gq_ref, gk_ref, gv_ref, gr_ref, lr_ref, wa_ref, ba_ref, gg_ref, st0_ref,
                o_ref, stf_ref, st_scr, bc_scr, bl_scr, qd_scr, kd_scr, vt_scr, o_scr,
                *, tt, valid, n_heads):
    j = pl.program_id(1)
    c_sz = GLA_CHUNK
    hp = n_heads * LANES

    @pl.when(j == 0)
    def _():
        st_scr[...] = st0_ref[0]

    x = _dot(lr_ref[0].astype(BF16), wa_ref[0]) + ba_ref[0]
    la = _log_sigmoid(x) * (1.0 / GATE_NORM)
    if valid < tt:
        la = jnp.where(lax.broadcasted_iota(jnp.int32, (tt, hp), 0) < valid, la, 0.0)
    r_i = lax.broadcasted_iota(jnp.int32, (tt, tt), 0)
    c_i = lax.broadcasted_iota(jnp.int32, (tt, tt), 1)
    low = jnp.where(((r_i // c_sz) == (c_i // c_sz)) & (c_i <= r_i), 1.0, 0.0).astype(BF16)
    la_hi = la.astype(BF16)
    la_lo = (la - la_hi.astype(F32)).astype(BF16)
    bc = _dot(low, la_hi) + _dot(low, la_lo)
    bc_scr[...] = bc
    n_ch = tt // c_sz
    in_ch = (lax.broadcasted_iota(jnp.int32, (n_ch, tt), 1) // c_sz
             == lax.broadcasted_iota(jnp.int32, (n_ch, tt), 0))
    tot = jnp.where(in_ch, 1.0, 0.0).astype(BF16)
    ends = _dot(tot, la_hi) + _dot(tot, la_lo)
    bl_scr[...] = ends
    bl = jnp.broadcast_to(ends[:, None, :], (n_ch, c_sz, hp)).reshape(tt, hp)
    qd_scr[...] = (gq_ref[0] * jnp.exp(bc)).astype(BF16)
    kd_scr[...] = (gk_ref[0] * jnp.exp(bl - bc)).astype(BF16)
    vt_scr[...] = gv_ref[0].T.astype(BF16)

    half = c_sz // 2
    t_half = lax.broadcasted_iota(jnp.int32, (half, LANES), 0)
    chunk_of_col = lax.broadcasted_iota(jnp.int32, (LANES, tt), 1) // c_sz

    def chunk(c, carry):
        rows = pl.ds(pl.multiple_of(c * c_sz, c_sz), c_sz)
        bc_c = bc_scr[rows, :]
        q_c = gq_ref[0, rows, :]
        k_c = gk_ref[0, rows, :]
        v_c = gv_ref[0, rows, :]
        decay = jnp.exp(bl_scr[pl.ds(c, 1), :])
        in_chunk = chunk_of_col == c
        for h in range(n_heads):
            hs = slice(h * LANES, (h + 1) * LANES)
            st = st_scr[h]
            o_h = _dot_nt(qd_scr[rows, hs], st.astype(BF16))
            bch = bc_c[:, hs]
            qh = q_c[:, hs]
            parts = [o_h[0:half], o_h[half:c_sz]]
            for s in range(c_sz):
                for hi in range(2):
                    r0 = hi * half
                    if s >= r0 + half:
                        continue
                    diff = bch[r0:r0 + half] - bch[s:s + 1, :]
                    if s > r0:
                        diff = jnp.where(t_half + r0 >= s, diff, -jnp.inf)
                    a_s = jnp.sum(qh[r0:r0 + half] * k_c[s:s + 1, hs] * jnp.exp(diff), axis=-1, keepdims=True)
                    parts[hi] = parts[hi] + a_s * v_c[s:s + 1, hs]
            o_scr[rows, hs] = jnp.concatenate(parts, axis=0)
            vtm = jnp.where(in_chunk, vt_scr[hs, :], jnp.zeros((), BF16))
            st_scr[h] = decay[:, hs] * st + _dot(vtm, kd_scr[:, hs])
        return carry

    lax.fori_loop(0, (valid + c_sz - 1) // c_sz, chunk, 0, unroll=GLA_UNROLL)

    o = o_scr[...]
    r = gr_ref[0]
    for h in range(n_heads):
        hs = slice(h * LANES, (h + 1) * LANES)
        oh = o[:, hs]
        oh = oh * lax.rsqrt(jnp.mean(oh * oh, axis=-1, keepdims=True) + EPS)
        rh = r[:, hs]
        o_ref[0, :, hs] = ((oh * gg_ref[0, :, hs]) * (rh * jax.nn.sigmoid(rh))).astype(BF16)

    @pl.when(j == pl.num_programs(1) - 1)
    def _():
        stf_ref[0] = st_scr[...]


def _gla(gq, gk, gv, gr, lr, wa, ba, gg, st0, layer, *, tt, valid):
    b, t, hp = gq.shape
    n_heads = hp // LANES
    lay = lambda bi, j: (layer, 0, 0)
    tile = lambda bi, j: (bi, j, 0)
    per_b = lambda bi, j: (bi, 0, 0, 0)
    body = functools.partial(_gla_kernel, tt=tt, valid=valid, n_heads=n_heads)
    return pl.pallas_call(
        body,
        grid=(b, t // tt),
        in_specs=[pl.BlockSpec((1, tt, hp), tile)] * 4 + [
            pl.BlockSpec((1, tt, LANES), tile),
            pl.BlockSpec((1, LANES, hp), lay),
            pl.BlockSpec((1, 1, hp), lay),
            pl.BlockSpec((1, 1, hp), lay),
            pl.BlockSpec((1, n_heads, LANES, LANES), per_b),
        ],
        out_specs=[pl.BlockSpec((1, tt, hp), tile), pl.BlockSpec((1, n_heads, LANES, LANES), per_b)],
        out_shape=[jax.ShapeDtypeStruct((b, t, hp), BF16),
                   jax.ShapeDtypeStruct((b, n_heads, LANES, LANES), F32)],
        scratch_shapes=[
            pltpu.VMEM((n_heads, LANES, LANES), F32),
            pltpu.VMEM((tt, hp), F32),
            pltpu.VMEM((tt // GLA_CHUNK, hp), F32),
            pltpu.VMEM((tt, hp), BF16),
            pltpu.VMEM((tt, hp), BF16),
            pltpu.VMEM((hp, tt), BF16),
            pltpu.VMEM((tt, hp), F32),
        ],
        compiler_params=_params("arbitrary", "arbitrary"),
        name="gla",
    )(gq, gk, gv, gr, lr, wa, ba, gg, st0)


def _merge_mlp_kernel(x_ref, ya_ref, ob_ref, oc_ref, gate_ref, gt1_ref, sc2_ref, sh2_ref, gt2_ref,
                      g2_ref, gf_ref, wa_ref, wb_ref, wc_ref, wo_ref, wup_ref, wdn_ref, *outs, final):
    d = x_ref.shape[-1]
    m = (gate_ref[:, 0:d] * _dot(ya_ref[...], wa_ref[0])
         + gate_ref[:, d:2 * d] * _dot(ob_ref[...], wb_ref[0])
         + gate_ref[:, 2 * d:3 * d] * _dot(oc_ref[...], wc_ref[0]))
    x1 = x_ref[...] + gt1_ref[0, 0, 0] * _dot(m.astype(BF16), wo_ref[0])
    h2 = _rms_rows(x1, g2_ref[0]) * (1.0 + sc2_ref[0, 0, 0]) + sh2_ref[0, 0, 0]
    up = jnp.maximum(_dot(h2.astype(BF16), wup_ref[0]), 0.0)
    x2 = x1 + gt2_ref[0, 0, 0] * _dot((up * up).astype(BF16), wdn_ref[0])
    outs[0][...] = x2
    if final:
        outs[1][...] = _rms_rows(x2, gf_ref[...])


def _merge_mlp(x, ya, ob, oc, gates, mods, g2, gf, wa, wb, wc, wo, wup, wdn, layer,
               *, tm, tiles_per_group, final):
    n, d = x.shape
    c = ya.shape[-1]
    dff = wup.shape[-1]
    row = lambda i: (i, 0)
    lay = lambda i: (layer, 0, 0)
    n_out = 2 if final else 1
    return pl.pallas_call(
        functools.partial(_merge_mlp_kernel, final=final),
        grid=(n // tm,),
        in_specs=[pl.BlockSpec((tm, d), row)] + [pl.BlockSpec((tm, c), row)] * 3 + [
            pl.BlockSpec((tm, 3 * d), row),
            _mod_spec(mods, 2, layer, tiles_per_group), _mod_spec(mods, 4, layer, tiles_per_group),
            _mod_spec(mods, 3, layer, tiles_per_group), _mod_spec(mods, 5, layer, tiles_per_group),
            pl.BlockSpec((1, 1, d), lay),
            pl.BlockSpec((1, d), lambda i: (0, 0)),
            _resident((1, c, d), lay), _resident((1, c, d), lay), _resident((1, c, d), lay),
            _resident((1, d, d), lay), _resident((1, d, dff), lay), _resident((1, dff, d), lay),
        ],
        out_specs=[pl.BlockSpec((tm, d), row)] * n_out,
        out_shape=[jax.ShapeDtypeStruct((n, d), F32)] * n_out,
        compiler_params=_params("arbitrary"),
        name="merge_mlp",
    )(x, ya, ob, oc, gates, mods, mods, mods, mods, g2, gf, wa, wb, wc, wo, wup, wdn)


def _t5_bucket_table(n_buckets, max_dist):
    dist = np.arange(max_dist + 1)
    max_exact = n_buckets // 2
    d = np.maximum(dist, 1).astype(np.float32)
    large = max_exact + (np.log(d / max_exact) / math.log(MAX_DISTANCE / max_exact)
                         * (n_buckets - max_exact)).astype(np.int32)
    large = np.minimum(large, n_buckets - 1)
    return np.where(dist < max_exact, dist, large).astype(np.int32)


def _pad_heads(w, n_heads):
    k = w.shape[-1] // n_heads
    w = w.reshape(w.shape[:-1] + (n_heads, k))
    w = jnp.pad(w, [(0, 0)] * (w.ndim - 1) + [(0, LANES - k)])
    return w.reshape(w.shape[:-2] + (n_heads * LANES,))


def kernel(x_prompt, x_sample, c_prompt, c_sample, cache_k, cache_v, page_table, state_conv, state_gla,
           w_ada, b_ada, g_norm1, w_in, w_dw, b_dw, ln_g, ln_b, w_pw2, w_pb, rel_bias, w_a2, b_a, g_gla,
           w_pc, w_o, g_norm2, w_up, w_down, g_final):
    bp, t, d = x_prompt.shape
    bs, ts, _ = x_sample.shape
    depth = w_in.shape[0]
    _, n_pool, page, h_b, hd_b = cache_k.shape
    d_b = h_b * hd_b
    d_conv = w_dw.shape[-1]
    width = w_dw.shape[1]
    rank = w_a2.shape[1]
    h_c = GLA_HEADS
    dk_c = w_a2.shape[-1] // h_c
    dv_c = state_gla.shape[-1]
    hp = h_c * LANES
    n_buckets = rel_bias.shape[0]
    n_pages = page_table.shape[1]
    past_len = n_pages * page
    assert dv_c == LANES and h_c * dv_c == hp and width - 1 <= CONV_HALO

    sizes = (2 * d_conv, d_b, d_b, d_b, h_c * dk_c, h_c * dk_c, h_c * dv_c, h_c * dv_c, rank, d, d, d)
    offs = np.cumsum((0,) + sizes)
    part = [w_in[:, :, offs[i]:offs[i + 1]] for i in range(len(sizes))]
    w_cat = jnp.concatenate([
        part[0], part[1], part[2], part[3],
        _pad_heads(part[4], h_c), _pad_heads(part[5], h_c), part[6], part[7],
        jnp.pad(part[8], ((0, 0), (0, 0), (0, LANES - rank))),
        part[9], part[10], part[11]], axis=-1).astype(BF16)
    wa2 = jnp.pad(_pad_heads(w_a2, h_c), ((0, 0), (0, LANES - rank), (0, 0))).astype(BF16)
    ba2 = _pad_heads(b_a, h_c).reshape(depth, 1, hp)
    gg = g_gla.reshape(depth, 1, hp)
    wpw2, wpb, wpc, wo, wup, wdn = (w.astype(BF16) for w in (w_pw2, w_pb, w_pc, w_o, w_up, w_down))
    g1 = g_norm1.reshape(depth, 1, d)
    g2 = g_norm2.reshape(depth, 1, d)
    gf = g_final.reshape(1, d)
    bdw = b_dw.reshape(depth, 1, d_conv)
    lng = ln_g.reshape(depth, 1, d_conv)
    lnb = ln_b.reshape(depth, 1, d_conv)

    n_c = bp + bs
    rows_c = -(-n_c // 8) * 8
    c_all = jnp.pad(jnp.concatenate([c_prompt, c_sample], axis=0), ((0, rows_c - n_c), (0, 0)))
    mods = _adaln(c_all, w_ada, b_ada).reshape(depth, rows_c, 6, d)

    blk = MOBA_BLOCK
    assert ts <= blk and past_len % blk == 0
    bucket = _t5_bucket_table(n_buckets, 2 * blk - 1)
    onehot = np.zeros((4 * blk, n_buckets), np.float32)
    onehot[np.arange(2 * blk), bucket] = 1.0
    by_dist = jnp.dot(onehot, rel_bias - rel_bias[n_buckets - 1:n_buckets, :],
                      precision=lax.Precision.HIGHEST).T
    toep = jnp.tile(by_dist, (1, blk))[:, :blk * (4 * blk - 1)].reshape(h_b, blk, 4 * blk - 1)
    causal = np.arange(blk)[:, None] <= np.arange(blk)[None, :]
    bias_t = jnp.stack([jnp.where(causal, toep[:, :, :blk] * LOG2E, NEG), toep[:, :, blk:2 * blk] * LOG2E],
                       axis=1)
    rows_s = ts * h_b
    own = jnp.transpose(toep[:, :ts, :ts], (2, 0, 1)).reshape(rows_s, ts)
    own_ok = np.repeat(np.arange(ts), h_b)[:, None] >= np.arange(LANES)[None, :]
    bias_own = jnp.where(own_ok, jnp.pad(own, ((0, 0), (0, LANES - ts))), NEG)
    bias_tail = jnp.transpose(toep[:, :, blk:blk + ts], (2, 0, 1)).reshape(rows_s, blk)
    assert past_len // blk <= LANES
    cache_kt = jnp.transpose(cache_k, (0, 1, 3, 4, 2)).reshape(depth, n_pool, d_b, page)
    cache_vt = jnp.transpose(cache_v, (0, 1, 3, 4, 2)).reshape(depth, n_pool, d_b, page)

    xp = x_prompt.reshape(bp * t, d)
    xs = x_sample.reshape(bs * ts, d)
    n_s = bs * ts
    tm = TOKEN_TILE
    tpb = t // tm
    dims = (d_conv, d_b, hp, hd_b, dk_c)
    ts_pad = 8
    gla_rows_s = GLA_CHUNK

    cp_l, sp_l, cs_l, ss_l = ([] for _ in range(4))
    y_prompt = y_sample = None
    zeros_hist = jnp.zeros((bp, CONV_HALO, d_conv), F32)
    zeros_state = jnp.zeros((bp, h_c, LANES, LANES), F32)
    mods = jnp.transpose(mods, (0, 2, 1, 3))
    mods_p = mods[:, :, :bp].reshape(depth, 6, bp, 1, d)
    mods_s = jnp.repeat(mods[:, :, bp:n_c], ts, axis=2).reshape(depth, 6, 1, n_s, d)
    kv_p = kv_s = None
    for l in range(depth):
        final = l == depth - 1

        (u, qb, kb, vb, k16, vt, km, gq, gk, gv, gr, lr, gates) = _inproj(
            xp, mods_p, g1, w_cat, kv_p, l, tm=tm, tiles_per_group=tpb, dims=dims, attn_aux=True)
        kv_p = (kb, vb)
        ya, tail = _conv_prompt(u.reshape(bp, t, d_conv), zeros_hist, w_dw, bdw, lng, lnb, l, tt=tm)
        ob = _moba_prompt(qb, k16, vt, km, bias_t, batch=bp, n_heads=h_b, hd=hd_b)
        oc, stf = _gla(*(a.reshape(bp, t, -1) for a in (gq, gk, gv, gr, lr)), wa2, ba2, gg, zeros_state, l,
                       tt=tm, valid=tm)
        res = _merge_mlp(xp, ya.reshape(bp * t, d_conv), ob.reshape(bp * t, d_b), oc.reshape(bp * t, hp),
                         gates, mods_p, g2, gf, wpw2, wpb, wpc, wo, wup, wdn, l,
                         tm=tm, tiles_per_group=tpb, final=final)
        xp = res[0]
        if final:
            y_prompt = res[1].reshape(bp, t, d)
        cp_l.append(tail[:, CONV_HALO - (width - 1):])
        sp_l.append(jnp.swapaxes(stf[..., :dk_c], -1, -2))

        (u, qb, kb, vb, gq, gk, gv, gr, lr, gates) = _inproj(
            xs, mods_s, g1, w_cat, kv_s, l, tm=n_s, tiles_per_group=1, dims=dims, attn_aux=False)
        kv_s = (kb, vb)
        kb, vb = kb[l], vb[l]
        ext = jnp.concatenate([state_conv[l], u.reshape(bs, ts, d_conv)], axis=1)
        ext_p = jnp.pad(ext, ((0, 0), (0, ts_pad + width - 1 - ext.shape[1] + (-(ts_pad + width - 1)) % 8), (0, 0)))
        ya = _conv_sample(ext_p, w_dw, bdw, lng, lnb, l, rows=ts_pad)[:, :ts]
        q_rep = jnp.repeat(qb.reshape(bs, ts, d_b), h_b, axis=1)
        pad_new = lambda a: jnp.pad(a.reshape(bs, ts, d_b), ((0, 0), (0, ts_pad - ts), (0, 0)))
        ob = _moba_sample(page_table, q_rep, pad_new(kb), pad_new(vb), bias_own, bias_tail,
                          cache_kt, cache_vt, l, n_heads=h_b, t_new=ts)
        pad_t = lambda a: jnp.pad(a.reshape(bs, ts, -1), ((0, 0), (0, gla_rows_s - ts), (0, 0)))
        st0 = jnp.pad(jnp.swapaxes(state_gla[l], -1, -2), ((0, 0), (0, 0), (0, 0), (0, LANES - dk_c)))
        oc, stf = _gla(pad_t(gq), pad_t(gk), pad_t(gv), pad_t(gr), pad_t(lr), wa2, ba2, gg, st0, l,
                       tt=gla_rows_s, valid=ts)
        res = _merge_mlp(xs, ya.reshape(n_s, d_conv), ob.reshape(n_s, d_b), oc[:, :ts].reshape(n_s, hp),
                         gates, mods_s, g2, gf, wpw2, wpb, wpc, wo, wup, wdn, l,
                         tm=n_s, tiles_per_group=1, final=final)
        xs = res[0]
        if final:
            y_sample = res[1].reshape(bs, ts, d)
        cs_l.append(ext[:, -(width - 1):])
        ss_l.append(jnp.swapaxes(stf[..., :dk_c], -1, -2))

    rows_p = lambda a: a.reshape(depth, bp, t, h_b, hd_b)
    rows_s_ = lambda a: a.reshape(depth, bs, ts, h_b, hd_b)
    return (y_prompt, y_sample, rows_p(kv_p[0]), rows_p(kv_p[1]), jnp.stack(cp_l), jnp.stack(sp_l),
            rows_s_(kv_s[0]), rows_s_(kv_s[1]), jnp.stack(cs_l), jnp.stack(ss_l))
```

```python
import functools
import math

import numpy as np
import jax
import jax.numpy as jnp
from jax import lax
from jax.experimental import pallas as pl
from jax.experimental.pallas import tpu as pltpu

F32 = jnp.float32
BF16 = jnp.bfloat16

MOBA_BLOCK = 256
MOBA_TOPK = 3
MAX_DISTANCE = 128
GLA_HEADS = 4
GLA_CHUNK = 16
GLA_UNROLL = 4
GATE_NORM = 16.0
EPS = 1e-6

LANES = 128
SUBLANES = 8
VMEM_LIMIT = 56 * 1024 * 1024

NEG = -1e30
LOG2E = math.log2(math.e)
V_ROW_PAD = 16
TOKEN_TILE = 256
PAGES_PER_CHUNK = 16
CONV_HALO = 32

_NT = (((1,), (1,)), ((), ()))


def _dot(a, b):
    return jnp.dot(a, b, preferred_element_type=F32)


def _dot_nt(a, b):
    return lax.dot_general(a, b, _NT, preferred_element_type=F32)


def _dot_f32(a, b):
    return jnp.dot(a, b, precision=lax.Precision.HIGHEST, preferred_element_type=F32)


def _dot_nt_f32(a, b):
    return lax.dot_general(a, b, _NT, precision=lax.Precision.HIGHEST, preferred_element_type=F32)


def _params(*sem):
    return pltpu.CompilerParams(dimension_semantics=sem, vmem_limit_bytes=VMEM_LIMIT)


def _resident(shape, index_map):
    return pl.BlockSpec(shape, index_map, pipeline_mode=pl.Buffered(1))


def _rms_rows(x, g):
    return (x * lax.rsqrt(jnp.mean(x * x, axis=-1, keepdims=True) + EPS)) * g


def _log_sigmoid(x):
    return jnp.minimum(x, 0.0) - jnp.log1p(jnp.exp(-jnp.abs(x)))


def _top_k_mask(s, idx, axis, k, sentinel):
    sel = jnp.zeros(s.shape, dtype=jnp.bool_)
    for _ in range(k):
        mx = jnp.max(s, axis=axis, keepdims=True)
        first = jnp.min(jnp.where(s == mx, idx, sentinel), axis=axis, keepdims=True)
        pick = (idx == first) & (mx > -jnp.inf)
        sel = sel | pick
        s = jnp.where(pick, -jnp.inf, s)
    return sel


def _adaln_kernel(c_ref, w_ref, b_ref, o_ref):
    c = c_ref[...]
    s = c * jax.nn.sigmoid(c)
    o_ref[0] = _dot(s.astype(BF16), w_ref[0].astype(BF16)) + b_ref[0]


def _adaln(c_all, w_ada, b_ada):
    depth, d, n = w_ada.shape
    rows = c_all.shape[0]
    tn = 1536
    return pl.pallas_call(
        _adaln_kernel,
        grid=(depth, n // tn),
        in_specs=[
            pl.BlockSpec((rows, d), lambda l, j: (0, 0)),
            pl.BlockSpec((1, d, tn), lambda l, j: (l, 0, j)),
            pl.BlockSpec((1, 1, tn), lambda l, j: (l, 0, j)),
        ],
        out_specs=pl.BlockSpec((1, rows, tn), lambda l, j: (l, 0, j)),
        out_shape=jax.ShapeDtypeStruct((depth, rows, n), F32),
        compiler_params=_params("arbitrary", "arbitrary"),
        name="adaln",
    )(c_all, w_ada, b_ada.reshape(depth, 1, n))


def _inproj_kernel(x_ref, sc_ref, sh_ref, g_ref, w_ref, *rest, d_conv, d_b, hp, d_model,
                   q_scale, gq_scale, n_heads_b, attn_aux, n_alias, kv_layer):
    outs = rest[n_alias:]
    if attn_aux:
        (u_ref, q_ref, k_ref, v_ref, k16_ref, vt_ref, km_ref,
         gq_ref, gk_ref, gv_ref, gr_ref, lr_ref, gate_ref) = outs
    else:
        (u_ref, q_ref, k_ref, v_ref, gq_ref, gk_ref, gv_ref, gr_ref, lr_ref, gate_ref) = outs
    x = x_ref[...]
    h = _rms_rows(x, g_ref[0]) * (1.0 + sc_ref[0, 0, 0]) + sh_ref[0, 0, 0]
    hb = h.astype(BF16)

    pos = [0]

    def seg(width):
        lo = pos[0]
        pos[0] = lo + width
        return _dot(hb, w_ref[0, :, lo:lo + width])

    a = seg(d_conv)
    g = seg(d_conv)
    u_ref[...] = a * jax.nn.sigmoid(g)
    q_ref[...] = seg(d_b) * q_scale
    k = seg(d_b)
    v = seg(d_b)
    def put(ref, val):
        if n_alias:
            ref[0] = val
        else:
            for l in range(ref.shape[0]):
                ref[l] = val if l == kv_layer else jnp.zeros_like(val)

    if not attn_aux:
        put(k_ref, k)
        put(v_ref, v)
    else:
        v_t = v.T
        put(k_ref, k.T[None])
        put(v_ref, v_t[None])
        k16_ref[0] = k.astype(BF16)
        vt = v_t.astype(BF16)
        hd = d_b // n_heads_b
        tail = jnp.where(lax.broadcasted_iota(jnp.int32, (V_ROW_PAD, vt.shape[1]), 0) == 0, 1.0, 0.0).astype(BF16)
        for h in range(n_heads_b):
            vt_ref[0, h * (hd + V_ROW_PAD):h * (hd + V_ROW_PAD) + hd, :] = vt[h * hd:(h + 1) * hd, :]
            vt_ref[0, h * (hd + V_ROW_PAD) + hd:(h + 1) * (hd + V_ROW_PAD), :] = tail
        km_ref[0] = jnp.mean(k, axis=0, keepdims=True)
    gq_ref[...] = seg(hp) * gq_scale
    gk_ref[...] = seg(hp)
    gv_ref[...] = seg(hp)
    gr_ref[...] = seg(hp)
    lr_ref[...] = seg(LANES)
    for i in range(3):
        gate_ref[:, i * d_model:(i + 1) * d_model] = jax.nn.sigmoid(seg(d_model))


def _mod_spec(mods, idx, layer, tiles_per_group):
    return pl.BlockSpec((1, 1, 1) + mods.shape[3:], lambda i: (layer, idx, i // tiles_per_group, 0, 0))


def _inproj(x, mods, g1, w_cat, kv_bufs, layer, *, tm, tiles_per_group, dims, attn_aux):
    n, d = x.shape
    d_conv, d_b, hp, hd_b, dk_c = dims
    depth, _, nw = w_cat.shape
    grid = (n // tm,)
    row = lambda i: (i, 0)
    lay = lambda i: (layer, 0, 0)
    lead, l0 = (depth, 0) if kv_bufs is None else (1, layer)
    if attn_aux:
        tpg = tiles_per_group
        kv_shape = (depth, n // (tpg * tm), d_b, tpg * tm)
        kv_spec = pl.BlockSpec((lead, 1, d_b, tm), lambda i: (l0, i // tpg, 0, i % tpg))
    else:
        kv_shape, kv_spec = (depth, n, d_b), pl.BlockSpec((lead, tm, d_b), lambda i: (l0, i, 0))
    outs = [((n, d_conv), F32), ((n, d_b), F32), (kv_shape, F32), (kv_shape, F32)]
    specs = [pl.BlockSpec((tm, d_conv), row), pl.BlockSpec((tm, d_b), row), kv_spec, kv_spec]
    if attn_aux:
        assert tm == MOBA_BLOCK
        nblk = n // MOBA_BLOCK
        vt_rows = d_b + (d_b // hd_b) * V_ROW_PAD
        outs += [((nblk, MOBA_BLOCK, d_b), BF16), ((nblk, vt_rows, MOBA_BLOCK), BF16), ((nblk, 1, d_b), F32)]
        specs += [pl.BlockSpec((1, MOBA_BLOCK, d_b), lambda i: (i, 0, 0)),
                  pl.BlockSpec((1, vt_rows, MOBA_BLOCK), lambda i: (i, 0, 0)),
                  pl.BlockSpec((1, 1, d_b), lambda i: (i, 0, 0))]
    outs += [((n, hp), F32)] * 4 + [((n, LANES), F32), ((n, 3 * d), F32)]
    specs += [pl.BlockSpec((tm, hp), row)] * 4 + [pl.BlockSpec((tm, LANES), row), pl.BlockSpec((tm, 3 * d), row)]
    body = functools.partial(_inproj_kernel, d_conv=d_conv, d_b=d_b, hp=hp, d_model=d,
                             q_scale=hd_b ** -0.5, gq_scale=dk_c ** -0.5, n_heads_b=d_b // hd_b,
                             attn_aux=attn_aux, n_alias=0 if kv_bufs is None else 2, kv_layer=layer)
    in_specs = [
        pl.BlockSpec((tm, d), row),
        _mod_spec(mods, 1, layer, tiles_per_group),
        _mod_spec(mods, 0, layer, tiles_per_group),
        pl.BlockSpec((1, 1, d), lay),
        _resident((1, d, nw), lay),
    ]
    args = [x, mods, mods, g1, w_cat]
    aliases = {}
    if kv_bufs is not None:
        in_specs += [pl.BlockSpec(memory_space=pl.ANY)] * 2
        aliases = {len(args): 2, len(args) + 1: 3}
        args += list(kv_bufs)
    return pl.pallas_call(
        body,
        grid=grid,
        in_specs=in_specs,
        out_specs=specs,
        out_shape=[jax.ShapeDtypeStruct(s, t) for s, t in outs],
        input_output_aliases=aliases,
        compiler_params=_params("arbitrary"),
        name="inproj",
    )(*args)


def _conv_taps(ext_ref, start, rows, wdw_ref, bdw_ref, lng_ref, lnb_ref, width, shift_scr=None):
    acc = jnp.zeros((rows, wdw_ref.shape[-1]), F32) + bdw_ref[0]
    if shift_scr is None:
        for k in range(width):
            acc = acc + ext_ref[pl.ds(start + k, rows), :] * wdw_ref[0, k:k + 1, :]
    else:
        span = shift_scr.shape[1]
        for r in range(1, SUBLANES):
            shift_scr[r - 1] = ext_ref[pl.ds(r, span), :]
        for k in range(width):
            a, r = divmod(start + k, SUBLANES)
            src = ext_ref if r == 0 else shift_scr.at[r - 1]
            acc = acc + src[pl.ds(a * SUBLANES, rows), :] * wdw_ref[0, k:k + 1, :]
    mu = jnp.mean(acc, axis=-1, keepdims=True)
    cen = acc - mu
    var = jnp.mean(cen * cen, axis=-1, keepdims=True)
    y = (cen * lax.rsqrt(var + EPS)) * lng_ref[0] + lnb_ref[0]
    return y * jax.nn.sigmoid(y)


def _conv_kernel(u_ref, hist_ref, wdw_ref, bdw_ref, lng_ref, lnb_ref, y_ref, tail_ref, ext_scr, shift_scr,
                 *, tt, width):
    j = pl.program_id(1)

    @pl.when(j == 0)
    def _():
        ext_scr[0:CONV_HALO, :] = hist_ref[0]

    @pl.when(j > 0)
    def _():
        ext_scr[0:CONV_HALO, :] = ext_scr[tt:tt + CONV_HALO, :]

    ext_scr[CONV_HALO:CONV_HALO + tt, :] = u_ref[0]
    start = CONV_HALO - (width - 1)
    y_ref[0] = _conv_taps(ext_scr, start, tt, wdw_ref, bdw_ref, lng_ref, lnb_ref, width, shift_scr).astype(BF16)

    @pl.when(j == pl.num_programs(1) - 1)
    def _():
        tail_ref[0] = ext_scr[tt:tt + CONV_HALO, :]


def _conv_prompt(u, hist, w_dw, b_dw, ln_g, ln_b, layer, *, tt):
    b, t, c = u.shape
    width = w_dw.shape[1]
    lay = lambda bi, j: (layer, 0, 0)
    return pl.pallas_call(
        functools.partial(_conv_kernel, tt=tt, width=width),
        grid=(b, t // tt),
        in_specs=[
            pl.BlockSpec((1, tt, c), lambda bi, j: (bi, j, 0)),
            pl.BlockSpec((1, CONV_HALO, c), lambda bi, j: (bi, 0, 0)),
            pl.BlockSpec((1, width, c), lay),
            pl.BlockSpec((1, 1, c), lay),
            pl.BlockSpec((1, 1, c), lay),
            pl.BlockSpec((1, 1, c), lay),
        ],
        out_specs=[pl.BlockSpec((1, tt, c), lambda bi, j: (bi, j, 0)),
                   pl.BlockSpec((1, CONV_HALO, c), lambda bi, j: (bi, 0, 0))],
        out_shape=[jax.ShapeDtypeStruct((b, t, c), BF16), jax.ShapeDtypeStruct((b, CONV_HALO, c), F32)],
        scratch_shapes=[pltpu.VMEM((CONV_HALO + tt, c), F32),
                        pltpu.VMEM((SUBLANES - 1, CONV_HALO + tt - SUBLANES, c), F32)],
        compiler_params=_params("arbitrary", "arbitrary"),
        name="conv_prompt",
    )(u, hist, w_dw, b_dw, ln_g, ln_b)


def _conv_small_kernel(ext_ref, wdw_ref, bdw_ref, lng_ref, lnb_ref, y_ref, *, rows, width):
    y_ref[0] = _conv_taps(ext_ref.at[0], 0, rows, wdw_ref, bdw_ref, lng_ref, lnb_ref, width).astype(BF16)


def _conv_sample(ext, w_dw, b_dw, ln_g, ln_b, layer, *, rows):
    b, r, c = ext.shape
    width = w_dw.shape[1]
    lay = lambda bi: (layer, 0, 0)
    return pl.pallas_call(
        functools.partial(_conv_small_kernel, rows=rows, width=width),
        grid=(b,),
        in_specs=[
            pl.BlockSpec((1, r, c), lambda bi: (bi, 0, 0)),
            pl.BlockSpec((1, width, c), lay),
            pl.BlockSpec((1, 1, c), lay),
            pl.BlockSpec((1, 1, c), lay),
            pl.BlockSpec((1, 1, c), lay),
        ],
        out_specs=pl.BlockSpec((1, rows, c), lambda bi: (bi, 0, 0)),
        out_shape=jax.ShapeDtypeStruct((b, rows, c), BF16),
        compiler_params=_params("arbitrary"),
        name="conv_sample",
    )(ext, w_dw, b_dw, ln_g, ln_b)


def _moba_prompt_kernel(q_ref, k16_ref, vt_ref, km_ref, bias_ref, o_ref,
                        qm_scr, sel_scr, m_scr, acc_scr, s2_scr, *, nb, n_heads, hd):
    i = pl.program_id(1)
    blk = MOBA_BLOCK
    q = q_ref[0]
    km = km_ref[0]
    lane = lax.broadcasted_iota(jnp.int32, (blk, LANES), 1)
    blk_idx = lax.broadcasted_iota(jnp.int32, (nb, blk), 0)
    heads_per_slab = LANES // hd

    km16 = km.astype(BF16)
    for h in range(n_heads):
        p, w = divmod(h, heads_per_slab)
        qm = jnp.where((lane // hd) == w, q[:, p * LANES:(p + 1) * LANES], 0.0)
        qm_scr[h] = (qm * LOG2E).astype(BF16)
        s = _dot_nt(km16[:, p * LANES:(p + 1) * LANES], qm.astype(BF16))
        s = jnp.where(blk_idx < i, s, -jnp.inf)
        sel = _top_k_mask(s, blk_idx, 0, MOBA_TOPK, nb)
        sel_scr[h] = jnp.where(sel, 1.0, 0.0)

    hv = vt_ref.shape[1] // n_heads

    def block(n, kind):
        kb = k16_ref[n]
        vb = vt_ref[n]

        for h in range(n_heads):
            p = h // heads_per_slab
            s2_scr[h, 0:blk, :] = _dot_nt(kb[:, p * LANES:(p + 1) * LANES], qm_scr[h])
        for h in range(n_heads):
            s = s2_scr[h, 0:blk, :]
            rows = slice(h * hv, (h + 1) * hv)
            if kind == "own":
                s = s + bias_ref[h, 0]
                m = jnp.max(s, axis=0, keepdims=True)
                m_scr[h:h + 1, :] = m
                acc_scr[rows, :] = _dot(vb[rows, :], jnp.exp2(s - m).astype(BF16))
            else:
                if kind == "prev":
                    s = s + bias_ref[h, 1]
                picked = sel_scr[h, pl.ds(n, 1), :] > 0.5
                m_old = m_scr[h:h + 1, :]
                m = jnp.where(picked, jnp.maximum(m_old, jnp.max(s, axis=0, keepdims=True)), m_old)
                alpha = jnp.exp2(m_old - m)
                pr = jnp.exp2(s - jnp.where(picked, m, -NEG))
                m_scr[h:h + 1, :] = m
                acc_scr[rows, :] = alpha * acc_scr[rows, :] + _dot(vb[rows, :], pr.astype(BF16))

    block(i, "own")

    @pl.when(i >= 1)
    def _():
        block(i - 1, "prev")

    def far_pair(n):
        kb = k16_ref[pl.ds(n, 2)].reshape(2 * blk, k16_ref.shape[-1])
        vb = jnp.concatenate([vt_ref[n], vt_ref[n + 1]], axis=1)
        for h in range(n_heads):
            p = h // heads_per_slab
            s2_scr[h] = _dot_nt(kb[:, p * LANES:(p + 1) * LANES], qm_scr[h])
        for h in range(n_heads):
            rows = slice(h * hv, (h + 1) * hv)
            s_a = s2_scr[h, 0:blk, :]
            s_b = s2_scr[h, blk:2 * blk, :]
            pick_a = sel_scr[h, pl.ds(n, 1), :] > 0.5
            pick_b = sel_scr[h, pl.ds(n + 1, 1), :] > 0.5
            m_old = m_scr[h:h + 1, :]
            m = jnp.maximum(m_old, jnp.maximum(
                jnp.where(pick_a, jnp.max(s_a, axis=0, keepdims=True), NEG),
                jnp.where(pick_b, jnp.max(s_b, axis=0, keepdims=True), NEG)))
            alpha = jnp.exp2(m_old - m)
            pr = jnp.concatenate([jnp.exp2(s_a - jnp.where(pick_a, m, -NEG)).astype(BF16),
                                  jnp.exp2(s_b - jnp.where(pick_b, m, -NEG)).astype(BF16)], axis=0)
            m_scr[h:h + 1, :] = m
            acc_scr[rows, :] = alpha * acc_scr[rows, :] + _dot(vb[rows, :], pr)

    n_far = jnp.maximum(i - 1, 0)

    def far(t, carry):
        far_pair(2 * t)
        return carry

    lax.fori_loop(0, n_far // 2, far, 0)

    @pl.when(n_far % 2 == 1)
    def _():
        block(n_far - 1, "far")

    out = [acc_scr[h * hv:h * hv + hd, :] / acc_scr[h * hv + hd:h * hv + hd + 1, :] for h in range(n_heads)]
    o_ref[0] = jnp.concatenate(out, axis=0).T.astype(BF16)


def _moba_prompt(q, k16, vt, km, bias_t, *, batch, n_heads, hd):
    n, d_b = q.shape
    t = n // batch
    nb = t // MOBA_BLOCK
    blk = MOBA_BLOCK
    body = functools.partial(_moba_prompt_kernel, nb=nb, n_heads=n_heads, hd=hd)
    return pl.pallas_call(
        body,
        grid=(batch, nb),
        in_specs=[
            pl.BlockSpec((1, blk, d_b), lambda b, i: (b * nb + i, 0, 0)),
            pl.BlockSpec((nb, blk, d_b), lambda b, i: (b, 0, 0)),
            pl.BlockSpec((nb, vt.shape[1], blk), lambda b, i: (b, 0, 0)),
            pl.BlockSpec((1, nb, d_b), lambda b, i: (b, 0, 0)),
            _resident((n_heads, 2, blk, blk), lambda b, i: (0, 0, 0, 0)),
        ],
        out_specs=pl.BlockSpec((1, blk, d_b), lambda b, i: (b * nb + i, 0, 0)),
        out_shape=jax.ShapeDtypeStruct((n // blk, blk, d_b), BF16),
        scratch_shapes=[
            pltpu.VMEM((n_heads, blk, LANES), BF16),
            pltpu.VMEM((n_heads, nb, blk), F32),
            pltpu.VMEM((n_heads, blk), F32),
            pltpu.VMEM((vt.shape[1], blk), F32),
            pltpu.VMEM((n_heads, 2 * blk, blk), F32),
        ],
        compiler_params=_params("arbitrary", "arbitrary"),
        name="moba_prompt",
    )(q.reshape(n // blk, blk, d_b), k16, vt, km.reshape(batch, nb, d_b), bias_t)


def _moba_sample_kernel(pt_ref, q_ref, kn_ref, vn_ref, bown_ref, btail_ref, kt_hbm, vt_hbm, o_ref,
                        s_scr, kv_scr, kbuf, vbuf, ksem, vsem,
                        *, layer, n_chunks, n_heads, t_new, page, ppc):
    b = pl.program_id(0)
    rows, d_b = q_ref.shape[1:]
    hd = d_b // n_heads
    blocks_per_chunk = ppc * page // MOBA_BLOCK
    pages_per_block = MOBA_BLOCK // page
    nbp = n_chunks * blocks_per_chunk

    def page_copy(hbm, buf, sem, seq, chunk, p):
        slot = chunk % 2
        return pltpu.make_async_copy(hbm.at[layer, pt_ref[seq, chunk * ppc + p]], buf.at[slot, p], sem.at[slot])

    def start_chunk(hbm, buf, sem, seq, chunk):
        for p in range(ppc):
            page_copy(hbm, buf, sem, seq, chunk, p).start()

    def wait_chunk(hbm, buf, sem, seq, chunk):
        for p in range(ppc):
            page_copy(hbm, buf, sem, seq, chunk, p).wait()

    @pl.when(b == 0)
    def _():
        start_chunk(kt_hbm, kbuf, ksem, b, 0)

    row_i = lax.broadcasted_iota(jnp.int32, (rows, d_b), 0)
    lane_i = lax.broadcasted_iota(jnp.int32, (rows, d_b), 1)
    head_mask = (lane_i // hd) == (row_i % n_heads)
    blk_lane = lax.broadcasted_iota(jnp.int32, (rows, LANES), 1)
    qb16 = jnp.where(head_mask, q_ref[0], 0.0).astype(BF16)

    bsum = jnp.zeros((rows, LANES), F32)
    for c in range(n_chunks):
        if c + 1 < n_chunks:
            start_chunk(kt_hbm, kbuf, ksem, b, c + 1)
        else:
            start_chunk(vt_hbm, vbuf, vsem, b, 0)
        wait_chunk(kt_hbm, kbuf, ksem, b, c)
        for p in range(ppc):
            s = _dot(qb16, kbuf[c % 2, p].astype(BF16))
            s_scr[c, :, p * page:(p + 1) * page] = s
            ps = jnp.sum(s, axis=-1, keepdims=True)
            psum = ps if p % pages_per_block == 0 else psum + ps
            if p % pages_per_block == pages_per_block - 1:
                bsum = jnp.where(blk_lane == c * blocks_per_chunk + p // pages_per_block, psum, bsum)

    bs = jnp.where(blk_lane < nbp, bsum, -jnp.inf)
    sel = _top_k_mask(bs, blk_lane, 1, MOBA_TOPK, LANES)
    blk_bias = jnp.where(sel, 0.0, NEG)

    kv_scr[...] = jnp.zeros(kv_scr.shape, F32)
    kv_scr[0:kn_ref.shape[1], :] = kn_ref[0]
    s_own = _dot_nt(qb16, kv_scr[...].astype(BF16)) + bown_ref[...]
    m = jnp.max(s_own, axis=-1, keepdims=True)
    for blk in range(nbp):
        cc, off = divmod(blk, blocks_per_chunk)
        cols = slice(off * MOBA_BLOCK, (off + 1) * MOBA_BLOCK)
        s = s_scr[cc, :, cols] + blk_bias[:, blk:blk + 1]
        if blk == nbp - 1:
            s = s + btail_ref[...]
        s_scr[cc, :, cols] = s
        m = jnp.maximum(m, jnp.max(s, axis=-1, keepdims=True))

    p_own = jnp.exp(s_own - m)
    l = jnp.sum(p_own, axis=-1, keepdims=True)
    for cc in range(n_chunks):
        pr = jnp.exp(s_scr[cc] - m)
        l = l + jnp.sum(pr, axis=-1, keepdims=True)
        s_scr[cc] = pr
    kv_scr[0:vn_ref.shape[1], :] = vn_ref[0]
    acc = _dot(p_own.astype(BF16), kv_scr[...].astype(BF16))

    for c in range(n_chunks):
        if c + 1 < n_chunks:
            start_chunk(vt_hbm, vbuf, vsem, b, c + 1)
        else:
            @pl.when(b + 1 < pl.num_programs(0))
            def _():
                start_chunk(kt_hbm, kbuf, ksem, b + 1, 0)
        wait_chunk(vt_hbm, vbuf, vsem, b, c)
        for p in range(ppc):
            pr = s_scr[c, :, p * page:(p + 1) * page]
            acc = acc + _dot_nt(pr.astype(BF16), vbuf[c % 2, p].astype(BF16))

    o = jnp.where(head_mask, acc / l, 0.0)
    o_ref[0] = jnp.sum(o.reshape(t_new, n_heads, d_b), axis=1).astype(BF16)


def _moba_sample(page_table, q_rep, k_new, v_new, bias_own, bias_tail, cache_kt, cache_vt, layer,
                 *, n_heads, t_new):
    b, rows, d_b = q_rep.shape
    page = cache_kt.shape[-1]
    n_pages = page_table.shape[1]
    ppc = min(PAGES_PER_CHUNK, n_pages)
    n_chunks = n_pages // ppc
    assert n_pages % ppc == 0 and (ppc * page) % MOBA_BLOCK == 0 and MOBA_BLOCK % page == 0
    chunk = ppc * page

    per_b = lambda bi, pt: (bi, 0, 0)
    const2 = lambda bi, pt: (0, 0)
    in_specs = [
        pl.BlockSpec((1, rows, d_b), per_b),
        pl.BlockSpec((1, k_new.shape[1], d_b), per_b),
        pl.BlockSpec((1, v_new.shape[1], d_b), per_b),
        pl.BlockSpec(bias_own.shape, const2),
        pl.BlockSpec(bias_tail.shape, const2),
        pl.BlockSpec(memory_space=pl.ANY),
        pl.BlockSpec(memory_space=pl.ANY),
    ]
    body = functools.partial(_moba_sample_kernel, layer=layer, n_chunks=n_chunks, n_heads=n_heads,
                             t_new=t_new, page=page, ppc=ppc)
    grid_spec = pltpu.PrefetchScalarGridSpec(
        num_scalar_prefetch=1,
        grid=(b,),
        in_specs=in_specs,
        out_specs=pl.BlockSpec((1, t_new, d_b), per_b),
        scratch_shapes=[
            pltpu.VMEM((n_chunks, rows, chunk), F32),
            pltpu.VMEM((LANES, d_b), F32),
            pltpu.VMEM((2, ppc, d_b, page), F32),
            pltpu.VMEM((2, ppc, d_b, page), F32),
            pltpu.SemaphoreType.DMA((2,)),
            pltpu.SemaphoreType.DMA((2,)),
        ],
    )
    return pl.pallas_call(
        body,
        grid_spec=grid_spec,
        out_shape=jax.ShapeDtypeStruct((b, t_new, d_b), BF16),
        compiler_params=_params("arbitrary"),
        name="moba_sample",
    )(page_table, q_rep, k_new, v_new, bias_own, bias_tail, cache_kt, cache_vt)


def _gla_kernel(gq_ref, gk_ref, gv_ref, gr_ref, lr_ref, wa_ref, ba_ref, gg_ref, st0_ref,
                o_ref, stf_ref, st_scr, bc_scr, bl_scr, qd_scr, kd_scr, vt_scr, o_scr,
                *, tt, valid, n_heads):
    j = pl.program_id(1)
    c_sz = GLA_CHUNK
    hp = n_heads * LANES

    @pl.when(j == 0)
    def _():
        st_scr[...] = st0_ref[0]

    x = _dot(lr_ref[0].astype(BF16), wa_ref[0]) + ba_ref[0]
    la = _log_sigmoid(x) * (1.0 / GATE_NORM)
    if valid < tt:
        la = jnp.where(lax.broadcasted_iota(jnp.int32, (tt, hp), 0) < valid, la, 0.0)
    r_i = lax.broadcasted_iota(jnp.int32, (tt, tt), 0)
    c_i = lax.broadcasted_iota(jnp.int32, (tt, tt), 1)
    low = jnp.where(((r_i // c_sz) == (c_i // c_sz)) & (c_i <= r_i), 1.0, 0.0).astype(BF16)
    la_hi = la.astype(BF16)
    la_lo = (la - la_hi.astype(F32)).astype(BF16)
    bc = _dot(low, la_hi) + _dot(low, la_lo)
    bc_scr[...] = bc
    n_ch = tt // c_sz
    in_ch = (lax.broadcasted_iota(jnp.int32, (n_ch, tt), 1) // c_sz
             == lax.broadcasted_iota(jnp.int32, (n_ch, tt), 0))
    tot = jnp.where(in_ch, 1.0, 0.0).astype(BF16)
    ends = _dot(tot, la_hi) + _dot(tot, la_lo)
    bl_scr[...] = ends
    bl = jnp.broadcast_to(ends[:, None, :], (n_ch, c_sz, hp)).reshape(tt, hp)
    qd_scr[...] = (gq_ref[0] * jnp.exp(bc)).astype(BF16)
    kd_scr[...] = (gk_ref[0] * jnp.exp(bl - bc)).astype(BF16)
    vt_scr[...] = gv_ref[0].T.astype(BF16)

    half = c_sz // 2
    t_half = lax.broadcasted_iota(jnp.int32, (half, LANES), 0)
    chunk_of_col = lax.broadcasted_iota(jnp.int32, (LANES, tt), 1) // c_sz

    def chunk(c, carry):
        rows = pl.ds(pl.multiple_of(c * c_sz, c_sz), c_sz)
        bc_c = bc_scr[rows, :]
        q_c = gq_ref[0, rows, :]
        k_c = gk_ref[0, rows, :]
        v_c = gv_ref[0, rows, :]
        decay = jnp.exp(bl_scr[pl.ds(c, 1), :])
        in_chunk = chunk_of_col == c
        for h in range(n_heads):
            hs = slice(h * LANES, (h + 1) * LANES)
            st = st_scr[h]
            o_h = _dot_nt(qd_scr[rows, hs], st.astype(BF16))
            bch = bc_c[:, hs]
            qh = q_c[:, hs]
            parts = [o_h[0:half], o_h[half:c_sz]]
            for s in range(c_sz):
                for hi in range(2):
                    r0 = hi * half
                    if s >= r0 + half:
                        continue
                    diff = bch[r0:r0 + half] - bch[s:s + 1, :]
                    if s > r0:
                        diff = jnp.where(t_half + r0 >= s, diff, -jnp.inf)
                    a_s = jnp.sum(qh[r0:r0 + half] * k_c[s:s + 1, hs] * jnp.exp(diff), axis=-1, keepdims=True)
                    parts[hi] = parts[hi] + a_s * v_c[s:s + 1, hs]
            o_scr[rows, hs] = jnp.concatenate(parts, axis=0)
            vtm = jnp.where(in_chunk, vt_scr[hs, :], jnp.zeros((), BF16))
            st_scr[h] = decay[:, hs] * st + _dot(vtm, kd_scr[:, hs])
        return carry

    lax.fori_loop(0, (valid + c_sz - 1) // c_sz, chunk, 0, unroll=GLA_UNROLL)

    o = o_scr[...]
    r = gr_ref[0]
    for h in range(n_heads):
        hs = slice(h * LANES, (h + 1) * LANES)
        oh = o[:, hs]
        oh = oh * lax.rsqrt(jnp.mean(oh * oh, axis=-1, keepdims=True) + EPS)
        rh = r[:, hs]
        o_ref[0, :, hs] = ((oh * gg_ref[0, :, hs]) * (rh * jax.nn.sigmoid(rh))).astype(BF16)

    @pl.when(j == pl.num_programs(1) - 1)
    def _():
        stf_ref[0] = st_scr[...]


def _gla(gq, gk, gv, gr, lr, wa, ba, gg, st0, layer, *, tt, valid):
    b, t, hp = gq.shape
    n_heads = hp // LANES
    lay = lambda bi, j: (layer, 0, 0)
    tile = lambda bi, j: (bi, j, 0)
    per_b = lambda bi, j: (bi, 0, 0, 0)
    body = functools.partial(_gla_kernel, tt=tt, valid=valid, n_heads=n_heads)
    return pl.pallas_call(
        body,
        grid=(b, t // tt),
        in_specs=[pl.BlockSpec((1, tt, hp), tile)] * 4 + [
            pl.BlockSpec((1, tt, LANES), tile),
            pl.BlockSpec((1, LANES, hp), lay),
            pl.BlockSpec((1, 1, hp), lay),
            pl.BlockSpec((1, 1, hp), lay),
            pl.BlockSpec((1, n_heads, LANES, LANES), per_b),
        ],
        out_specs=[pl.BlockSpec((1, tt, hp), tile), pl.BlockSpec((1, n_heads, LANES, LANES), per_b)],
        out_shape=[jax.ShapeDtypeStruct((b, t, hp), BF16),
                   jax.ShapeDtypeStruct((b, n_heads, LANES, LANES), F32)],
        scratch_shapes=[
            pltpu.VMEM((n_heads, LANES, LANES), F32),
            pltpu.VMEM((tt, hp), F32),
            pltpu.VMEM((tt // GLA_CHUNK, hp), F32),
            pltpu.VMEM((tt, hp), BF16),
            pltpu.VMEM((tt, hp), BF16),
            pltpu.VMEM((hp, tt), BF16),
            pltpu.VMEM((tt, hp), F32),
        ],
        compiler_params=_params("arbitrary", "arbitrary"),
        name="gla",
    )(gq, gk, gv, gr, lr, wa, ba, gg, st0)


def _merge_mlp_kernel(x_ref, ya_ref, ob_ref, oc_ref, gate_ref, gt1_ref, sc2_ref, sh2_ref, gt2_ref,
                      g2_ref, gf_ref, wa_ref, wb_ref, wc_ref, wo_ref, wup_ref, wdn_ref, *outs, final):
    d = x_ref.shape[-1]
    m = (gate_ref[:, 0:d] * _dot(ya_ref[...], wa_ref[0])
         + gate_ref[:, d:2 * d] * _dot(ob_ref[...], wb_ref[0])
         + gate_ref[:, 2 * d:3 * d] * _dot(oc_ref[...], wc_ref[0]))
    x1 = x_ref[...] + gt1_ref[0, 0, 0] * _dot(m.astype(BF16), wo_ref[0])
    h2 = _rms_rows(x1, g2_ref[0]) * (1.0 + sc2_ref[0, 0, 0]) + sh2_ref[0, 0, 0]
    up = jnp.maximum(_dot(h2.astype(BF16), wup_ref[0]), 0.0)
    x2 = x1 + gt2_ref[0, 0, 0] * _dot((up * up).astype(BF16), wdn_ref[0])
    outs[0][...] = x2
    if final:
        outs[1][...] = _rms_rows(x2, gf_ref[...])


def _merge_mlp(x, ya, ob, oc, gates, mods, g2, gf, wa, wb, wc, wo, wup, wdn, layer,
               *, tm, tiles_per_group, final):
    n, d = x.shape
    c = ya.shape[-1]
    dff = wup.shape[-1]
    row = lambda i: (i, 0)
    lay = lambda i: (layer, 0, 0)
    n_out = 2 if final else 1
    return pl.pallas_call(
        functools.partial(_merge_mlp_kernel, final=final),
        grid=(n // tm,),
        in_specs=[pl.BlockSpec((tm, d), row)] + [pl.BlockSpec((tm, c), row)] * 3 + [
            pl.BlockSpec((tm, 3 * d), row),
            _mod_spec(mods, 2, layer, tiles_per_group), _mod_spec(mods, 4, layer, tiles_per_group),
            _mod_spec(mods, 3, layer, tiles_per_group), _mod_spec(mods, 5, layer, tiles_per_group),
            pl.BlockSpec((1, 1, d), lay),
            pl.BlockSpec((1, d), lambda i: (0, 0)),
            _resident((1, c, d), lay), _resident((1, c, d), lay), _resident((1, c, d), lay),
            _resident((1, d, d), lay), _resident((1, d, dff), lay), _resident((1, dff, d), lay),
        ],
        out_specs=[pl.BlockSpec((tm, d), row)] * n_out,
        out_shape=[jax.ShapeDtypeStruct((n, d), F32)] * n_out,
        compiler_params=_params("arbitrary"),
        name="merge_mlp",
    )(x, ya, ob, oc, gates, mods, mods, mods, mods, g2, gf, wa, wb, wc, wo, wup, wdn)


def _t5_bucket_table(n_buckets, max_dist):
    dist = np.arange(max_dist + 1)
    max_exact = n_buckets // 2
    d = np.maximum(dist, 1).astype(np.float32)
    large = max_exact + (np.log(d / max_exact) / math.log(MAX_DISTANCE / max_exact)
                         * (n_buckets - max_exact)).astype(np.int32)
    large = np.minimum(large, n_buckets - 1)
    return np.where(dist < max_exact, dist, large).astype(np.int32)


def _pad_heads(w, n_heads):
    k = w.shape[-1] // n_heads
    w = w.reshape(w.shape[:-1] + (n_heads, k))
    w = jnp.pad(w, [(0, 0)] * (w.ndim - 1) + [(0, LANES - k)])
    return w.reshape(w.shape[:-2] + (n_heads * LANES,))


def kernel(x_prompt, x_sample, c_prompt, c_sample, cache_k, cache_v, page_table, state_conv, state_gla,
           w_ada, b_ada, g_norm1, w_in, w_dw, b_dw, ln_g, ln_b, w_pw2, w_pb, rel_bias, w_a2, b_a, g_gla,
           w_pc, w_o, g_norm2, w_up, w_down, g_final):
    bp, t, d = x_prompt.shape
    bs, ts, _ = x_sample.shape
    depth = w_in.shape[0]
    _, n_pool, page, h_b, hd_b = cache_k.shape
    d_b = h_b * hd_b
    d_conv = w_dw.shape[-1]
    width = w_dw.shape[1]
    rank = w_a2.shape[1]
    h_c = GLA_HEADS
    dk_c = w_a2.shape[-1] // h_c
    dv_c = state_gla.shape[-1]
    hp = h_c * LANES
    n_buckets = rel_bias.shape[0]
    n_pages = page_table.shape[1]
    past_len = n_pages * page
    assert dv_c == LANES and h_c * dv_c == hp and width - 1 <= CONV_HALO

    sizes = (2 * d_conv, d_b, d_b, d_b, h_c * dk_c, h_c * dk_c, h_c * dv_c, h_c * dv_c, rank, d, d, d)
    offs = np.cumsum((0,) + sizes)
    part = [w_in[:, :, offs[i]:offs[i + 1]] for i in range(len(sizes))]
    w_cat = jnp.concatenate([
        part[0], part[1], part[2], part[3],
        _pad_heads(part[4], h_c), _pad_heads(part[5], h_c), part[6], part[7],
        jnp.pad(part[8], ((0, 0), (0, 0), (0, LANES - rank))),
        part[9], part[10], part[11]], axis=-1).astype(BF16)
    wa2 = jnp.pad(_pad_heads(w_a2, h_c), ((0, 0), (0, LANES - rank), (0, 0))).astype(BF16)
    ba2 = _pad_heads(b_a, h_c).reshape(depth, 1, hp)
    gg = g_gla.reshape(depth, 1, hp)
    wpw2, wpb, wpc, wo, wup, wdn = (w.astype(BF16) for w in (w_pw2, w_pb, w_pc, w_o, w_up, w_down))
    g1 = g_norm1.reshape(depth, 1, d)
    g2 = g_norm2.reshape(depth, 1, d)
    gf = g_final.reshape(1, d)
    bdw = b_dw.reshape(depth, 1, d_conv)
    lng = ln_g.reshape(depth, 1, d_conv)
    lnb = ln_b.reshape(depth, 1, d_conv)

    n_c = bp + bs
    rows_c = -(-n_c // 8) * 8
    c_all = jnp.pad(jnp.concatenate([c_prompt, c_sample], axis=0), ((0, rows_c - n_c), (0, 0)))
    mods = _adaln(c_all, w_ada, b_ada).reshape(depth, rows_c, 6, d)

    blk = MOBA_BLOCK
    assert ts <= blk and past_len % blk == 0
    bucket = _t5_bucket_table(n_buckets, 2 * blk - 1)
    onehot = np.zeros((4 * blk, n_buckets), np.float32)
    onehot[np.arange(2 * blk), bucket] = 1.0
    by_dist = jnp.dot(onehot, rel_bias - rel_bias[n_buckets - 1:n_buckets, :],
                      precision=lax.Precision.HIGHEST).T
    toep = jnp.tile(by_dist, (1, blk))[:, :blk * (4 * blk - 1)].reshape(h_b, blk, 4 * blk - 1)
    causal = np.arange(blk)[:, None] <= np.arange(blk)[None, :]
    bias_t = jnp.stack([jnp.where(causal, toep[:, :, :blk] * LOG2E, NEG), toep[:, :, blk:2 * blk] * LOG2E],
                       axis=1)
    rows_s = ts * h_b
    own = jnp.transpose(toep[:, :ts, :ts], (2, 0, 1)).reshape(rows_s, ts)
    own_ok = np.repeat(np.arange(ts), h_b)[:, None] >= np.arange(LANES)[None, :]
    bias_own = jnp.where(own_ok, jnp.pad(own, ((0, 0), (0, LANES - ts))), NEG)
    bias_tail = jnp.transpose(toep[:, :, blk:blk + ts], (2, 0, 1)).reshape(rows_s, blk)
    assert past_len // blk <= LANES
    cache_kt = jnp.transpose(cache_k, (0, 1, 3, 4, 2)).reshape(depth, n_pool, d_b, page)
    cache_vt = jnp.transpose(cache_v, (0, 1, 3, 4, 2)).reshape(depth, n_pool, d_b, page)

    xp = x_prompt.reshape(bp * t, d)
    xs = x_sample.reshape(bs * ts, d)
    n_s = bs * ts
    tm = TOKEN_TILE
    tpb = t // tm
    dims = (d_conv, d_b, hp, hd_b, dk_c)
    ts_pad = 8
    gla_rows_s = GLA_CHUNK

    cp_l, sp_l, cs_l, ss_l = ([] for _ in range(4))
    y_prompt = y_sample = None
    zeros_hist = jnp.zeros((bp, CONV_HALO, d_conv), F32)
    zeros_state = jnp.zeros((bp, h_c, LANES, LANES), F32)
    mods = jnp.transpose(mods, (0, 2, 1, 3))
    mods_p = mods[:, :, :bp].reshape(depth, 6, bp, 1, d)
    mods_s = jnp.repeat(mods[:, :, bp:n_c], ts, axis=2).reshape(depth, 6, 1, n_s, d)
    kv_p = kv_s = None
    for l in range(depth):
        final = l == depth - 1

        (u, qb, kb, vb, k16, vt, km, gq, gk, gv, gr, lr, gates) = _inproj(
            xp, mods_p, g1, w_cat, kv_p, l, tm=tm, tiles_per_group=tpb, dims=dims, attn_aux=True)
        kv_p = (kb, vb)
        ya, tail = _conv_prompt(u.reshape(bp, t, d_conv), zeros_hist, w_dw, bdw, lng, lnb, l, tt=tm)
        ob = _moba_prompt(qb, k16, vt, km, bias_t, batch=bp, n_heads=h_b, hd=hd_b)
        oc, stf = _gla(*(a.reshape(bp, t, -1) for a in (gq, gk, gv, gr, lr)), wa2, ba2, gg, zeros_state, l,
                       tt=tm, valid=tm)
        res = _merge_mlp(xp, ya.reshape(bp * t, d_conv), ob.reshape(bp * t, d_b), oc.reshape(bp * t, hp),
                         gates, mods_p, g2, gf, wpw2, wpb, wpc, wo, wup, wdn, l,
                         tm=tm, tiles_per_group=tpb, final=final)
        xp = res[0]
        if final:
            y_prompt = res[1].reshape(bp, t, d)
        cp_l.append(tail[:, CONV_HALO - (width - 1):])
        sp_l.append(jnp.swapaxes(stf[..., :dk_c], -1, -2))

        (u, qb, kb, vb, gq, gk, gv, gr, lr, gates) = _inproj(
            xs, mods_s, g1, w_cat, kv_s, l, tm=n_s, tiles_per_group=1, dims=dims, attn_aux=False)
        kv_s = (kb, vb)
        kb, vb = kb[l], vb[l]
        ext = jnp.concatenate([state_conv[l], u.reshape(bs, ts, d_conv)], axis=1)
        ext_p = jnp.pad(ext, ((0, 0), (0, ts_pad + width - 1 - ext.shape[1] + (-(ts_pad + width - 1)) % 8), (0, 0)))
        ya = _conv_sample(ext_p, w_dw, bdw, lng, lnb, l, rows=ts_pad)[:, :ts]
        q_rep = jnp.repeat(qb.reshape(bs, ts, d_b), h_b, axis=1)
        pad_new = lambda a: jnp.pad(a.reshape(bs, ts, d_b), ((0, 0), (0, ts_pad - ts), (0, 0)))
        ob = _moba_sample(page_table, q_rep, pad_new(kb), pad_new(vb), bias_own, bias_tail,
                          cache_kt, cache_vt, l, n_heads=h_b, t_new=ts)
        pad_t = lambda a: jnp.pad(a.reshape(bs, ts, -1), ((0, 0), (0, gla_rows_s - ts), (0, 0)))
        st0 = jnp.pad(jnp.swapaxes(state_gla[l], -1, -2), ((0, 0), (0, 0), (0, 0), (0, LANES - dk_c)))
        oc, stf = _gla(pad_t(gq), pad_t(gk), pad_t(gv), pad_t(gr), pad_t(lr), wa2, ba2, gg, st0, l,
                       tt=gla_rows_s, valid=ts)
        res = _merge_mlp(xs, ya.reshape(n_s, d_conv), ob.reshape(n_s, d_b), oc[:, :ts].reshape(n_s, hp),
                         gates, mods_s, g2, gf, wpw2, wpb, wpc, wo, wup, wdn, l,
                         tm=n_s, tiles_per_group=1, final=final)
        xs = res[0]
        if final:
            y_sample = res[1].reshape(bs, ts, d)
        cs_l.append(ext[:, -(width - 1):])
        ss_l.append(jnp.swapaxes(stf[..., :dk_c], -1, -2))

    rows_p = lambda a: jnp.transpose(a.reshape(depth, bp, h_b, hd_b, t), (0, 1, 4, 2, 3))
    rows_s_ = lambda a: a.reshape(depth, bs, ts, h_b, hd_b)
    return (y_prompt, y_sample, rows_p(kv_p[0]), rows_p(kv_p[1]), jnp.stack(cp_l), jnp.stack(sp_l),
            rows_s_(kv_s[0]), rows_s_(kv_s[1]), jnp.stack(cs_l), jnp.stack(ss_l))
```

```python
import functools
import math

import numpy as np
import jax
import jax.numpy as jnp
from jax import lax
from jax.experimental import pallas as pl
from jax.experimental.pallas import tpu as pltpu

F32 = jnp.float32
BF16 = jnp.bfloat16

MOBA_BLOCK = 256
MOBA_TOPK = 3
MAX_DISTANCE = 128
GLA_HEADS = 4
GLA_CHUNK = 16
GLA_UNROLL = 4
GATE_NORM = 16.0
EPS = 1e-6

LANES = 128
SUBLANES = 8
VMEM_LIMIT = 56 * 1024 * 1024

NEG = -1e30
LOG2E = math.log2(math.e)
V_ROW_PAD = 16
TOKEN_TILE = 256
PAGES_PER_CHUNK = 32
CONV_HALO = 32

_NT = (((1,), (1,)), ((), ()))


def _dot(a, b):
    return jnp.dot(a, b, preferred_element_type=F32)


def _dot_nt(a, b):
    return lax.dot_general(a, b, _NT, preferred_element_type=F32)


def _dot_f32(a, b):
    return jnp.dot(a, b, precision=lax.Precision.HIGHEST, preferred_element_type=F32)


def _dot_nt_f32(a, b):
    return lax.dot_general(a, b, _NT, precision=lax.Precision.HIGHEST, preferred_element_type=F32)


def _params(*sem):
    return pltpu.CompilerParams(dimension_semantics=sem, vmem_limit_bytes=VMEM_LIMIT)


def _resident(shape, index_map):
    return pl.BlockSpec(shape, index_map, pipeline_mode=pl.Buffered(1))


def _rms_rows(x, g):
    return (x * lax.rsqrt(jnp.mean(x * x, axis=-1, keepdims=True) + EPS)) * g


def _log_sigmoid(x):
    return jnp.minimum(x, 0.0) - jnp.log1p(jnp.exp(-jnp.abs(x)))


def _top_k_mask(s, idx, axis, k, sentinel):
    sel = jnp.zeros(s.shape, dtype=jnp.bool_)
    for _ in range(k):
        mx = jnp.max(s, axis=axis, keepdims=True)
        first = jnp.min(jnp.where(s == mx, idx, sentinel), axis=axis, keepdims=True)
        pick = (idx == first) & (mx > -jnp.inf)
        sel = sel | pick
        s = jnp.where(pick, -jnp.inf, s)
    return sel


def _adaln_kernel(c_ref, w_ref, b_ref, o_ref):
    c = c_ref[...]
    s = c * jax.nn.sigmoid(c)
    o_ref[0] = _dot(s.astype(BF16), w_ref[0].astype(BF16)) + b_ref[0]


def _adaln(c_all, w_ada, b_ada):
    depth, d, n = w_ada.shape
    rows = c_all.shape[0]
    tn = 1536
    return pl.pallas_call(
        _adaln_kernel,
        grid=(depth, n // tn),
        in_specs=[
            pl.BlockSpec((rows, d), lambda l, j: (0, 0)),
            pl.BlockSpec((1, d, tn), lambda l, j: (l, 0, j)),
            pl.BlockSpec((1, 1, tn), lambda l, j: (l, 0, j)),
        ],
        out_specs=pl.BlockSpec((1, rows, tn), lambda l, j: (l, 0, j)),
        out_shape=jax.ShapeDtypeStruct((depth, rows, n), F32),
        compiler_params=_params("arbitrary", "arbitrary"),
        name="adaln",
    )(c_all, w_ada, b_ada.reshape(depth, 1, n))


def _inproj_kernel(x_ref, sc_ref, sh_ref, g_ref, w_ref, *rest, d_conv, d_b, hp, d_model,
                   q_scale, gq_scale, n_heads_b, attn_aux, n_alias, kv_layer):
    outs = rest[n_alias:]
    if attn_aux:
        (u_ref, q_ref, k_ref, v_ref, k16_ref, vt_ref, km_ref,
         gq_ref, gk_ref, gv_ref, gr_ref, lr_ref, gate_ref) = outs
    else:
        (u_ref, q_ref, k_ref, v_ref, gq_ref, gk_ref, gv_ref, gr_ref, lr_ref, gate_ref) = outs
    x = x_ref[...]
    h = _rms_rows(x, g_ref[0]) * (1.0 + sc_ref[0, 0, 0]) + sh_ref[0, 0, 0]
    hb = h.astype(BF16)

    pos = [0]

    def seg(width):
        lo = pos[0]
        pos[0] = lo + width
        return _dot(hb, w_ref[0, :, lo:lo + width])

    a = seg(d_conv)
    g = seg(d_conv)
    u_ref[...] = a * jax.nn.sigmoid(g)
    q_ref[...] = seg(d_b) * q_scale
    k = seg(d_b)
    v = seg(d_b)
    def put(ref, val):
        if n_alias:
            ref[0] = val
        else:
            for l in range(ref.shape[0]):
                ref[l] = val if l == kv_layer else jnp.zeros_like(val)

    if not attn_aux:
        put(k_ref, k)
        put(v_ref, v)
    else:
        v_t = v.T
        put(k_ref, k.T[None])
        put(v_ref, v_t[None])
        k16_ref[0] = k.astype(BF16)
        vt = v_t.astype(BF16)
        hd = d_b // n_heads_b
        tail = jnp.where(lax.broadcasted_iota(jnp.int32, (V_ROW_PAD, vt.shape[1]), 0) == 0, 1.0, 0.0).astype(BF16)
        for h in range(n_heads_b):
            vt_ref[0, h * (hd + V_ROW_PAD):h * (hd + V_ROW_PAD) + hd, :] = vt[h * hd:(h + 1) * hd, :]
            vt_ref[0, h * (hd + V_ROW_PAD) + hd:(h + 1) * (hd + V_ROW_PAD), :] = tail
        km_ref[0] = jnp.mean(k, axis=0, keepdims=True)
    gq_ref[...] = seg(hp) * gq_scale
    gk_ref[...] = seg(hp)
    gv_ref[...] = seg(hp)
    gr_ref[...] = seg(hp)
    lr_ref[...] = seg(LANES)
    for i in range(3):
        gate_ref[:, i * d_model:(i + 1) * d_model] = jax.nn.sigmoid(seg(d_model))


def _mod_spec(mods, idx, layer, tiles_per_group):
    return pl.BlockSpec((1, 1, 1) + mods.shape[3:], lambda i: (layer, idx, i // tiles_per_group, 0, 0))


def _inproj(x, mods, g1, w_cat, kv_bufs, layer, *, tm, tiles_per_group, dims, attn_aux):
    n, d = x.shape
    d_conv, d_b, hp, hd_b, dk_c = dims
    depth, _, nw = w_cat.shape
    grid = (n // tm,)
    row = lambda i: (i, 0)
    lay = lambda i: (layer, 0, 0)
    lead, l0 = (depth, 0) if kv_bufs is None else (1, layer)
    if attn_aux:
        tpg = tiles_per_group
        kv_shape = (depth, n // (tpg * tm), d_b, tpg * tm)
        kv_spec = pl.BlockSpec((lead, 1, d_b, tm), lambda i: (l0, i // tpg, 0, i % tpg))
    else:
        kv_shape, kv_spec = (depth, n, d_b), pl.BlockSpec((lead, tm, d_b), lambda i: (l0, i, 0))
    outs = [((n, d_conv), F32), ((n, d_b), F32), (kv_shape, F32), (kv_shape, F32)]
    specs = [pl.BlockSpec((tm, d_conv), row), pl.BlockSpec((tm, d_b), row), kv_spec, kv_spec]
    if attn_aux:
        assert tm == MOBA_BLOCK
        nblk = n // MOBA_BLOCK
        vt_rows = d_b + (d_b // hd_b) * V_ROW_PAD
        outs += [((nblk, MOBA_BLOCK, d_b), BF16), ((nblk, vt_rows, MOBA_BLOCK), BF16), ((nblk, 1, d_b), F32)]
        specs += [pl.BlockSpec((1, MOBA_BLOCK, d_b), lambda i: (i, 0, 0)),
                  pl.BlockSpec((1, vt_rows, MOBA_BLOCK), lambda i: (i, 0, 0)),
                  pl.BlockSpec((1, 1, d_b), lambda i: (i, 0, 0))]
    outs += [((n, hp), F32)] * 4 + [((n, LANES), F32), ((n, 3 * d), F32)]
    specs += [pl.BlockSpec((tm, hp), row)] * 4 + [pl.BlockSpec((tm, LANES), row), pl.BlockSpec((tm, 3 * d), row)]
    body = functools.partial(_inproj_kernel, d_conv=d_conv, d_b=d_b, hp=hp, d_model=d,
                             q_scale=hd_b ** -0.5, gq_scale=dk_c ** -0.5, n_heads_b=d_b // hd_b,
                             attn_aux=attn_aux, n_alias=0 if kv_bufs is None else 2, kv_layer=layer)
    in_specs = [
        pl.BlockSpec((tm, d), row),
        _mod_spec(mods, 1, layer, tiles_per_group),
        _mod_spec(mods, 0, layer, tiles_per_group),
        pl.BlockSpec((1, 1, d), lay),
        _resident((1, d, nw), lay),
    ]
    args = [x, mods, mods, g1, w_cat]
    aliases = {}
    if kv_bufs is not None:
        in_specs += [pl.BlockSpec(memory_space=pl.ANY)] * 2
        aliases = {len(args): 2, len(args) + 1: 3}
        args += list(kv_bufs)
    return pl.pallas_call(
        body,
        grid=grid,
        in_specs=in_specs,
        out_specs=specs,
        out_shape=[jax.ShapeDtypeStruct(s, t) for s, t in outs],
        input_output_aliases=aliases,
        compiler_params=_params("arbitrary"),
        name="inproj",
    )(*args)


def _conv_taps(ext_ref, start, rows, wdw_ref, bdw_ref, lng_ref, lnb_ref, width, shift_scr=None):
    acc = jnp.zeros((rows, wdw_ref.shape[-1]), F32) + bdw_ref[0]
    if shift_scr is None:
        for k in range(width):
            acc = acc + ext_ref[pl.ds(start + k, rows), :] * wdw_ref[0, k:k + 1, :]
    else:
        span = shift_scr.shape[1]
        for r in range(1, SUBLANES):
            shift_scr[r - 1] = ext_ref[pl.ds(r, span), :]
        for k in range(width):
            a, r = divmod(start + k, SUBLANES)
            src = ext_ref if r == 0 else shift_scr.at[r - 1]
            acc = acc + src[pl.ds(a * SUBLANES, rows), :] * wdw_ref[0, k:k + 1, :]
    mu = jnp.mean(acc, axis=-1, keepdims=True)
    cen = acc - mu
    var = jnp.mean(cen * cen, axis=-1, keepdims=True)
    y = (cen * lax.rsqrt(var + EPS)) * lng_ref[0] + lnb_ref[0]
    return y * jax.nn.sigmoid(y)


def _conv_kernel(u_ref, hist_ref, wdw_ref, bdw_ref, lng_ref, lnb_ref, y_ref, tail_ref, ext_scr, shift_scr,
                 *, tt, width):
    j = pl.program_id(1)

    @pl.when(j == 0)
    def _():
        ext_scr[0:CONV_HALO, :] = hist_ref[0]

    @pl.when(j > 0)
    def _():
        ext_scr[0:CONV_HALO, :] = ext_scr[tt:tt + CONV_HALO, :]

    ext_scr[CONV_HALO:CONV_HALO + tt, :] = u_ref[0]
    start = CONV_HALO - (width - 1)
    y_ref[0] = _conv_taps(ext_scr, start, tt, wdw_ref, bdw_ref, lng_ref, lnb_ref, width, shift_scr).astype(BF16)

    @pl.when(j == pl.num_programs(1) - 1)
    def _():
        tail_ref[0] = ext_scr[tt:tt + CONV_HALO, :]


def _conv_prompt(u, hist, w_dw, b_dw, ln_g, ln_b, layer, *, tt):
    b, t, c = u.shape
    width = w_dw.shape[1]
    lay = lambda bi, j: (layer, 0, 0)
    return pl.pallas_call(
        functools.partial(_conv_kernel, tt=tt, width=width),
        grid=(b, t // tt),
        in_specs=[
            pl.BlockSpec((1, tt, c), lambda bi, j: (bi, j, 0)),
            pl.BlockSpec((1, CONV_HALO, c), lambda bi, j: (bi, 0, 0)),
            pl.BlockSpec((1, width, c), lay),
            pl.BlockSpec((1, 1, c), lay),
            pl.BlockSpec((1, 1, c), lay),
            pl.BlockSpec((1, 1, c), lay),
        ],
        out_specs=[pl.BlockSpec((1, tt, c), lambda bi, j: (bi, j, 0)),
                   pl.BlockSpec((1, CONV_HALO, c), lambda bi, j: (bi, 0, 0))],
        out_shape=[jax.ShapeDtypeStruct((b, t, c), BF16), jax.ShapeDtypeStruct((b, CONV_HALO, c), F32)],
        scratch_shapes=[pltpu.VMEM((CONV_HALO + tt, c), F32),
                        pltpu.VMEM((SUBLANES - 1, CONV_HALO + tt - SUBLANES, c), F32)],
        compiler_params=_params("arbitrary", "arbitrary"),
        name="conv_prompt",
    )(u, hist, w_dw, b_dw, ln_g, ln_b)


def _conv_small_kernel(ext_ref, wdw_ref, bdw_ref, lng_ref, lnb_ref, y_ref, *, rows, width):
    y_ref[0] = _conv_taps(ext_ref.at[0], 0, rows, wdw_ref, bdw_ref, lng_ref, lnb_ref, width).astype(BF16)


def _conv_sample(ext, w_dw, b_dw, ln_g, ln_b, layer, *, rows):
    b, r, c = ext.shape
    width = w_dw.shape[1]
    lay = lambda bi: (layer, 0, 0)
    return pl.pallas_call(
        functools.partial(_conv_small_kernel, rows=rows, width=width),
        grid=(b,),
        in_specs=[
            pl.BlockSpec((1, r, c), lambda bi: (bi, 0, 0)),
            pl.BlockSpec((1, width, c), lay),
            pl.BlockSpec((1, 1, c), lay),
            pl.BlockSpec((1, 1, c), lay),
            pl.BlockSpec((1, 1, c), lay),
        ],
        out_specs=pl.BlockSpec((1, rows, c), lambda bi: (bi, 0, 0)),
        out_shape=jax.ShapeDtypeStruct((b, rows, c), BF16),
        compiler_params=_params("arbitrary"),
        name="conv_sample",
    )(ext, w_dw, b_dw, ln_g, ln_b)


def _moba_prompt_kernel(q_ref, k16_ref, vt_ref, km_ref, bias_ref, o_ref,
                        qm_scr, sel_scr, m_scr, acc_scr, s2_scr, *, nb, n_heads, hd):
    i = pl.program_id(1)
    blk = MOBA_BLOCK
    q = q_ref[0]
    km = km_ref[0]
    lane = lax.broadcasted_iota(jnp.int32, (blk, LANES), 1)
    blk_idx = lax.broadcasted_iota(jnp.int32, (nb, blk), 0)
    heads_per_slab = LANES // hd

    km16 = km.astype(BF16)
    for h in range(n_heads):
        p, w = divmod(h, heads_per_slab)
        qm = jnp.where((lane // hd) == w, q[:, p * LANES:(p + 1) * LANES], 0.0)
        qm_scr[h] = (qm * LOG2E).astype(BF16)
        s = _dot_nt(km16[:, p * LANES:(p + 1) * LANES], qm.astype(BF16))
        s = jnp.where(blk_idx < i, s, -jnp.inf)
        sel = _top_k_mask(s, blk_idx, 0, MOBA_TOPK, nb)
        sel_scr[h] = jnp.where(sel, 1.0, 0.0)

    hv = vt_ref.shape[1] // n_heads

    def block(n, kind):
        kb = k16_ref[n]
        vb = vt_ref[n]

        for h in range(n_heads):
            p = h // heads_per_slab
            s2_scr[h, 0:blk, :] =_dot_nt(kb[:, p * LANES:(p + 1) * LANES], qm_scr[h])
        for h in range(n_heads):
            s = s2_scr[h, 0:blk, :]
            rows = slice(h * hv, (h + 1) * hv)
            if kind == "own":
                s = s + bias_ref[h, 0]
                m = jnp.max(s, axis=0, keepdims=True)
                m_scr[h:h + 1, :] = m
                acc_scr[rows, :] = _dot(vb[rows, :], jnp.exp2(s - m).astype(BF16))
            else:
                picked = sel_scr[h, pl.ds(n, 1), :] > 0.5
                m_old = m_scr[h:h + 1, :]
                m = jnp.where(picked, jnp.maximum(m_old, jnp.max(s, axis=0, keepdims=True)), m_old)
                alpha = jnp.exp2(m_old - m)
                pr = jnp.exp2(s - jnp.where(picked, m, -NEG))
                m_scr[h:h + 1, :] = m
                acc_scr[rows, :] = alpha * acc_scr[rows, :] + _dot(vb[rows, :], pr.astype(BF16))

    def pair_logits(n, buf):
        kb = k16_ref[pl.ds(n, 2)].reshape(2 * blk, k16_ref.shape[-1])
        for h in range(n_heads):
            p = h // heads_per_slab
            buf[h] = _dot_nt(kb[:, p * LANES:(p + 1) * LANES], qm_scr[h])

    @pl.when(i == 0)
    def _():
        block(0, "own")

    @pl.when(i >= 1)
    def _():
        n = i - 1
        pair_logits(n, s2_scr)
        vb = jnp.concatenate([vt_ref[n], vt_ref[n + 1]], axis=1)
        for h in range(n_heads):
            rows = slice(h * hv, (h + 1) * hv)
            s_a = s2_scr[h, 0:blk, :] + bias_ref[h, 1]
            s_b = s2_scr[h, blk:2 * blk, :] + bias_ref[h, 0]
            pick_a = sel_scr[h, pl.ds(n, 1), :] > 0.5
            m = jnp.maximum(jnp.where(pick_a, jnp.max(s_a, axis=0, keepdims=True), NEG),
                            jnp.max(s_b, axis=0, keepdims=True))
            pr = jnp.concatenate([jnp.exp2(s_a - jnp.where(pick_a, m, -NEG)).astype(BF16),
                                  jnp.exp2(s_b - m).astype(BF16)], axis=0)
            m_scr[h:h + 1, :] = m
            acc_scr[rows, :] = _dot(vb[rows, :], pr)

    def pair_update(n, buf):
        vb = jnp.concatenate([vt_ref[n], vt_ref[n + 1]], axis=1)
        for h in range(n_heads):
            rows = slice(h * hv, (h + 1) * hv)
            s_a = buf[h, 0:blk, :]
            s_b = buf[h, blk:2 * blk, :]
            pick_a = sel_scr[h, pl.ds(n, 1), :] > 0.5
            pick_b = sel_scr[h, pl.ds(n + 1, 1), :] > 0.5
            m_old = m_scr[h:h + 1, :]
            m = jnp.maximum(m_old, jnp.maximum(
                jnp.where(pick_a, jnp.max(s_a, axis=0, keepdims=True), NEG),
                jnp.where(pick_b, jnp.max(s_b, axis=0, keepdims=True), NEG)))
            alpha = jnp.exp2(m_old - m)
            pr = jnp.concatenate([jnp.exp2(s_a - jnp.where(pick_a, m, -NEG)).astype(BF16),
                                  jnp.exp2(s_b - jnp.where(pick_b, m, -NEG)).astype(BF16)], axis=0)
            m_scr[h:h + 1, :] = m
            acc_scr[rows, :] = alpha * acc_scr[rows, :] + _dot(vb[rows, :], pr)

    n_far = jnp.maximum(i - 1, 0)

    def far(t, carry):
        pair_logits(2 * t, s2_scr)
        pair_update(2 * t, s2_scr)
        return carry

    lax.fori_loop(0, n_far // 2, far, 0)

    @pl.when(n_far % 2 == 1)
    def _():
        block(n_far - 1, "far")

    out = [acc_scr[h * hv:h * hv + hd, :] / acc_scr[h * hv + hd:h * hv + hd + 1, :] for h in range(n_heads)]
    o_ref[0] = jnp.concatenate(out, axis=0).T.astype(BF16)


def _moba_prompt(q, k16, vt, km, bias_t, *, batch, n_heads, hd):
    n, d_b = q.shape
    t = n // batch
    nb = t // MOBA_BLOCK
    blk = MOBA_BLOCK
    body = functools.partial(_moba_prompt_kernel, nb=nb, n_heads=n_heads, hd=hd)
    return pl.pallas_call(
        body,
        grid=(batch, nb),
        in_specs=[
            pl.BlockSpec((1, blk, d_b), lambda b, i: (b * nb + i, 0, 0)),
            pl.BlockSpec((nb, blk, d_b), lambda b, i: (b, 0, 0)),
            pl.BlockSpec((nb, vt.shape[1], blk), lambda b, i: (b, 0, 0)),
            pl.BlockSpec((1, nb, d_b), lambda b, i: (b, 0, 0)),
            _resident((n_heads, 2, blk, blk), lambda b, i: (0, 0, 0, 0)),
        ],
        out_specs=pl.BlockSpec((1, blk, d_b), lambda b, i: (b * nb + i, 0, 0)),
        out_shape=jax.ShapeDtypeStruct((n // blk, blk, d_b), BF16),
        scratch_shapes=[
            pltpu.VMEM((n_heads, blk, LANES), BF16),
            pltpu.VMEM((n_heads, nb, blk), F32),
            pltpu.VMEM((n_heads, blk), F32),
            pltpu.VMEM((vt.shape[1], blk), F32),
            pltpu.VMEM((n_heads, 2 * blk, blk), F32),
        ],
        compiler_params=_params("arbitrary", "arbitrary"),
        name="moba_prompt",
    )(q.reshape(n // blk, blk, d_b), k16, vt, km.reshape(batch, nb, d_b), bias_t)


def _moba_sample_kernel(pt_ref, q_ref, kn_ref, vn_ref, bown_ref, btail_ref, kt_hbm, vt_hbm, o_ref,
                        s_scr, kv_scr, kbuf, vbuf, ksem, vsem,
                        *, layer, n_chunks, n_heads, t_new, page, ppc):
    b = pl.program_id(0)
    rows, d_b = q_ref.shape[1:]
    hd = d_b // n_heads
    blocks_per_chunk = ppc * page // MOBA_BLOCK
    pages_per_block = MOBA_BLOCK // page
    nbp = n_chunks * blocks_per_chunk

    def page_copy(hbm, buf, sem, seq, chunk, p):
        slot = chunk % 2
        return pltpu.make_async_copy(hbm.at[layer, pt_ref[seq, chunk * ppc + p]], buf.at[slot, p], sem.at[slot])

    def start_chunk(hbm, buf, sem, seq, chunk):
        for p in range(ppc):
            page_copy(hbm, buf, sem, seq, chunk, p).start()

    def wait_chunk(hbm, buf, sem, seq, chunk):
        for p in range(ppc):
            page_copy(hbm, buf, sem, seq, chunk, p).wait()

    @pl.when(b == 0)
    def _():
        start_chunk(kt_hbm, kbuf, ksem, b, 0)

    row_i = lax.broadcasted_iota(jnp.int32, (rows, d_b), 0)
    lane_i = lax.broadcasted_iota(jnp.int32, (rows, d_b), 1)
    head_mask = (lane_i // hd) == (row_i % n_heads)
    blk_lane = lax.broadcasted_iota(jnp.int32, (rows, LANES), 1)
    qb16 = jnp.where(head_mask, q_ref[0], 0.0).astype(BF16)

    bsum = jnp.zeros((rows, LANES), F32)
    for c in range(n_chunks):
        if c + 1 < n_chunks:
            start_chunk(kt_hbm, kbuf, ksem, b, c + 1)
        else:
            start_chunk(vt_hbm, vbuf, vsem, b, 0)
        wait_chunk(kt_hbm, kbuf, ksem, b, c)
        for p in range(ppc):
            s = _dot(qb16, kbuf[c % 2, p].astype(BF16))
            s_scr[c, :, p * page:(p + 1) * page] = s
            ps = jnp.sum(s, axis=-1, keepdims=True)
            psum = ps if p % pages_per_block == 0 else psum + ps
            if p % pages_per_block == pages_per_block - 1:
                bsum = jnp.where(blk_lane == c * blocks_per_chunk + p // pages_per_block, psum, bsum)

    bs = jnp.where(blk_lane < nbp, bsum, -jnp.inf)
    sel = _top_k_mask(bs, blk_lane, 1, MOBA_TOPK, LANES)
    blk_bias = jnp.where(sel, 0.0, NEG)

    kv_scr[...] = jnp.zeros(kv_scr.shape, F32)
    kv_scr[0:kn_ref.shape[1], :] = kn_ref[0]
    s_own = _dot_nt(qb16, kv_scr[...].astype(BF16)) + bown_ref[...]
    m = jnp.max(s_own, axis=-1, keepdims=True)
    for blk in range(nbp):
        cc, off = divmod(blk, blocks_per_chunk)
        cols = slice(off * MOBA_BLOCK, (off + 1) * MOBA_BLOCK)
        s = s_scr[cc, :, cols] + blk_bias[:, blk:blk + 1]
        if blk == nbp - 1:
            s = s + btail_ref[...]
        s_scr[cc, :, cols] = s
        m = jnp.maximum(m, jnp.max(s, axis=-1, keepdims=True))

    p_own = jnp.exp(s_own - m)
    l = jnp.sum(p_own, axis=-1, keepdims=True)
    for cc in range(n_chunks):
        pr = jnp.exp(s_scr[cc] - m)
        l = l + jnp.sum(pr, axis=-1, keepdims=True)
        s_scr[cc] = pr
    kv_scr[0:vn_ref.shape[1], :] = vn_ref[0]
    acc = _dot(p_own.astype(BF16), kv_scr[...].astype(BF16))

    for c in range(n_chunks):
        if c + 1 < n_chunks:
            start_chunk(vt_hbm, vbuf, vsem, b, c + 1)
        else:
            @pl.when(b + 1 < pl.num_programs(0))
            def _():
                start_chunk(kt_hbm, kbuf, ksem, b + 1, 0)
        wait_chunk(vt_hbm, vbuf, vsem, b, c)
        for p in range(ppc):
            pr = s_scr[c, :, p * page:(p + 1) * page]
            acc = acc + _dot_nt(pr.astype(BF16), vbuf[c % 2, p].astype(BF16))

    o = jnp.where(head_mask, acc / l, 0.0)
    o_ref[0] = jnp.sum(o.reshape(t_new, n_heads, d_b), axis=1).astype(BF16)


def _moba_sample(page_table, q_rep, k_new, v_new, bias_own, bias_tail, cache_kt, cache_vt, layer,
                 *, n_heads, t_new):
    b, rows, d_b = q_rep.shape
    page = cache_kt.shape[-1]
    n_pages = page_table.shape[1]
    ppc = min(PAGES_PER_CHUNK, n_pages)
    n_chunks = n_pages // ppc
    assert n_pages % ppc == 0 and (ppc * page) % MOBA_BLOCK == 0 and MOBA_BLOCK % page == 0
    chunk = ppc * page

    per_b = lambda bi, pt: (bi, 0, 0)
    const2 = lambda bi, pt: (0, 0)
    in_specs = [
        pl.BlockSpec((1, rows, d_b), per_b),
        pl.BlockSpec((1, k_new.shape[1], d_b), per_b),
        pl.BlockSpec((1, v_new.shape[1], d_b), per_b),
        pl.BlockSpec(bias_own.shape, const2),
        pl.BlockSpec(bias_tail.shape, const2),
        pl.BlockSpec(memory_space=pl.ANY),
        pl.BlockSpec(memory_space=pl.ANY),
    ]
    body = functools.partial(_moba_sample_kernel, layer=layer, n_chunks=n_chunks, n_heads=n_heads,
                             t_new=t_new, page=page, ppc=ppc)
    grid_spec = pltpu.PrefetchScalarGridSpec(
        num_scalar_prefetch=1,
        grid=(b,),
        in_specs=in_specs,
        out_specs=pl.BlockSpec((1, t_new, d_b), per_b),
        scratch_shapes=[
            pltpu.VMEM((n_chunks, rows, chunk), F32),
            pltpu.VMEM((LANES, d_b), F32),
            pltpu.VMEM((2, ppc, d_b, page), F32),
            pltpu.VMEM((2, ppc, d_b, page), F32),
            pltpu.SemaphoreType.DMA((2,)),
            pltpu.SemaphoreType.DMA((2,)),
        ],
    )
    return pl.pallas_call(
        body,
        grid_spec=grid_spec,
        out_shape=jax.ShapeDtypeStruct((b, t_new, d_b), BF16),
        compiler_params=_params("arbitrary"),
        name="moba_sample",
    )(page_table, q_rep, k_new, v_new, bias_own, bias_tail, cache_kt, cache_vt)


def _gla_kernel(gq_ref, gk_ref, gv_ref, gr_ref, lr_ref, wa_ref, ba_ref, gg_ref, st0_ref,
                o_ref, stf_ref, st_scr, bc_scr, bl_scr, qd_scr, kd_scr, o_scr,
                *, tt, valid, n_heads):
    j = pl.program_id(1)
    c_sz = GLA_CHUNK
    hp = n_heads * LANES

    @pl.when(j == 0)
    def _():
        st_scr[...] = st0_ref[0]

    x = _dot(lr_ref[0].astype(BF16), wa_ref[0]) + ba_ref[0]
    la = _log_sigmoid(x) * (1.0 / GATE_NORM)
    if valid < tt:
        la = jnp.where(lax.broadcasted_iota(jnp.int32, (tt, hp), 0) < valid, la, 0.0)
    r_i = lax.broadcasted_iota(jnp.int32, (tt, tt), 0)
    c_i = lax.broadcasted_iota(jnp.int32, (tt, tt), 1)
    low = jnp.where(((r_i // c_sz) == (c_i // c_sz)) & (c_i <= r_i), 1.0, 0.0).astype(BF16)
    la_hi = la.astype(BF16)
    la_lo = (la - la_hi.astype(F32)).astype(BF16)
    bc = _dot(low, la_hi) + _dot(low, la_lo)
    bc_scr[...] = bc
    n_ch = tt // c_sz
    in_ch = (lax.broadcasted_iota(jnp.int32, (n_ch, tt), 1) // c_sz
             == lax.broadcasted_iota(jnp.int32, (n_ch, tt), 0))
    tot = jnp.where(in_ch, 1.0, 0.0).astype(BF16)
    ends = _dot(tot, la_hi) + _dot(tot, la_lo)
    bl_scr[...] = ends
    bl = jnp.broadcast_to(ends[:, None, :], (n_ch, c_sz, hp)).reshape(tt, hp)
    qd_scr[...] = (gq_ref[0] * jnp.exp(bc)).astype(BF16)
    kd_scr[...] = (gk_ref[0] * jnp.exp(bl - bc)).astype(BF16)

    half = c_sz // 2
    t_half = lax.broadcasted_iota(jnp.int32, (half, LANES), 0)

    def chunk(c, carry):
        rows = pl.ds(pl.multiple_of(c * c_sz, c_sz), c_sz)
        bc_c = bc_scr[rows, :]
        q_c = gq_ref[0, rows, :]
        k_c = gk_ref[0, rows, :]
        v_c = gv_ref[0, rows, :]
        decay = jnp.exp(bl_scr[pl.ds(c, 1), :])
        for h in range(n_heads):
            hs = slice(h * LANES, (h + 1) * LANES)
            st = st_scr[h]
            o_h = _dot_nt(qd_scr[rows, hs], st.astype(BF16))
            bch = bc_c[:, hs]
            qh = q_c[:, hs]
            parts = [o_h[0:half], o_h[half:c_sz]]
            for s in range(c_sz):
                for hi in range(2):
                    r0 = hi * half
                    if s >= r0 + half:
                        continue
                    diff = bch[r0:r0 + half] - bch[s:s + 1, :]
                    if s > r0:
                        diff = jnp.where(t_half + r0 >= s, diff, -jnp.inf)
                    a_s = jnp.sum(qh[r0:r0 + half] * k_c[s:s + 1, hs] * jnp.exp(diff), axis=-1, keepdims=True)
                    parts[hi] = parts[hi] + a_s * v_c[s:s + 1, hs]
            o_scr[rows, hs] = jnp.concatenate(parts, axis=0)
            kvt = lax.dot_general(v_c[:, hs].astype(BF16), kd_scr[rows, hs], (((0,), (0,)), ((), ())),
                                  preferred_element_type=F32)
            st_scr[h] = decay[:, hs] * st + kvt
        return carry

    lax.fori_loop(0, (valid + c_sz - 1) // c_sz, chunk, 0, unroll=GLA_UNROLL)

    o = o_scr[...]
    r = gr_ref[0]
    for h in range(n_heads):
        hs = slice(h * LANES, (h + 1) * LANES)
        oh = o[:, hs]
        oh = oh * lax.rsqrt(jnp.mean(oh * oh, axis=-1, keepdims=True) + EPS)
        rh = r[:, hs]
        o_ref[0, :, hs] = ((oh * gg_ref[0, :, hs]) * (rh * jax.nn.sigmoid(rh))).astype(BF16)

    @pl.when(j == pl.num_programs(1) - 1)
    def _():
        stf_ref[0] = st_scr[...]


def _gla(gq, gk, gv, gr, lr, wa, ba, gg, st0, layer, *, tt, valid):
    b, t, hp = gq.shape
    n_heads = hp // LANES
    lay = lambda bi, j: (layer, 0, 0)
    tile = lambda bi, j: (bi, j, 0)
    per_b = lambda bi, j: (bi, 0, 0, 0)
    body = functools.partial(_gla_kernel, tt=tt, valid=valid, n_heads=n_heads)
    return pl.pallas_call(
        body,
        grid=(b, t // tt),
        in_specs=[pl.BlockSpec((1, tt, hp), tile)] * 4 + [
            pl.BlockSpec((1, tt, LANES), tile),
            pl.BlockSpec((1, LANES, hp), lay),
            pl.BlockSpec((1, 1, hp), lay),
            pl.BlockSpec((1, 1, hp), lay),
            pl.BlockSpec((1, n_heads, LANES, LANES), per_b),
        ],
        out_specs=[pl.BlockSpec((1, tt, hp), tile), pl.BlockSpec((1, n_heads, LANES, LANES), per_b)],
        out_shape=[jax.ShapeDtypeStruct((b, t, hp), BF16),
                   jax.ShapeDtypeStruct((b, n_heads, LANES, LANES), F32)],
        scratch_shapes=[
            pltpu.VMEM((n_heads, LANES, LANES), F32),
            pltpu.VMEM((tt, hp), F32),
            pltpu.VMEM((tt // GLA_CHUNK, hp), F32),
            pltpu.VMEM((tt, hp), BF16),
            pltpu.VMEM((tt, hp), BF16),
            pltpu.VMEM((tt, hp), F32),
        ],
        compiler_params=_params("arbitrary", "arbitrary"),
        name="gla",
    )(gq, gk, gv, gr, lr, wa, ba, gg, st0)


def _merge_mlp_kernel(x_ref, ya_ref, ob_ref, oc_ref, gate_ref, gt1_ref, sc2_ref, sh2_ref, gt2_ref,
                      g2_ref, gf_ref, wa_ref, wb_ref, wc_ref, wo_ref, wup_ref, wdn_ref, *outs, final):
    d = x_ref.shape[-1]
    m = (gate_ref[:, 0:d] * _dot(ya_ref[...], wa_ref[0])
         + gate_ref[:, d:2 * d] * _dot(ob_ref[...], wb_ref[0])
         + gate_ref[:, 2 * d:3 * d] * _dot(oc_ref[...], wc_ref[0]))
    x1 = x_ref[...] + gt1_ref[0, 0, 0] * _dot(m.astype(BF16), wo_ref[0])
    h2 = _rms_rows(x1, g2_ref[0]) * (1.0 + sc2_ref[0, 0, 0]) + sh2_ref[0, 0, 0]
    up = jnp.maximum(_dot(h2.astype(BF16), wup_ref[0]), 0.0)
    x2 = x1 + gt2_ref[0, 0, 0] * _dot((up * up).astype(BF16), wdn_ref[0])
    outs[0][...] = x2
    if final:
        outs[1][...] = _rms_rows(x2, gf_ref[...])


def _merge_mlp(x, ya, ob, oc, gates, mods, g2, gf, wa, wb, wc, wo, wup, wdn, layer,
               *, tm, tiles_per_group, final):
    n, d = x.shape
    c = ya.shape[-1]
    dff = wup.shape[-1]
    row = lambda i: (i, 0)
    lay = lambda i: (layer, 0, 0)
    n_out = 2 if final else 1
    return pl.pallas_call(
        functools.partial(_merge_mlp_kernel, final=final),
        grid=(n // tm,),
        in_specs=[pl.BlockSpec((tm, d), row)] + [pl.BlockSpec((tm, c), row)] * 3 + [
            pl.BlockSpec((tm, 3 * d), row),
            _mod_spec(mods, 2, layer, tiles_per_group), _mod_spec(mods, 4, layer, tiles_per_group),
            _mod_spec(mods, 3, layer, tiles_per_group), _mod_spec(mods, 5, layer, tiles_per_group),
            pl.BlockSpec((1, 1, d), lay),
            pl.BlockSpec((1, d), lambda i: (0, 0)),
            _resident((1, c, d), lay), _resident((1, c, d), lay), _resident((1, c, d), lay),
            _resident((1, d, d), lay), _resident((1, d, dff), lay), _resident((1, dff, d), lay),
        ],
        out_specs=[pl.BlockSpec((tm, d), row)] * n_out,
        out_shape=[jax.ShapeDtypeStruct((n, d), F32)] * n_out,
        compiler_params=_params("arbitrary"),
        name="merge_mlp",
    )(x, ya, ob, oc, gates, mods, mods, mods, mods, g2, gf, wa, wb, wc, wo, wup, wdn)


def _t5_bucket_table(n_buckets, max_dist):
    dist = np.arange(max_dist + 1)
    max_exact = n_buckets // 2
    d = np.maximum(dist, 1).astype(np.float32)
    large = max_exact + (np.log(d / max_exact) / math.log(MAX_DISTANCE / max_exact)
                         * (n_buckets - max_exact)).astype(np.int32)
    large = np.minimum(large, n_buckets - 1)
    return np.where(dist < max_exact, dist, large).astype(np.int32)


def _pad_heads(w, n_heads):
    k = w.shape[-1] // n_heads
    w = w.reshape(w.shape[:-1] + (n_heads, k))
    w = jnp.pad(w, [(0, 0)] * (w.ndim - 1) + [(0, LANES - k)])
    return w.reshape(w.shape[:-2] + (n_heads * LANES,))


def kernel(x_prompt, x_sample, c_prompt, c_sample, cache_k, cache_v, page_table, state_conv, state_gla,
           w_ada, b_ada, g_norm1, w_in, w_dw, b_dw, ln_g, ln_b, w_pw2, w_pb, rel_bias, w_a2, b_a, g_gla,
           w_pc, w_o, g_norm2, w_up, w_down, g_final):
    bp, t, d = x_prompt.shape
    bs, ts, _ = x_sample.shape
    depth = w_in.shape[0]
    _, n_pool, page, h_b, hd_b = cache_k.shape
    d_b = h_b * hd_b
    d_conv = w_dw.shape[-1]
    width = w_dw.shape[1]
    rank = w_a2.shape[1]
    h_c = GLA_HEADS
    dk_c = w_a2.shape[-1] // h_c
    dv_c = state_gla.shape[-1]
    hp = h_c * LANES
    n_buckets = rel_bias.shape[0]
    n_pages = page_table.shape[1]
    past_len = n_pages * page
    assert dv_c == LANES and h_c * dv_c == hp and width - 1 <= CONV_HALO

    sizes = (2 * d_conv, d_b, d_b, d_b, h_c * dk_c, h_c * dk_c, h_c * dv_c, h_c * dv_c, rank, d, d, d)
    offs = np.cumsum((0,) + sizes)
    part = [w_in[:, :, offs[i]:offs[i + 1]] for i in range(len(sizes))]
    w_cat = jnp.concatenate([
        part[0], part[1], part[2], part[3],
        _pad_heads(part[4], h_c), _pad_heads(part[5], h_c), part[6], part[7],
        jnp.pad(part[8], ((0, 0), (0, 0), (0, LANES - rank))),
        part[9], part[10], part[11]], axis=-1).astype(BF16)
    wa2 = jnp.pad(_pad_heads(w_a2, h_c), ((0, 0), (0, LANES - rank), (0, 0))).astype(BF16)
    ba2 = _pad_heads(b_a, h_c).reshape(depth, 1, hp)
    gg = g_gla.reshape(depth, 1, hp)
    wpw2, wpb, wpc, wo, wup, wdn = (w.astype(BF16) for w in (w_pw2, w_pb, w_pc, w_o, w_up, w_down))
    g1 = g_norm1.reshape(depth, 1, d)
    g2 = g_norm2.reshape(depth, 1, d)
    gf = g_final.reshape(1, d)
    bdw = b_dw.reshape(depth, 1, d_conv)
    lng = ln_g.reshape(depth, 1, d_conv)
    lnb = ln_b.reshape(depth, 1, d_conv)

    n_c = bp + bs
    rows_c = -(-n_c // 8) * 8
    c_all = jnp.pad(jnp.concatenate([c_prompt, c_sample], axis=0), ((0, rows_c - n_c), (0, 0)))
    mods = _adaln(c_all, w_ada, b_ada).reshape(depth, rows_c, 6, d)

    blk = MOBA_BLOCK
    assert ts <= blk and past_len % blk == 0
    bucket = _t5_bucket_table(n_buckets, 2 * blk - 1)
    onehot = np.zeros((4 * blk, n_buckets), np.float32)
    onehot[np.arange(2 * blk), bucket] = 1.0
    by_dist = jnp.dot(onehot, rel_bias - rel_bias[n_buckets - 1:n_buckets, :],
                      precision=lax.Precision.HIGHEST).T
    toep = jnp.tile(by_dist, (1, blk))[:, :blk * (4 * blk - 1)].reshape(h_b, blk, 4 * blk - 1)
    causal = np.arange(blk)[:, None] <= np.arange(blk)[None, :]
    bias_t = jnp.stack([jnp.where(causal, toep[:, :, :blk] * LOG2E, NEG), toep[:, :, blk:2 * blk] * LOG2E],
                       axis=1)
    rows_s = ts * h_b
    own = jnp.transpose(toep[:, :ts, :ts], (2, 0, 1)).reshape(rows_s, ts)
    own_ok = np.repeat(np.arange(ts), h_b)[:, None] >= np.arange(LANES)[None, :]
    bias_own = jnp.where(own_ok, jnp.pad(own, ((0, 0), (0, LANES - ts))), NEG)
    bias_tail = jnp.transpose(toep[:, :, blk:blk + ts], (2, 0, 1)).reshape(rows_s, blk)
    assert past_len // blk <= LANES
    cache_kt = jnp.transpose(cache_k, (0, 1, 3, 4, 2)).reshape(depth, n_pool, d_b, page)
    cache_vt = jnp.transpose(cache_v, (0, 1, 3, 4, 2)).reshape(depth, n_pool, d_b, page)

    xp = x_prompt.reshape(bp * t, d)
    xs = x_sample.reshape(bs * ts, d)
    n_s = bs * ts
    tm = TOKEN_TILE
    tpb = t // tm
    dims = (d_conv, d_b, hp, hd_b, dk_c)
    ts_pad = 8
    gla_rows_s = GLA_CHUNK

    cp_l, sp_l, cs_l, ss_l = ([] for _ in range(4))
    y_prompt = y_sample = None
    zeros_hist = jnp.zeros((bp, CONV_HALO, d_conv), F32)
    zeros_state = jnp.zeros((bp, h_c, LANES, LANES), F32)
    mods = jnp.transpose(mods, (0, 2, 1, 3))
    mods_p = mods[:, :, :bp].reshape(depth, 6, bp, 1, d)
    mods_s = jnp.repeat(mods[:, :, bp:n_c], ts, axis=2).reshape(depth, 6, 1, n_s, d)
    kv_p = kv_s = None
    for l in range(depth):
        final = l == depth - 1

        (u, qb, kb, vb, k16, vt, km, gq, gk, gv, gr, lr, gates) = _inproj(
            xp, mods_p, g1, w_cat, kv_p, l, tm=tm, tiles_per_group=tpb, dims=dims, attn_aux=True)
        kv_p = (kb, vb)
        ya, tail = _conv_prompt(u.reshape(bp, t, d_conv), zeros_hist, w_dw, bdw, lng, lnb, l, tt=tm)
        ob = _moba_prompt(qb, k16, vt, km, bias_t, batch=bp, n_heads=h_b, hd=hd_b)
        oc, stf = _gla(*(a.reshape(bp, t, -1) for a in (gq, gk, gv, gr, lr)), wa2, ba2, gg, zeros_state, l,
                       tt=tm, valid=tm)
        res = _merge_mlp(xp, ya.reshape(bp * t, d_conv), ob.reshape(bp * t, d_b), oc.reshape(bp * t, hp),
                         gates, mods_p, g2, gf, wpw2, wpb, wpc, wo, wup, wdn, l,
                         tm=tm, tiles_per_group=tpb, final=final)
        xp = res[0]
        if final:
            y_prompt = res[1].reshape(bp, t, d)
        cp_l.append(tail[:, CONV_HALO - (width - 1):])
        sp_l.append(jnp.swapaxes(stf[..., :dk_c], -1, -2))

        (u, qb, kb, vb, gq, gk, gv, gr, lr, gates) = _inproj(
            xs, mods_s, g1, w_cat, kv_s, l, tm=n_s, tiles_per_group=1, dims=dims, attn_aux=False)
        kv_s = (kb, vb)
        kb, vb = kb[l], vb[l]
        ext = jnp.concatenate([state_conv[l], u.reshape(bs, ts, d_conv)], axis=1)
        ext_p = jnp.pad(ext, ((0, 0), (0, ts_pad + width - 1 - ext.shape[1] + (-(ts_pad + width - 1)) % 8), (0, 0)))
        ya = _conv_sample(ext_p, w_dw, bdw, lng, lnb, l, rows=ts_pad)[:, :ts]
        q_rep = jnp.repeat(qb.reshape(bs, ts, d_b), h_b, axis=1)
        pad_new = lambda a: jnp.pad(a.reshape(bs, ts, d_b), ((0, 0), (0, ts_pad - ts), (0, 0)))
        ob = _moba_sample(page_table, q_rep, pad_new(kb), pad_new(vb), bias_own, bias_tail,
                          cache_kt, cache_vt, l, n_heads=h_b, t_new=ts)
        pad_t = lambda a: jnp.pad(a.reshape(bs, ts, -1), ((0, 0), (0, gla_rows_s - ts), (0, 0)))
        st0 = jnp.pad(jnp.swapaxes(state_gla[l], -1, -2), ((0, 0), (0, 0), (0, 0), (0, LANES - dk_c)))
        oc, stf = _gla(pad_t(gq), pad_t(gk), pad_t(gv), pad_t(gr), pad_t(lr), wa2, ba2, gg, st0, l,
                       tt=gla_rows_s, valid=ts)
        res = _merge_mlp(xs, ya.reshape(n_s, d_conv), ob.reshape(n_s, d_b), oc[:, :ts].reshape(n_s, hp),
                         gates, mods_s, g2, gf, wpw2, wpb, wpc, wo, wup, wdn, l,
                         tm=n_s, tiles_per_group=1, final=final)
        xs = res[0]
        if final:
            y_sample = res[1].reshape(bs, ts, d)
        cs_l.append(ext[:, -(width - 1):])
        ss_l.append(jnp.swapaxes(stf[..., :dk_c], -1, -2))

    rows_p = lambda a: jnp.transpose(a.reshape(depth, bp, h_b, hd_b, t), (0, 1, 4, 2, 3))
    rows_s_ = lambda a: a.reshape(depth, bs, ts, h_b, hd_b)
    return (y_prompt, y_sample, rows_p(kv_p[0]), rows_p(kv_p[1]), jnp.stack(cp_l), jnp.stack(sp_l),
            rows_s_(kv_s[0]), rows_s_(kv_s[1]), jnp.stack(cs_l), jnp.stack(ss_l))
```

```python
import functools
import math

import numpy as np
import jax
import jax.numpy as jnp
from jax import lax
from jax.experimental import pallas as pl
from jax.experimental.pallas import tpu as pltpu

F32 = jnp.float32
BF16 = jnp.bfloat16

MOBA_BLOCK = 256
MOBA_TOPK = 3
MAX_DISTANCE = 128
GLA_HEADS = 4
GLA_CHUNK = 16
GATE_NORM = 16.0
EPS = 1e-6

LANES = 128
SUBLANES = 8
VMEM_LIMIT = 56 * 1024 * 1024

NEG = -1e30
LOG2E = math.log2(math.e)
V_ROW_PAD = 16
TOKEN_TILE = 256
PAGES_PER_CHUNK = 32
CONV_HALO = 32

_NT = (((1,), (1,)), ((), ()))


def _dot(a, b):
    return jnp.dot(a, b, preferred_element_type=F32)


def _dot_nt(a, b):
    return lax.dot_general(a, b, _NT, preferred_element_type=F32)


def _dot_f32(a, b):
    return jnp.dot(a, b, precision=lax.Precision.HIGHEST, preferred_element_type=F32)


def _dot_nt_f32(a, b):
    return lax.dot_general(a, b, _NT, precision=lax.Precision.HIGHEST, preferred_element_type=F32)


def _params(*sem):
    return pltpu.CompilerParams(dimension_semantics=sem, vmem_limit_bytes=VMEM_LIMIT)


def _resident(shape, index_map):
    return pl.BlockSpec(shape, index_map, pipeline_mode=pl.Buffered(1))


def _rms_rows(x, g):
    return (x * lax.rsqrt(jnp.mean(x * x, axis=-1, keepdims=True) + EPS)) * g


def _log_sigmoid(x):
    return jnp.minimum(x, 0.0) - jnp.log(1.0 + jnp.exp(-jnp.abs(x)))


def _top_k_mask(s, idx, axis, k, sentinel):
    sel = jnp.zeros(s.shape, dtype=jnp.bool_)
    for _ in range(k):
        mx = jnp.max(s, axis=axis, keepdims=True)
        first = jnp.min(jnp.where(s == mx, idx, sentinel), axis=axis, keepdims=True)
        pick = (idx == first) & (mx > -jnp.inf)
        sel = sel | pick
        s = jnp.where(pick, -jnp.inf, s)
    return sel


def _adaln_kernel(c_ref, w_ref, b_ref, o_ref):
    c = c_ref[...]
    s = c * jax.nn.sigmoid(c)
    o_ref[0] = _dot(s.astype(BF16), w_ref[0].astype(BF16)) + b_ref[0]


def _adaln(c_all, w_ada, b_ada):
    depth, d, n = w_ada.shape
    rows = c_all.shape[0]
    tn = 1536
    return pl.pallas_call(
        _adaln_kernel,
        grid=(depth, n // tn),
        in_specs=[
            pl.BlockSpec((rows, d), lambda l, j: (0, 0)),
            pl.BlockSpec((1, d, tn), lambda l, j: (l, 0, j)),
            pl.BlockSpec((1, 1, tn), lambda l, j: (l, 0, j)),
        ],
        out_specs=pl.BlockSpec((1, rows, tn), lambda l, j: (l, 0, j)),
        out_shape=jax.ShapeDtypeStruct((depth, rows, n), F32),
        compiler_params=_params("arbitrary", "arbitrary"),
        name="adaln",
    )(c_all, w_ada, b_ada.reshape(depth, 1, n))


def _inproj_kernel(x_ref, sc_ref, sh_ref, g_ref, w_ref, *rest, d_conv, d_b, hp, d_model,
                   q_scale, gq_scale, n_heads_b, attn_aux, n_alias, kv_layer):
    outs = rest[n_alias:]
    if attn_aux:
        (u_ref, q_ref, k_ref, v_ref, k16_ref, vt_ref, km_ref,
         gq_ref, gk_ref, gv_ref, gr_ref, lr_ref, gate_ref) = outs
    else:
        (u_ref, q_ref, k_ref, v_ref, gq_ref, gk_ref, gv_ref, gr_ref, lr_ref, gate_ref) = outs
    x = x_ref[...]
    h = _rms_rows(x, g_ref[0]) * (1.0 + sc_ref[0, 0, 0]) + sh_ref[0, 0, 0]
    hb = h.astype(BF16)

    pos = [0]

    def seg(width):
        lo = pos[0]
        pos[0] = lo + width
        return _dot(hb, w_ref[0, :, lo:lo + width])

    a = seg(d_conv)
    g = seg(d_conv)
    u_ref[...] = a * jax.nn.sigmoid(g)
    q_ref[...] = seg(d_b) * q_scale
    k = seg(d_b)
    v = seg(d_b)
    def put(ref, val):
        if n_alias:
            ref[0] = val
        else:
            for l in range(ref.shape[0]):
                ref[l] = val if l == kv_layer else jnp.zeros_like(val)

    if not attn_aux:
        put(k_ref, k)
        put(v_ref, v)
    else:
        v_t = v.T
        put(k_ref, k.T[None])
        put(v_ref, v_t[None])
        k16_ref[0] = k.astype(BF16)
        vt = v_t.astype(BF16)
        hd = d_b // n_heads_b
        tail = jnp.where(lax.broadcasted_iota(jnp.int32, (V_ROW_PAD, vt.shape[1]), 0) == 0, 1.0, 0.0).astype(BF16)
        for h in range(n_heads_b):
            vt_ref[0, h * (hd + V_ROW_PAD):h * (hd + V_ROW_PAD) + hd, :] = vt[h * hd:(h + 1) * hd, :]
            vt_ref[0, h * (hd + V_ROW_PAD) + hd:(h + 1) * (hd + V_ROW_PAD), :] = tail
        km_ref[0] = jnp.mean(k, axis=0, keepdims=True)
    gq_ref[...] = seg(hp) * gq_scale
    gk_ref[...] = seg(hp)
    gv_ref[...] = seg(hp)
    gr_ref[...] = seg(hp)
    lr_ref[...] = seg(LANES)
    for i in range(3):
        gate_ref[:, i * d_model:(i + 1) * d_model] = jax.nn.sigmoid(seg(d_model))


def _mod_spec(mods, idx, layer, tiles_per_group):
    return pl.BlockSpec((1, 1, 1) + mods.shape[3:], lambda i: (layer, idx, i // tiles_per_group, 0, 0))


def _inproj(x, mods, g1, w_cat, kv_bufs, layer, *, tm, tiles_per_group, dims, attn_aux):
    n, d = x.shape
    d_conv, d_b, hp, hd_b, dk_c = dims
    depth, _, nw = w_cat.shape
    grid = (n // tm,)
    row = lambda i: (i, 0)
    lay = lambda i: (layer, 0, 0)
    lead, l0 = (depth, 0) if kv_bufs is None else (1, layer)
    if attn_aux:
        tpg = tiles_per_group
        kv_shape = (depth, n // (tpg * tm), d_b, tpg * tm)
        kv_spec = pl.BlockSpec((lead, 1, d_b, tm), lambda i: (l0, i // tpg, 0, i % tpg))
    else:
        kv_shape, kv_spec = (depth, n, d_b), pl.BlockSpec((lead, tm, d_b), lambda i: (l0, i, 0))
    outs = [((n, d_conv), F32), ((n, d_b), F32), (kv_shape, F32), (kv_shape, F32)]
    specs = [pl.BlockSpec((tm, d_conv), row), pl.BlockSpec((tm, d_b), row), kv_spec, kv_spec]
    if attn_aux:
        assert tm == MOBA_BLOCK
        nblk = n // MOBA_BLOCK
        vt_rows = d_b + (d_b // hd_b) * V_ROW_PAD
        outs += [((nblk, MOBA_BLOCK, d_b), BF16), ((nblk, vt_rows, MOBA_BLOCK), BF16), ((nblk, 1, d_b), F32)]
        specs += [pl.BlockSpec((1, MOBA_BLOCK, d_b), lambda i: (i, 0, 0)),
                  pl.BlockSpec((1, vt_rows, MOBA_BLOCK), lambda i: (i, 0, 0)),
                  pl.BlockSpec((1, 1, d_b), lambda i: (i, 0, 0))]
    outs += [((n, hp), F32)] * 4 + [((n, LANES), F32), ((n, 3 * d), F32)]
    specs += [pl.BlockSpec((tm, hp), row)] * 4 + [pl.BlockSpec((tm, LANES), row), pl.BlockSpec((tm, 3 * d), row)]
    body = functools.partial(_inproj_kernel, d_conv=d_conv, d_b=d_b, hp=hp, d_model=d,
                             q_scale=hd_b ** -0.5, gq_scale=dk_c ** -0.5, n_heads_b=d_b // hd_b,
                             attn_aux=attn_aux, n_alias=0 if kv_bufs is None else 2, kv_layer=layer)
    in_specs = [
        pl.BlockSpec((tm, d), row),
        _mod_spec(mods, 1, layer, tiles_per_group),
        _mod_spec(mods, 0, layer, tiles_per_group),
        pl.BlockSpec((1, 1, d), lay),
        _resident((1, d, nw), lay),
    ]
    args = [x, mods, mods, g1, w_cat]
    aliases = {}
    if kv_bufs is not None:
        in_specs += [pl.BlockSpec(memory_space=pl.ANY)] * 2
        aliases = {len(args): 2, len(args) + 1: 3}
        args += list(kv_bufs)
    return pl.pallas_call(
        body,
        grid=grid,
        in_specs=in_specs,
        out_specs=specs,
        out_shape=[jax.ShapeDtypeStruct(s, t) for s, t in outs],
        input_output_aliases=aliases,
        compiler_params=_params("arbitrary"),
        name="inproj",
    )(*args)


def _conv_post(acc, lng_ref, lnb_ref):
    mu = jnp.mean(acc, axis=-1, keepdims=True)
    cen = acc - mu
    var = jnp.mean(cen * cen, axis=-1, keepdims=True)
    y = (cen * lax.rsqrt(var + EPS)) * lng_ref[0] + lnb_ref[0]
    return y * jax.nn.sigmoid(y)


def _conv_taps(ext_ref, start, rows, wdw_ref, bdw_ref, lng_ref, lnb_ref, width):
    acc = jnp.zeros((rows, wdw_ref.shape[-1]), F32) + bdw_ref[0]
    for k in range(width):
        acc = acc + ext_ref[pl.ds(start + k, rows), :] * wdw_ref[0, k:k + 1, :]
    return _conv_post(acc, lng_ref, lnb_ref)


def _conv_taps_aligned(ext_ref, start, rows, wdw_ref, bdw_ref, lng_ref, lnb_ref, width, shift_scr, out_ref):
    span = shift_scr.shape[1]
    for r in range(1, SUBLANES):
        shift_scr[r - 1] = ext_ref[pl.ds(r, span), :]
    acc = jnp.zeros((rows, wdw_ref.shape[-1]), F32) + bdw_ref[0]
    for k in range(width):
        a, r = divmod(start + k, SUBLANES)
        src = ext_ref if r == 0 else shift_scr.at[r - 1]
        acc = acc + src[pl.ds(a * SUBLANES, rows), :] * wdw_ref[0, k:k + 1, :]
    out_ref[...] = _conv_post(acc, lng_ref, lnb_ref).astype(out_ref.dtype)


def _conv_kernel(u_ref, hist_ref, wdw_ref, bdw_ref, lng_ref, lnb_ref, y_ref, tail_ref, ext_scr, shift_scr,
                 *, tt, width):
    j = pl.program_id(1)

    @pl.when(j == 0)
    def _():
        ext_scr[0:CONV_HALO, :] = hist_ref[0]

    @pl.when(j > 0)
    def _():
        ext_scr[0:CONV_HALO, :] = ext_scr[tt:tt + CONV_HALO, :]

    ext_scr[CONV_HALO:CONV_HALO + tt, :] = u_ref[0]
    start = CONV_HALO - (width - 1)
    _conv_taps_aligned(ext_scr, start, tt, wdw_ref, bdw_ref, lng_ref, lnb_ref, width, shift_scr, y_ref.at[0])

    @pl.when(j == pl.num_programs(1) - 1)
    def _():
        tail_ref[0] = ext_scr[tt:tt + CONV_HALO, :]


def _conv_prompt(u, hist, w_dw, b_dw, ln_g, ln_b, layer, *, tt):
    b, t, c = u.shape
    width = w_dw.shape[1]
    lay = lambda bi, j: (layer, 0, 0)
    return pl.pallas_call(
        functools.partial(_conv_kernel, tt=tt, width=width),
        grid=(b, t // tt),
        in_specs=[
            pl.BlockSpec((1, tt, c), lambda bi, j: (bi, j, 0)),
            pl.BlockSpec((1, CONV_HALO, c), lambda bi, j: (bi, 0, 0)),
            pl.BlockSpec((1, width, c), lay),
            pl.BlockSpec((1, 1, c), lay),
            pl.BlockSpec((1, 1, c), lay),
            pl.BlockSpec((1, 1, c), lay),
        ],
        out_specs=[pl.BlockSpec((1, tt, c), lambda bi, j: (bi, j, 0)),
                   pl.BlockSpec((1, CONV_HALO, c), lambda bi, j: (bi, 0, 0))],
        out_shape=[jax.ShapeDtypeStruct((b, t, c), BF16), jax.ShapeDtypeStruct((b, CONV_HALO, c), F32)],
        scratch_shapes=[pltpu.VMEM((CONV_HALO + tt, c), F32),
                        pltpu.VMEM((SUBLANES - 1, CONV_HALO + tt - SUBLANES, c), F32)],
        compiler_params=_params("arbitrary", "arbitrary"),
        name="conv_prompt",
    )(u, hist, w_dw, b_dw, ln_g, ln_b)


def _conv_small_kernel(ext_ref, wdw_ref, bdw_ref, lng_ref, lnb_ref, y_ref, *, rows, width):
    y_ref[0] = _conv_taps(ext_ref.at[0], 0, rows, wdw_ref, bdw_ref, lng_ref, lnb_ref, width).astype(BF16)


def _conv_sample(ext, w_dw, b_dw, ln_g, ln_b, layer, *, rows):
    b, r, c = ext.shape
    width = w_dw.shape[1]
    lay = lambda bi: (layer, 0, 0)
    return pl.pallas_call(
        functools.partial(_conv_small_kernel, rows=rows, width=width),
        grid=(b,),
        in_specs=[
            pl.BlockSpec((1, r, c), lambda bi: (bi, 0, 0)),
            pl.BlockSpec((1, width, c), lay),
            pl.BlockSpec((1, 1, c), lay),
            pl.BlockSpec((1, 1, c), lay),
            pl.BlockSpec((1, 1, c), lay),
        ],
        out_specs=pl.BlockSpec((1, rows, c), lambda bi: (bi, 0, 0)),
        out_shape=jax.ShapeDtypeStruct((b, rows, c), BF16),
        compiler_params=_params("arbitrary"),
        name="conv_sample",
    )(ext, w_dw, b_dw, ln_g, ln_b)


def _moba_prompt_kernel(q_ref, k16_ref, vt_ref, km_ref, bias_ref, o_ref,
                        qm_scr, sel_scr, m_scr, acc_scr, s2_scr, *, nb, n_heads, hd):
    i = pl.program_id(1)
    blk = MOBA_BLOCK
    q = q_ref[0]
    km = km_ref[0]
    lane = lax.broadcasted_iota(jnp.int32, (blk, LANES), 1)
    blk_idx = lax.broadcasted_iota(jnp.int32, (nb, blk), 0)
    heads_per_slab = LANES // hd

    km16 = km.astype(BF16)
    for h in range(n_heads):
        p, w = divmod(h, heads_per_slab)
        qm = jnp.where((lane // hd) == w, q[:, p * LANES:(p + 1) * LANES], 0.0)
        qm_scr[h] = (qm * LOG2E).astype(BF16)
        s = _dot_nt(km16[:, p * LANES:(p + 1) * LANES], qm.astype(BF16))
        s = jnp.where(blk_idx < i, s, -jnp.inf)
        sel = _top_k_mask(s, blk_idx, 0, MOBA_TOPK, nb)
        sel_scr[h] = jnp.where(sel, 1.0, 0.0)

    hv = vt_ref.shape[1] // n_heads

    def block(n, kind):
        kb = k16_ref[n]
        vb = vt_ref[n]

        for h in range(n_heads):
            p = h // heads_per_slab
            s2_scr[h, 0:blk, :] =_dot_nt(kb[:, p * LANES:(p + 1) * LANES], qm_scr[h])
        for h in range(n_heads):
            s = s2_scr[h, 0:blk, :]
            rows = slice(h * hv, (h + 1) * hv)
            if kind == "own":
                s = s + bias_ref[h, 0]
                m = jnp.max(s, axis=0, keepdims=True)
                m_scr[h:h + 1, :] = m
                acc_scr[rows, :] = _dot(vb[rows, :], jnp.exp2(s - m).astype(BF16))
            else:
                picked = sel_scr[h, pl.ds(n, 1), :] > 0.5
                m_old = m_scr[h:h + 1, :]
                m = jnp.where(picked, jnp.maximum(m_old, jnp.max(s, axis=0, keepdims=True)), m_old)
                alpha = jnp.exp2(m_old - m)
                pr = jnp.exp2(s - jnp.where(picked, m, -NEG))
                m_scr[h:h + 1, :] = m
                acc_scr[rows, :] = alpha * acc_scr[rows, :] + _dot(vb[rows, :], pr.astype(BF16))

    def pair_logits(n, buf):
        kb = k16_ref[pl.ds(n, 2)].reshape(2 * blk, k16_ref.shape[-1])
        for h in range(n_heads):
            p = h // heads_per_slab
            buf[h] = _dot_nt(kb[:, p * LANES:(p + 1) * LANES], qm_scr[h])

    @pl.when(i == 0)
    def _():
        block(0, "own")

    @pl.when(i >= 1)
    def _():
        n = i - 1
        pair_logits(n, s2_scr)
        vb = jnp.concatenate([vt_ref[n], vt_ref[n + 1]], axis=1)
        for h in range(n_heads):
            rows = slice(h * hv, (h + 1) * hv)
            s_a = s2_scr[h, 0:blk, :] + bias_ref[h, 1]
            s_b = s2_scr[h, blk:2 * blk, :] + bias_ref[h, 0]
            pick_a = sel_scr[h, pl.ds(n, 1), :] > 0.5
            m = jnp.maximum(jnp.where(pick_a, jnp.max(s_a, axis=0, keepdims=True), NEG),
                            jnp.max(s_b, axis=0, keepdims=True))
            pr = jnp.concatenate([jnp.exp2(s_a - jnp.where(pick_a, m, -NEG)).astype(BF16),
                                  jnp.exp2(s_b - m).astype(BF16)], axis=0)
            m_scr[h:h + 1, :] = m
            acc_scr[rows, :] = _dot(vb[rows, :], pr)

    def pair_update(n, buf):
        vb = jnp.concatenate([vt_ref[n], vt_ref[n + 1]], axis=1)
        for h in range(n_heads):
            rows = slice(h * hv, (h + 1) * hv)
            s_a = buf[h, 0:blk, :]
            s_b = buf[h, blk:2 * blk, :]
            pick_a = sel_scr[h, pl.ds(n, 1), :] > 0.5
            pick_b = sel_scr[h, pl.ds(n + 1, 1), :] > 0.5
            m_old = m_scr[h:h + 1, :]
            m = jnp.maximum(m_old, jnp.maximum(
                jnp.where(pick_a, jnp.max(s_a, axis=0, keepdims=True), NEG),
                jnp.where(pick_b, jnp.max(s_b, axis=0, keepdims=True), NEG)))
            alpha = jnp.exp2(m_old - m)
            pr = jnp.concatenate([jnp.exp2(s_a - jnp.where(pick_a, m, -NEG)).astype(BF16),
                                  jnp.exp2(s_b - jnp.where(pick_b, m, -NEG)).astype(BF16)], axis=0)
            m_scr[h:h + 1, :] = m
            acc_scr[rows, :] = alpha * acc_scr[rows, :] + _dot(vb[rows, :], pr)

    n_far = jnp.maximum(i - 1, 0)

    def far(t, carry):
        pair_logits(2 * t, s2_scr)
        pair_update(2 * t, s2_scr)
        return carry

    lax.fori_loop(0, n_far // 2, far, 0)

    @pl.when(n_far % 2 == 1)
    def _():
        block(n_far - 1, "far")

    out = [acc_scr[h * hv:h * hv + hd, :] / acc_scr[h * hv + hd:h * hv + hd + 1, :] for h in range(n_heads)]
    o_ref[0] = jnp.concatenate(out, axis=0).T.astype(BF16)


def _moba_prompt(q, k16, vt, km, bias_t, *, batch, n_heads, hd):
    n, d_b = q.shape
    t = n // batch
    nb = t // MOBA_BLOCK
    blk = MOBA_BLOCK
    body = functools.partial(_moba_prompt_kernel, nb=nb, n_heads=n_heads, hd=hd)
    return pl.pallas_call(
        body,
        grid=(batch, nb),
        in_specs=[
            pl.BlockSpec((1, blk, d_b), lambda b, i: (b * nb + i, 0, 0)),
            pl.BlockSpec((nb, blk, d_b), lambda b, i: (b, 0, 0)),
            pl.BlockSpec((nb, vt.shape[1], blk), lambda b, i: (b, 0, 0)),
            pl.BlockSpec((1, nb, d_b), lambda b, i: (b, 0, 0)),
            _resident((n_heads, 2, blk, blk), lambda b, i: (0, 0, 0, 0)),
        ],
        out_specs=pl.BlockSpec((1, blk, d_b), lambda b, i: (b * nb + i, 0, 0)),
        out_shape=jax.ShapeDtypeStruct((n // blk, blk, d_b), BF16),
        scratch_shapes=[
            pltpu.VMEM((n_heads, blk, LANES), BF16),
            pltpu.VMEM((n_heads, nb, blk), F32),
            pltpu.VMEM((n_heads, blk), F32),
            pltpu.VMEM((vt.shape[1], blk), F32),
            pltpu.VMEM((n_heads, 2 * blk, blk), F32),
        ],
        compiler_params=_params("arbitrary", "arbitrary"),
        name="moba_prompt",
    )(q.reshape(n // blk, blk, d_b), k16, vt, km.reshape(batch, nb, d_b), bias_t)


def _moba_sample_kernel(pt_ref, q_ref, kn_ref, vn_ref, bown_ref, btail_ref, kt_hbm, vt_hbm, o_ref,
                        s_scr, kv_scr, kbuf, vbuf, ksem, vsem,
                        *, layer, n_chunks, n_heads, t_new, page, ppc):
    b = pl.program_id(0)
    rows, d_b = q_ref.shape[1:]
    hd = d_b // n_heads
    blocks_per_chunk = ppc * page // MOBA_BLOCK
    pages_per_block = MOBA_BLOCK // page
    nbp = n_chunks * blocks_per_chunk

    def page_copy(hbm, buf, sem, seq, chunk, p):
        slot = chunk % 2
        return pltpu.make_async_copy(hbm.at[layer, pt_ref[seq, chunk * ppc + p]], buf.at[slot, p], sem.at[slot])

    def start_chunk(hbm, buf, sem, seq, chunk):
        for p in range(ppc):
            page_copy(hbm, buf, sem, seq, chunk, p).start()

    def wait_chunk(hbm, buf, sem, seq, chunk):
        for p in range(ppc):
            page_copy(hbm, buf, sem, seq, chunk, p).wait()

    @pl.when(b == 0)
    def _():
        start_chunk(kt_hbm, kbuf, ksem, b, 0)

    row_i = lax.broadcasted_iota(jnp.int32, (rows, d_b), 0)
    lane_i = lax.broadcasted_iota(jnp.int32, (rows, d_b), 1)
    head_mask = (lane_i // hd) == (row_i % n_heads)
    blk_lane = lax.broadcasted_iota(jnp.int32, (rows, LANES), 1)
    qb16 = jnp.where(head_mask, q_ref[0], 0.0).astype(BF16)

    bsum = jnp.zeros((rows, LANES), F32)
    for c in range(n_chunks):
        if c + 1 < n_chunks:
            start_chunk(kt_hbm, kbuf, ksem, b, c + 1)
        else:
            start_chunk(vt_hbm, vbuf, vsem, b, 0)
        wait_chunk(kt_hbm, kbuf, ksem, b, c)
        for p in range(ppc):
            s = _dot(qb16, kbuf[c % 2, p].astype(BF16))
            s_scr[c, :, p * page:(p + 1) * page] = s
            ps = jnp.sum(s, axis=-1, keepdims=True)
            psum = ps if p % pages_per_block == 0 else psum + ps
            if p % pages_per_block == pages_per_block - 1:
                bsum = jnp.where(blk_lane == c * blocks_per_chunk + p // pages_per_block, psum, bsum)

    bs = jnp.where(blk_lane < nbp, bsum, -jnp.inf)
    sel = _top_k_mask(bs, blk_lane, 1, MOBA_TOPK, LANES)
    blk_bias = jnp.where(sel, 0.0, NEG)

    kv_scr[...] = jnp.zeros(kv_scr.shape, F32)
    kv_scr[0:kn_ref.shape[1], :] = kn_ref[0]
    s_own = _dot_nt(qb16, kv_scr[...].astype(BF16)) + bown_ref[...]
    m = jnp.max(s_own, axis=-1, keepdims=True)
    for blk in range(nbp):
        cc, off = divmod(blk, blocks_per_chunk)
        cols = slice(off * MOBA_BLOCK, (off + 1) * MOBA_BLOCK)
        s = s_scr[cc, :, cols] + blk_bias[:, blk:blk + 1]
        if blk == nbp - 1:
            s = s + btail_ref[...]
        s_scr[cc, :, cols] = s
        m = jnp.maximum(m, jnp.max(s, axis=-1, keepdims=True))

    p_own = jnp.exp(s_own - m)
    l = jnp.sum(p_own, axis=-1, keepdims=True)
    for cc in range(n_chunks):
        pr = jnp.exp(s_scr[cc] - m)
        l = l + jnp.sum(pr, axis=-1, keepdims=True)
        s_scr[cc] = pr
    kv_scr[0:vn_ref.shape[1], :] = vn_ref[0]
    acc = _dot(p_own.astype(BF16), kv_scr[...].astype(BF16))

    for c in range(n_chunks):
        if c + 1 < n_chunks:
            start_chunk(vt_hbm, vbuf, vsem, b, c + 1)
        else:
            @pl.when(b + 1 < pl.num_programs(0))
            def _():
                start_chunk(kt_hbm, kbuf, ksem, b + 1, 0)
        wait_chunk(vt_hbm, vbuf, vsem, b, c)
        for p in range(ppc):
            pr = s_scr[c, :, p * page:(p + 1) * page]
            acc = acc + _dot_nt(pr.astype(BF16), vbuf[c % 2, p].astype(BF16))

    o = jnp.where(head_mask, acc / l, 0.0)
    o_ref[0] = jnp.sum(o.reshape(t_new, n_heads, d_b), axis=1).astype(BF16)


def _moba_sample(page_table, q_rep, k_new, v_new, bias_own, bias_tail, cache_kt, cache_vt, layer,
                 *, n_heads, t_new):
    b, rows, d_b = q_rep.shape
    page = cache_kt.shape[-1]
    n_pages = page_table.shape[1]
    ppc = min(PAGES_PER_CHUNK, n_pages)
    n_chunks = n_pages // ppc
    assert n_pages % ppc == 0 and (ppc * page) % MOBA_BLOCK == 0 and MOBA_BLOCK % page == 0
    chunk = ppc * page

    per_b = lambda bi, pt: (bi, 0, 0)
    const2 = lambda bi, pt: (0, 0)
    in_specs = [
        pl.BlockSpec((1, rows, d_b), per_b),
        pl.BlockSpec((1, k_new.shape[1], d_b), per_b),
        pl.BlockSpec((1, v_new.shape[1], d_b), per_b),
        pl.BlockSpec(bias_own.shape, const2),
        pl.BlockSpec(bias_tail.shape, const2),
        pl.BlockSpec(memory_space=pl.ANY),
        pl.BlockSpec(memory_space=pl.ANY),
    ]
    body = functools.partial(_moba_sample_kernel, layer=layer, n_chunks=n_chunks, n_heads=n_heads,
                             t_new=t_new, page=page, ppc=ppc)
    grid_spec = pltpu.PrefetchScalarGridSpec(
        num_scalar_prefetch=1,
        grid=(b,),
        in_specs=in_specs,
        out_specs=pl.BlockSpec((1, t_new, d_b), per_b),
        scratch_shapes=[
            pltpu.VMEM((n_chunks, rows, chunk), F32),
            pltpu.VMEM((LANES, d_b), F32),
            pltpu.VMEM((2, ppc, d_b, page), F32),
            pltpu.VMEM((2, ppc, d_b, page), F32),
            pltpu.SemaphoreType.DMA((2,)),
            pltpu.SemaphoreType.DMA((2,)),
        ],
    )
    return pl.pallas_call(
        body,
        grid_spec=grid_spec,
        out_shape=jax.ShapeDtypeStruct((b, t_new, d_b), BF16),
        compiler_params=_params("arbitrary"),
        name="moba_sample",
    )(page_table, q_rep, k_new, v_new, bias_own, bias_tail, cache_kt, cache_vt)


def _gla_kernel(gq_ref, gk_ref, gv_ref, gr_ref, lr_ref, wa_ref, ba_ref, gg_ref, st0_ref,
                o_ref, stf_ref, st_scr, bc_scr, bl_scr, qd_scr, kd_scr, o_scr,
                *, tt, valid, n_heads):
    j = pl.program_id(1)
    c_sz = GLA_CHUNK
    hp = n_heads * LANES

    @pl.when(j == 0)
    def _():
        st_scr[...] = st0_ref[0]

    x = _dot(lr_ref[0].astype(BF16), wa_ref[0]) + ba_ref[0]
    la = _log_sigmoid(x) * (1.0 / GATE_NORM)
    if valid < tt:
        la = jnp.where(lax.broadcasted_iota(jnp.int32, (tt, hp), 0) < valid, la, 0.0)
    r_i = lax.broadcasted_iota(jnp.int32, (tt, tt), 0)
    c_i = lax.broadcasted_iota(jnp.int32, (tt, tt), 1)
    low = jnp.where(((r_i // c_sz) == (c_i // c_sz)) & (c_i <= r_i), 1.0, 0.0).astype(BF16)
    la_hi = la.astype(BF16)
    la_lo = (la - la_hi.astype(F32)).astype(BF16)
    bc = _dot(low, la_hi) + _dot(low, la_lo)
    bc_scr[...] = bc
    n_ch = tt // c_sz
    in_ch = (lax.broadcasted_iota(jnp.int32, (n_ch, tt), 1) // c_sz
             == lax.broadcasted_iota(jnp.int32, (n_ch, tt), 0))
    tot = jnp.where(in_ch, 1.0, 0.0).astype(BF16)
    ends = _dot(tot, la_hi) + _dot(tot, la_lo)
    bl_scr[...] = ends
    bl = jnp.broadcast_to(ends[:, None, :], (n_ch, c_sz, hp)).reshape(tt, hp)
    qd_scr[...] = (gq_ref[0] * jnp.exp(bc)).astype(BF16)
    kd_scr[...] = (gk_ref[0] * jnp.exp(bl - bc)).astype(BF16)

    half = c_sz // 2
    t_half = lax.broadcasted_iota(jnp.int32, (half, LANES), 0)

    def chunk(c, carry):
        rows = pl.ds(c * c_sz, c_sz)
        bc_c = bc_scr[rows, :]
        q_c = gq_ref[0, rows, :]
        k_c = gk_ref[0, rows, :]
        v_c = gv_ref[0, rows, :]
        decay = jnp.exp(bl_scr[pl.ds(c, 1), :])
        for h in range(n_heads):
            hs = slice(h * LANES, (h + 1) * LANES)
            st = st_scr[h]
            o_h = _dot_nt(qd_scr[rows, hs], st.astype(BF16))
            bch = bc_c[:, hs]
            qh = q_c[:, hs]
            parts = [o_h[0:half], o_h[half:c_sz]]
            for s in range(c_sz):
                row_s = pl.ds(c * c_sz + s, 1)
                b_s = jnp.broadcast_to(bc_scr[row_s, hs], (half, LANES))
                k_s = jnp.broadcast_to(gk_ref[0, row_s, hs], (half, LANES))
                v_s = jnp.broadcast_to(gv_ref[0, row_s, hs], (half, LANES))
                for hi in range(2):
                    r0 = hi * half
                    if s >= r0 + half:
                        continue
                    diff = bch[r0:r0 + half] - b_s
                    if s > r0:
                        diff = jnp.where(t_half + r0 >= s, diff, -jnp.inf)
                    a_s = jnp.sum(qh[r0:r0 + half] * k_s * jnp.exp(diff), axis=-1, keepdims=True)
                    parts[hi] = parts[hi] + a_s * v_s
            o_scr[rows, hs] = jnp.concatenate(parts, axis=0)
            kvt = lax.dot_general(v_c[:, hs].astype(BF16), kd_scr[rows, hs], (((0,), (0,)), ((), ())),
                                  preferred_element_type=F32)
            st_scr[h] = decay[:, hs] * st + kvt
        return carry

    for c in range((valid + c_sz - 1) // c_sz):
        chunk(c, 0)

    o = o_scr[...]
    r = gr_ref[0]
    for h in range(n_heads):
        hs = slice(h * LANES, (h + 1) * LANES)
        oh = o[:, hs]
        oh = oh * lax.rsqrt(jnp.mean(oh * oh, axis=-1, keepdims=True) + EPS)
        rh = r[:, hs]
        o_ref[0, :, hs] = ((oh * gg_ref[0, :, hs]) * (rh * jax.nn.sigmoid(rh))).astype(BF16)

    @pl.when(j == pl.num_programs(1) - 1)
    def _():
        stf_ref[0] = st_scr[...]


def _gla(gq, gk, gv, gr, lr, wa, ba, gg, st0, layer, *, tt, valid):
    b, t, hp = gq.shape
    n_heads = hp // LANES
    lay = lambda bi, j: (layer, 0, 0)
    tile = lambda bi, j: (bi, j, 0)
    per_b = lambda bi, j: (bi, 0, 0, 0)
    body = functools.partial(_gla_kernel, tt=tt, valid=valid, n_heads=n_heads)
    return pl.pallas_call(
        body,
        grid=(b, t // tt),
        in_specs=[pl.BlockSpec((1, tt, hp), tile)] * 4 + [
            pl.BlockSpec((1, tt, LANES), tile),
            pl.BlockSpec((1, LANES, hp), lay),
            pl.BlockSpec((1, 1, hp), lay),
            pl.BlockSpec((1, 1, hp), lay),
            pl.BlockSpec((1, n_heads, LANES, LANES), per_b),
        ],
        out_specs=[pl.BlockSpec((1, tt, hp), tile), pl.BlockSpec((1, n_heads, LANES, LANES), per_b)],
        out_shape=[jax.ShapeDtypeStruct((b, t, hp), BF16),
                   jax.ShapeDtypeStruct((b, n_heads, LANES, LANES), F32)],
        scratch_shapes=[
            pltpu.VMEM((n_heads, LANES, LANES), F32),
            pltpu.VMEM((tt, hp), F32),
            pltpu.VMEM((tt // GLA_CHUNK, hp), F32),
            pltpu.VMEM((tt, hp), BF16),
            pltpu.VMEM((tt, hp), BF16),
            pltpu.VMEM((tt, hp), F32),
        ],
        compiler_params=_params("arbitrary", "arbitrary"),
        name="gla",
    )(gq, gk, gv, gr, lr, wa, ba, gg, st0)


def _merge_mlp_kernel(x_ref, ya_ref, ob_ref, oc_ref, gate_ref, gt1_ref, sc2_ref, sh2_ref, gt2_ref,
                      g2_ref, gf_ref, wa_ref, wb_ref, wc_ref, wo_ref, wup_ref, wdn_ref, *outs, final):
    d = x_ref.shape[-1]
    m = (gate_ref[:, 0:d] * _dot(ya_ref[...], wa_ref[0])
         + gate_ref[:, d:2 * d] * _dot(ob_ref[...], wb_ref[0])
         + gate_ref[:, 2 * d:3 * d] * _dot(oc_ref[...], wc_ref[0]))
    x1 = x_ref[...] + gt1_ref[0, 0, 0] * _dot(m.astype(BF16), wo_ref[0])
    h2 = _rms_rows(x1, g2_ref[0]) * (1.0 + sc2_ref[0, 0, 0]) + sh2_ref[0, 0, 0]
    up = jnp.maximum(_dot(h2.astype(BF16), wup_ref[0]), 0.0)
    x2 = x1 + gt2_ref[0, 0, 0] * _dot((up * up).astype(BF16), wdn_ref[0])
    outs[0][...] = x2
    if final:
        outs[1][...] = _rms_rows(x2, gf_ref[...])


def _merge_mlp(x, ya, ob, oc, gates, mods, g2, gf, wa, wb, wc, wo, wup, wdn, layer,
               *, tm, tiles_per_group, final):
    n, d = x.shape
    c = ya.shape[-1]
    dff = wup.shape[-1]
    row = lambda i: (i, 0)
    lay = lambda i: (layer, 0, 0)
    n_out = 2 if final else 1
    return pl.pallas_call(
        functools.partial(_merge_mlp_kernel, final=final),
        grid=(n // tm,),
        in_specs=[pl.BlockSpec((tm, d), row)] + [pl.BlockSpec((tm, c), row)] * 3 + [
            pl.BlockSpec((tm, 3 * d), row),
            _mod_spec(mods, 2, layer, tiles_per_group), _mod_spec(mods, 4, layer, tiles_per_group),
            _mod_spec(mods, 3, layer, tiles_per_group), _mod_spec(mods, 5, layer, tiles_per_group),
            pl.BlockSpec((1, 1, d), lay),
            pl.BlockSpec((1, d), lambda i: (0, 0)),
            _resident((1, c, d), lay), _resident((1, c, d), lay), _resident((1, c, d), lay),
            _resident((1, d, d), lay), _resident((1, d, dff), lay), _resident((1, dff, d), lay),
        ],
        out_specs=[pl.BlockSpec((tm, d), row)] * n_out,
        out_shape=[jax.ShapeDtypeStruct((n, d), F32)] * n_out,
        compiler_params=_params("arbitrary"),
        name="merge_mlp",
    )(x, ya, ob, oc, gates, mods, mods, mods, mods, g2, gf, wa, wb, wc, wo, wup, wdn)


def _t5_bucket_table(n_buckets, max_dist):
    dist = np.arange(max_dist + 1)
    max_exact = n_buckets // 2
    d = np.maximum(dist, 1).astype(np.float32)
    large = max_exact + (np.log(d / max_exact) / math.log(MAX_DISTANCE / max_exact)
                         * (n_buckets - max_exact)).astype(np.int32)
    large = np.minimum(large, n_buckets - 1)
    return np.where(dist < max_exact, dist, large).astype(np.int32)


def _pad_heads(w, n_heads):
    k = w.shape[-1] // n_heads
    w = w.reshape(w.shape[:-1] + (n_heads, k))
    w = jnp.pad(w, [(0, 0)] * (w.ndim - 1) + [(0, LANES - k)])
    return w.reshape(w.shape[:-2] + (n_heads * LANES,))


def kernel(x_prompt, x_sample, c_prompt, c_sample, cache_k, cache_v, page_table, state_conv, state_gla,
           w_ada, b_ada, g_norm1, w_in, w_dw, b_dw, ln_g, ln_b, w_pw2, w_pb, rel_bias, w_a2, b_a, g_gla,
           w_pc, w_o, g_norm2, w_up, w_down, g_final):
    bp, t, d = x_prompt.shape
    bs, ts, _ = x_sample.shape
    depth = w_in.shape[0]
    _, n_pool, page, h_b, hd_b = cache_k.shape
    d_b = h_b * hd_b
    d_conv = w_dw.shape[-1]
    width = w_dw.shape[1]
    rank = w_a2.shape[1]
    h_c = GLA_HEADS
    dk_c = w_a2.shape[-1] // h_c
    dv_c = state_gla.shape[-1]
    hp = h_c * LANES
    n_buckets = rel_bias.shape[0]
    n_pages = page_table.shape[1]
    past_len = n_pages * page
    assert dv_c == LANES and h_c * dv_c == hp and width - 1 <= CONV_HALO

    sizes = (2 * d_conv, d_b, d_b, d_b, h_c * dk_c, h_c * dk_c, h_c * dv_c, h_c * dv_c, rank, d, d, d)
    offs = np.cumsum((0,) + sizes)
    part = [w_in[:, :, offs[i]:offs[i + 1]] for i in range(len(sizes))]
    w_cat = jnp.concatenate([
        part[0], part[1], part[2], part[3],
        _pad_heads(part[4], h_c), _pad_heads(part[5], h_c), part[6], part[7],
        jnp.pad(part[8], ((0, 0), (0, 0), (0, LANES - rank))),
        part[9], part[10], part[11]], axis=-1).astype(BF16)
    wa2 = jnp.pad(_pad_heads(w_a2, h_c), ((0, 0), (0, LANES - rank), (0, 0))).astype(BF16)
    ba2 = _pad_heads(b_a, h_c).reshape(depth, 1, hp)
    gg = g_gla.reshape(depth, 1, hp)
    wpw2, wpb, wpc, wo, wup, wdn = (w.astype(BF16) for w in (w_pw2, w_pb, w_pc, w_o, w_up, w_down))
    g1 = g_norm1.reshape(depth, 1, d)
    g2 = g_norm2.reshape(depth, 1, d)
    gf = g_final.reshape(1, d)
    bdw = b_dw.reshape(depth, 1, d_conv)
    lng = ln_g.reshape(depth, 1, d_conv)
    lnb = ln_b.reshape(depth, 1, d_conv)

    n_c = bp + bs
    rows_c = -(-n_c // 8) * 8
    c_all = jnp.pad(jnp.concatenate([c_prompt, c_sample], axis=0), ((0, rows_c - n_c), (0, 0)))
    mods = _adaln(c_all, w_ada, b_ada).reshape(depth, rows_c, 6, d)

    blk = MOBA_BLOCK
    assert ts <= blk and past_len % blk == 0
    bucket = _t5_bucket_table(n_buckets, 2 * blk - 1)
    onehot = np.zeros((4 * blk, n_buckets), np.float32)
    onehot[np.arange(2 * blk), bucket] = 1.0
    by_dist = jnp.dot(onehot, rel_bias - rel_bias[n_buckets - 1:n_buckets, :],
                      precision=lax.Precision.HIGHEST).T
    toep = jnp.tile(by_dist, (1, blk))[:, :blk * (4 * blk - 1)].reshape(h_b, blk, 4 * blk - 1)
    causal = np.arange(blk)[:, None] <= np.arange(blk)[None, :]
    bias_t = jnp.stack([jnp.where(causal, toep[:, :, :blk] * LOG2E, NEG), toep[:, :, blk:2 * blk] * LOG2E],
                       axis=1)
    rows_s = ts * h_b
    own = jnp.transpose(toep[:, :ts, :ts], (2, 0, 1)).reshape(rows_s, ts)
    own_ok = np.repeat(np.arange(ts), h_b)[:, None] >= np.arange(LANES)[None, :]
    bias_own = jnp.where(own_ok, jnp.pad(own, ((0, 0), (0, LANES - ts))), NEG)
    bias_tail = jnp.transpose(toep[:, :, blk:blk + ts], (2, 0, 1)).reshape(rows_s, blk)
    assert past_len // blk <= LANES
    cache_kt = jnp.transpose(cache_k, (0, 1, 3, 4, 2)).reshape(depth, n_pool, d_b, page)
    cache_vt = jnp.transpose(cache_v, (0, 1, 3, 4, 2)).reshape(depth, n_pool, d_b, page)

    xp = x_prompt.reshape(bp * t, d)
    xs = x_sample.reshape(bs * ts, d)
    n_s = bs * ts
    tm = TOKEN_TILE
    tpb = t // tm
    dims = (d_conv, d_b, hp, hd_b, dk_c)
    ts_pad = 8
    gla_rows_s = GLA_CHUNK

    cp_l, sp_l, cs_l, ss_l = ([] for _ in range(4))
    y_prompt = y_sample = None
    zeros_hist = jnp.zeros((bp, CONV_HALO, d_conv), F32)
    zeros_state = jnp.zeros((bp, h_c, LANES, LANES), F32)
    mods = jnp.transpose(mods, (0, 2, 1, 3))
    mods_p = mods[:, :, :bp].reshape(depth, 6, bp, 1, d)
    mods_s = jnp.repeat(mods[:, :, bp:n_c], ts, axis=2).reshape(depth, 6, 1, n_s, d)
    kv_p = kv_s = None
    for l in range(depth):
        final = l == depth - 1

        (u, qb, kb, vb, k16, vt, km, gq, gk, gv, gr, lr, gates) = _inproj(
            xp, mods_p, g1, w_cat, kv_p, l, tm=tm, tiles_per_group=tpb, dims=dims, attn_aux=True)
        kv_p = (kb, vb)
        ya, tail = _conv_prompt(u.reshape(bp, t, d_conv), zeros_hist, w_dw, bdw, lng, lnb, l, tt=tm)
        ob = _moba_prompt(qb, k16, vt, km, bias_t, batch=bp, n_heads=h_b, hd=hd_b)
        oc, stf = _gla(*(a.reshape(bp, t, -1) for a in (gq, gk, gv, gr, lr)), wa2, ba2, gg, zeros_state, l,
                       tt=tm, valid=tm)
        res = _merge_mlp(xp, ya.reshape(bp * t, d_conv), ob.reshape(bp * t, d_b), oc.reshape(bp * t, hp),
                         gates, mods_p, g2, gf, wpw2, wpb, wpc, wo, wup, wdn, l,
                         tm=tm, tiles_per_group=tpb, final=final)
        xp = res[0]
        if final:
            y_prompt = res[1].reshape(bp, t, d)
        cp_l.append(tail[:, CONV_HALO - (width - 1):])
        sp_l.append(jnp.swapaxes(stf[..., :dk_c], -1, -2))

        (u, qb, kb, vb, gq, gk, gv, gr, lr, gates) = _inproj(
            xs, mods_s, g1, w_cat, kv_s, l, tm=n_s, tiles_per_group=1, dims=dims, attn_aux=False)
        kv_s = (kb, vb)
        kb, vb = kb[l], vb[l]
        ext = jnp.concatenate([state_conv[l], u.reshape(bs, ts, d_conv)], axis=1)
        ext_p = jnp.pad(ext, ((0, 0), (0, ts_pad + width - 1 - ext.shape[1] + (-(ts_pad + width - 1)) % 8), (0, 0)))
        ya = _conv_sample(ext_p, w_dw, bdw, lng, lnb, l, rows=ts_pad)[:, :ts]
        q_rep = jnp.repeat(qb.reshape(bs, ts, d_b), h_b, axis=1)
        pad_new = lambda a: jnp.pad(a.reshape(bs, ts, d_b), ((0, 0), (0, ts_pad - ts), (0, 0)))
        ob = _moba_sample(page_table, q_rep, pad_new(kb), pad_new(vb), bias_own, bias_tail,
                          cache_kt, cache_vt, l, n_heads=h_b, t_new=ts)
        pad_t = lambda a: jnp.pad(a.reshape(bs, ts, -1), ((0, 0), (0, gla_rows_s - ts), (0, 0)))
        st0 = jnp.pad(jnp.swapaxes(state_gla[l], -1, -2), ((0, 0), (0, 0), (0, 0), (0, LANES - dk_c)))
        oc, stf = _gla(pad_t(gq), pad_t(gk), pad_t(gv), pad_t(gr), pad_t(lr), wa2, ba2, gg, st0, l,
                       tt=gla_rows_s, valid=ts)
        res = _merge_mlp(xs, ya.reshape(n_s, d_conv), ob.reshape(n_s, d_b), oc[:, :ts].reshape(n_s, hp),
                         gates, mods_s, g2, gf, wpw2, wpb, wpc, wo, wup, wdn, l,
                         tm=n_s, tiles_per_group=1, final=final)
        xs = res[0]
        if final:
            y_sample = res[1].reshape(bs, ts, d)
        cs_l.append(ext[:, -(width - 1):])
        ss_l.append(jnp.swapaxes(stf[..., :dk_c], -1, -2))

    rows_p = lambda a: jnp.transpose(a.reshape(depth, bp, h_b, hd_b, t), (0, 1, 4, 2, 3))
    rows_s_ = lambda a: a.reshape(depth, bs, ts, h_b, hd_b)
    return (y_prompt, y_sample, rows_p(kv_p[0]), rows_p(kv_p[1]), jnp.stack(cp_l), jnp.stack(sp_l),
            rows_s_(kv_s[0]), rows_s_(kv_s[1]), jnp.stack(cs_l), jnp.stack(ss_l))
```

```python
import functools
import math

import numpy as np
import jax
import jax.numpy as jnp
from jax import lax
from jax.experimental import pallas as pl
from jax.experimental.pallas import tpu as pltpu

F32 = jnp.float32
BF16 = jnp.bfloat16

MOBA_BLOCK = 256
MOBA_TOPK = 3
MAX_DISTANCE = 128
GLA_HEADS = 4
GLA_CHUNK = 16
GATE_NORM = 16.0
EPS = 1e-6

LANES = 128
SUBLANES = 8
VMEM_LIMIT = 56 * 1024 * 1024

NEG = -1e30
LOG2E = math.log2(math.e)
V_ROW_PAD = 16
TOKEN_TILE = 256
SAMPLE_SEQS_PER_STEP = 8
PAGES_PER_CHUNK = 32
CONV_HALO = 32

_NT = (((1,), (1,)), ((), ()))


def _dot(a, b):
    return jnp.dot(a, b, preferred_element_type=F32)


def _dot_nt(a, b):
    return lax.dot_general(a, b, _NT, preferred_element_type=F32)


def _params(*sem):
    return pltpu.CompilerParams(dimension_semantics=sem, vmem_limit_bytes=VMEM_LIMIT)


def _resident(shape, index_map):
    return pl.BlockSpec(shape, index_map, pipeline_mode=pl.Buffered(1))


def _rms_rows(x, g):
    return (x * lax.rsqrt(jnp.mean(x * x, axis=-1, keepdims=True) + EPS)) * g


def _log_sigmoid(x):
    return jnp.minimum(x, 0.0) - jnp.log(1.0 + jnp.exp(-jnp.abs(x)))


def _top_k_mask(s, idx, axis, k, sentinel):
    sel = jnp.zeros(s.shape, dtype=jnp.bool_)
    for _ in range(k):
        mx = jnp.max(s, axis=axis, keepdims=True)
        first = jnp.min(jnp.where(s == mx, idx, sentinel), axis=axis, keepdims=True)
        pick = (idx == first) & (mx > -jnp.inf)
        sel = sel | pick
        s = jnp.where(pick, -jnp.inf, s)
    return sel


def _adaln_kernel(c_ref, w_ref, b_ref, o_ref):
    c = c_ref[...]
    s = c * jax.nn.sigmoid(c)
    o_ref[0] = _dot(s.astype(BF16), w_ref[0].astype(BF16)) + b_ref[0]


def _adaln(c_all, w_ada, b_ada):
    depth, d, n = w_ada.shape
    rows = c_all.shape[0]
    tn = 1536
    return pl.pallas_call(
        _adaln_kernel,
        grid=(depth, n // tn),
        in_specs=[
            pl.BlockSpec((rows, d), lambda l, j: (0, 0)),
            pl.BlockSpec((1, d, tn), lambda l, j: (l, 0, j)),
            pl.BlockSpec((1, 1, tn), lambda l, j: (l, 0, j)),
        ],
        out_specs=pl.BlockSpec((1, rows, tn), lambda l, j: (l, 0, j)),
        out_shape=jax.ShapeDtypeStruct((depth, rows, n), F32),
        compiler_params=_params("arbitrary", "arbitrary"),
        name="adaln",
    )(c_all, w_ada, b_ada.reshape(depth, 1, n))


def _inproj_kernel(x_ref, sc_ref, sh_ref, g_ref, w_ref, *rest, d_conv, d_b, hp, d_model,
                   q_scale, gq_scale, n_heads_b, attn_aux, n_alias, kv_layer):
    outs = rest[n_alias:]
    if attn_aux:
        (u_ref, q_ref, k_ref, v_ref, k16_ref, vt_ref, km_ref,
         gq_ref, gk_ref, gv_ref, gr_ref, lr_ref, gate_ref) = outs
    else:
        (u_ref, q_ref, k_ref, v_ref, gq_ref, gk_ref, gv_ref, gr_ref, lr_ref, gate_ref) = outs
    x = x_ref[...]
    h = _rms_rows(x, g_ref[0]) * (1.0 + sc_ref[0, 0, 0]) + sh_ref[0, 0, 0]
    hb = h.astype(BF16)

    pos = [0]

    def seg(width):
        lo = pos[0]
        pos[0] = lo + width
        return _dot(hb, w_ref[0, :, lo:lo + width])

    a = seg(d_conv)
    g = seg(d_conv)
    u_ref[...] = a * jax.nn.sigmoid(g)
    q_ref[...] = seg(d_b) * q_scale
    k = seg(d_b)
    v = seg(d_b)
    def put(ref, val):
        if n_alias:
            ref[0] = val
        else:
            for l in range(ref.shape[0]):
                ref[l] = val if l == kv_layer else jnp.zeros_like(val)

    if not attn_aux:
        put(k_ref, k)
        put(v_ref, v)
    else:
        v_t = v.T
        put(k_ref, k.T[None])
        put(v_ref, v_t[None])
        k16_ref[0] = k.astype(BF16)
        vt = v_t.astype(BF16)
        hd = d_b // n_heads_b
        tail = jnp.where(lax.broadcasted_iota(jnp.int32, (V_ROW_PAD, vt.shape[1]), 0) == 0, 1.0, 0.0).astype(BF16)
        for h in range(n_heads_b):
            vt_ref[0, h * (hd + V_ROW_PAD):h * (hd + V_ROW_PAD) + hd, :] = vt[h * hd:(h + 1) * hd, :]
            vt_ref[0, h * (hd + V_ROW_PAD) + hd:(h + 1) * (hd + V_ROW_PAD), :] = tail
        km_ref[0] = jnp.mean(k, axis=0, keepdims=True)
    gq_ref[...] = seg(hp) * gq_scale
    gk_ref[...] = seg(hp)
    gv_ref[...] = seg(hp)
    gr_ref[...] = seg(hp)
    lr_ref[...] = seg(LANES)
    for i in range(3):
        gate_ref[:, i * d_model:(i + 1) * d_model] = jax.nn.sigmoid(seg(d_model))


def _mod_spec(mods, idx, layer, tiles_per_group):
    return pl.BlockSpec((1, 1, 1) + mods.shape[3:], lambda i: (layer, idx, i // tiles_per_group, 0, 0))


def _inproj(x, mods, g1, w_cat, kv_bufs, layer, *, tm, tiles_per_group, dims, attn_aux):
    n, d = x.shape
    d_conv, d_b, hp, hd_b, dk_c = dims
    depth, _, nw = w_cat.shape
    grid = (n // tm,)
    row = lambda i: (i, 0)
    lay = lambda i: (layer, 0, 0)
    lead, l0 = (depth, 0) if kv_bufs is None else (1, layer)
    if attn_aux:
        tpg = tiles_per_group
        kv_shape = (depth, n // (tpg * tm), d_b, tpg * tm)
        kv_spec = pl.BlockSpec((lead, 1, d_b, tm), lambda i: (l0, i // tpg, 0, i % tpg))
    else:
        kv_shape, kv_spec = (depth, n, d_b), pl.BlockSpec((lead, tm, d_b), lambda i: (l0, i, 0))
    outs = [((n, d_conv), F32), ((n, d_b), F32), (kv_shape, F32), (kv_shape, F32)]
    specs = [pl.BlockSpec((tm, d_conv), row), pl.BlockSpec((tm, d_b), row), kv_spec, kv_spec]
    if attn_aux:
        assert tm == MOBA_BLOCK
        nblk = n // MOBA_BLOCK
        vt_rows = d_b + (d_b // hd_b) * V_ROW_PAD
        outs += [((nblk, MOBA_BLOCK, d_b), BF16), ((nblk, vt_rows, MOBA_BLOCK), BF16), ((nblk, 1, d_b), F32)]
        specs += [pl.BlockSpec((1, MOBA_BLOCK, d_b), lambda i: (i, 0, 0)),
                  pl.BlockSpec((1, vt_rows, MOBA_BLOCK), lambda i: (i, 0, 0)),
                  pl.BlockSpec((1, 1, d_b), lambda i: (i, 0, 0))]
    outs += [((n, hp), F32)] * 4 + [((n, LANES), F32), ((n, 3 * d), F32)]
    specs += [pl.BlockSpec((tm, hp), row)] * 4 + [pl.BlockSpec((tm, LANES), row), pl.BlockSpec((tm, 3 * d), row)]
    body = functools.partial(_inproj_kernel, d_conv=d_conv, d_b=d_b, hp=hp, d_model=d,
                             q_scale=hd_b ** -0.5, gq_scale=dk_c ** -0.5, n_heads_b=d_b // hd_b,
                             attn_aux=attn_aux, n_alias=0 if kv_bufs is None else 2, kv_layer=layer)
    in_specs = [
        pl.BlockSpec((tm, d), row),
        _mod_spec(mods, 1, layer, tiles_per_group),
        _mod_spec(mods, 0, layer, tiles_per_group),
        pl.BlockSpec((1, 1, d), lay),
        _resident((1, d, nw), lay),
    ]
    args = [x, mods, mods, g1, w_cat]
    aliases = {}
    if kv_bufs is not None:
        in_specs += [pl.BlockSpec(memory_space=pl.ANY)] * 2
        aliases = {len(args): 2, len(args) + 1: 3}
        args += list(kv_bufs)
    return pl.pallas_call(
        body,
        grid=grid,
        in_specs=in_specs,
        out_specs=specs,
        out_shape=[jax.ShapeDtypeStruct(s, t) for s, t in outs],
        input_output_aliases=aliases,
        compiler_params=_params("arbitrary"),
        name="inproj",
    )(*args)


def _conv_post(acc, lng_ref, lnb_ref):
    mu = jnp.mean(acc, axis=-1, keepdims=True)
    cen = acc - mu
    var = jnp.mean(cen * cen, axis=-1, keepdims=True)
    y = (cen * lax.rsqrt(var + EPS)) * lng_ref[0] + lnb_ref[0]
    return y * jax.nn.sigmoid(y)


def _conv_taps(ext_ref, start, rows, wdw_ref, bdw_ref, lng_ref, lnb_ref, width):
    acc = jnp.zeros((rows, wdw_ref.shape[-1]), F32) + bdw_ref[0]
    for k in range(width):
        acc = acc + ext_ref[pl.ds(start + k, rows), :] * wdw_ref[0, k:k + 1, :]
    return _conv_post(acc, lng_ref, lnb_ref)


def _conv_taps_aligned(ext_ref, start, rows, wdw_ref, bdw_ref, lng_ref, lnb_ref, width, shift_scr, out_ref):
    span = shift_scr.shape[1]
    for r in range(1, SUBLANES):
        shift_scr[r - 1] = ext_ref[pl.ds(r, span), :]
    acc = jnp.zeros((rows, wdw_ref.shape[-1]), F32) + bdw_ref[0]
    for k in range(width):
        a, r = divmod(start + k, SUBLANES)
        src = ext_ref if r == 0 else shift_scr.at[r - 1]
        acc = acc + src[pl.ds(a * SUBLANES, rows), :] * wdw_ref[0, k:k + 1, :]
    out_ref[...] = _conv_post(acc, lng_ref, lnb_ref).astype(out_ref.dtype)


def _conv_kernel(u_ref, hist_ref, wdw_ref, bdw_ref, lng_ref, lnb_ref, y_ref, tail_ref, ext_scr, shift_scr,
                 *, tt, width):
    j = pl.program_id(1)

    @pl.when(j == 0)
    def _():
        ext_scr[0:CONV_HALO, :] = hist_ref[0]

    @pl.when(j > 0)
    def _():
        ext_scr[0:CONV_HALO, :] = ext_scr[tt:tt + CONV_HALO, :]

    ext_scr[CONV_HALO:CONV_HALO + tt, :] = u_ref[0]
    start = CONV_HALO - (width - 1)
    _conv_taps_aligned(ext_scr, start, tt, wdw_ref, bdw_ref, lng_ref, lnb_ref, width, shift_scr, y_ref.at[0])

    @pl.when(j == pl.num_programs(1) - 1)
    def _():
        tail_ref[0] = ext_scr[tt:tt + CONV_HALO, :]


def _conv_prompt(u, hist, w_dw, b_dw, ln_g, ln_b, layer, *, tt):
    b, t, c = u.shape
    width = w_dw.shape[1]
    lay = lambda bi, j: (layer, 0, 0)
    return pl.pallas_call(
        functools.partial(_conv_kernel, tt=tt, width=width),
        grid=(b, t // tt),
        in_specs=[
            pl.BlockSpec((1, tt, c), lambda bi, j: (bi, j, 0)),
            pl.BlockSpec((1, CONV_HALO, c), lambda bi, j: (bi, 0, 0)),
            pl.BlockSpec((1, width, c), lay),
            pl.BlockSpec((1, 1, c), lay),
            pl.BlockSpec((1, 1, c), lay),
            pl.BlockSpec((1, 1, c), lay),
        ],
        out_specs=[pl.BlockSpec((1, tt, c), lambda bi, j: (bi, j, 0)),
                   pl.BlockSpec((1, CONV_HALO, c), lambda bi, j: (bi, 0, 0))],
        out_shape=[jax.ShapeDtypeStruct((b, t, c), BF16), jax.ShapeDtypeStruct((b, CONV_HALO, c), F32)],
        scratch_shapes=[pltpu.VMEM((CONV_HALO + tt, c), F32),
                        pltpu.VMEM((SUBLANES - 1, CONV_HALO + tt - SUBLANES, c), F32)],
        compiler_params=_params("arbitrary", "arbitrary"),
        name="conv_prompt",
    )(u, hist, w_dw, b_dw, ln_g, ln_b)


def _conv_small_kernel(ext_ref, wdw_ref, bdw_ref, lng_ref, lnb_ref, y_ref, *, rows, width):
    for i in range(ext_ref.shape[0]):
        y_ref[i] = _conv_taps(ext_ref.at[i], 0, rows, wdw_ref, bdw_ref, lng_ref, lnb_ref, width).astype(BF16)


def _conv_sample(ext, w_dw, b_dw, ln_g, ln_b, layer, *, rows):
    b, r, c = ext.shape
    width = w_dw.shape[1]
    per_step = math.gcd(b, SAMPLE_SEQS_PER_STEP)
    lay = lambda bi: (layer, 0, 0)
    return pl.pallas_call(
        functools.partial(_conv_small_kernel, rows=rows, width=width),
        grid=(b // per_step,),
        in_specs=[
            pl.BlockSpec((per_step, r, c), lambda bi: (bi, 0, 0)),
            pl.BlockSpec((1, width, c), lay),
            pl.BlockSpec((1, 1, c), lay),
            pl.BlockSpec((1, 1, c), lay),
            pl.BlockSpec((1, 1, c), lay),
        ],
        out_specs=pl.BlockSpec((per_step, rows, c), lambda bi: (bi, 0, 0)),
        out_shape=jax.ShapeDtypeStruct((b, rows, c), BF16),
        compiler_params=_params("arbitrary"),
        name="conv_sample",
    )(ext, w_dw, b_dw, ln_g, ln_b)


def _moba_prompt_kernel(q_ref, k16_ref, vt_ref, km_ref, bias_ref, o_ref,
                        qm_scr, sel_scr, m_scr, acc_scr, s2_scr, *, nb, n_heads, hd):
    i = pl.program_id(1)
    blk = MOBA_BLOCK
    q = q_ref[0]
    km = km_ref[0]
    lane = lax.broadcasted_iota(jnp.int32, (blk, LANES), 1)
    blk_idx = lax.broadcasted_iota(jnp.int32, (nb, blk), 0)
    heads_per_slab = LANES // hd

    km16 = km.astype(BF16)
    for h in range(n_heads):
        p, w = divmod(h, heads_per_slab)
        qm = jnp.where((lane // hd) == w, q[:, p * LANES:(p + 1) * LANES], 0.0)
        qm_scr[h] = (qm * LOG2E).astype(BF16)
        s = _dot_nt(km16[:, p * LANES:(p + 1) * LANES], qm.astype(BF16))
        s = jnp.where(blk_idx < i, s, -jnp.inf)
        sel = _top_k_mask(s, blk_idx, 0, MOBA_TOPK, nb)
        sel_scr[h] = jnp.where(sel, 1.0, 0.0)

    hv = vt_ref.shape[1] // n_heads

    def block(n, kind):
        kb = k16_ref[n]
        vb = vt_ref[n]

        for h in range(n_heads):
            p = h // heads_per_slab
            s2_scr[h, 0:blk, :] =_dot_nt(kb[:, p * LANES:(p + 1) * LANES], qm_scr[h])
        for h in range(n_heads):
            s = s2_scr[h, 0:blk, :]
            rows = slice(h * hv, (h + 1) * hv)
            if kind == "own":
                s = s + bias_ref[h, 0]
                m = jnp.max(s, axis=0, keepdims=True)
                m_scr[h:h + 1, :] = m
                acc_scr[rows, :] = _dot(vb[rows, :], jnp.exp2(s - m).astype(BF16))
            else:
                picked = sel_scr[h, pl.ds(n, 1), :] > 0.5
                m_old = m_scr[h:h + 1, :]
                m = jnp.where(picked, jnp.maximum(m_old, jnp.max(s, axis=0, keepdims=True)), m_old)
                alpha = jnp.exp2(m_old - m)
                pr = jnp.exp2(s - jnp.where(picked, m, -NEG))
                m_scr[h:h + 1, :] = m
                acc_scr[rows, :] = alpha * acc_scr[rows, :] + _dot(vb[rows, :], pr.astype(BF16))

    def pair_logits(n, buf):
        kb = k16_ref[pl.ds(n, 2)].reshape(2 * blk, k16_ref.shape[-1])
        for h in range(n_heads):
            p = h // heads_per_slab
            buf[h] = _dot_nt(kb[:, p * LANES:(p + 1) * LANES], qm_scr[h])

    @pl.when(i == 0)
    def _():
        block(0, "own")

    @pl.when(i >= 1)
    def _():
        n = i - 1
        pair_logits(n, s2_scr)
        vb = jnp.concatenate([vt_ref[n], vt_ref[n + 1]], axis=1)
        for h in range(n_heads):
            rows = slice(h * hv, (h + 1) * hv)
            s_a = s2_scr[h, 0:blk, :] + bias_ref[h, 1]
            s_b = s2_scr[h, blk:2 * blk, :] + bias_ref[h, 0]
            pick_a = sel_scr[h, pl.ds(n, 1), :] > 0.5
            m = jnp.maximum(jnp.where(pick_a, jnp.max(s_a, axis=0, keepdims=True), NEG),
                            jnp.max(s_b, axis=0, keepdims=True))
            pr = jnp.concatenate([jnp.exp2(s_a - jnp.where(pick_a, m, -NEG)).astype(BF16),
                                  jnp.exp2(s_b - m).astype(BF16)], axis=0)
            m_scr[h:h + 1, :] = m
            acc_scr[rows, :] = _dot(vb[rows, :], pr)

    def pair_update(n, buf):
        vb = jnp.concatenate([vt_ref[n], vt_ref[n + 1]], axis=1)
        for h in range(n_heads):
            rows = slice(h * hv, (h + 1) * hv)
            s_a = buf[h, 0:blk, :]
            s_b = buf[h, blk:2 * blk, :]
            pick_a = sel_scr[h, pl.ds(n, 1), :] > 0.5
            pick_b = sel_scr[h, pl.ds(n + 1, 1), :] > 0.5
            m_old = m_scr[h:h + 1, :]
            m = jnp.maximum(m_old, jnp.maximum(
                jnp.where(pick_a, jnp.max(s_a, axis=0, keepdims=True), NEG),
                jnp.where(pick_b, jnp.max(s_b, axis=0, keepdims=True), NEG)))
            alpha = jnp.exp2(m_old - m)
            pr = jnp.concatenate([jnp.exp2(s_a - jnp.where(pick_a, m, -NEG)).astype(BF16),
                                  jnp.exp2(s_b - jnp.where(pick_b, m, -NEG)).astype(BF16)], axis=0)
            m_scr[h:h + 1, :] = m
            acc_scr[rows, :] = alpha * acc_scr[rows, :] + _dot(vb[rows, :], pr)

    n_far = jnp.maximum(i - 1, 0)

    def far(t, carry):
        pair_logits(2 * t, s2_scr)
        pair_update(2 * t, s2_scr)
        return carry

    lax.fori_loop(0, n_far // 2, far, 0)

    @pl.when(n_far % 2 == 1)
    def _():
        block(n_far - 1, "far")

    out = [acc_scr[h * hv:h * hv + hd, :] / acc_scr[h * hv + hd:h * hv + hd + 1, :] for h in range(n_heads)]
    o_ref[0] = jnp.concatenate(out, axis=0).T.astype(BF16)


def _moba_prompt(q, k16, vt, km, bias_t, *, batch, n_heads, hd):
    n, d_b = q.shape
    t = n // batch
    nb = t // MOBA_BLOCK
    blk = MOBA_BLOCK
    body = functools.partial(_moba_prompt_kernel, nb=nb, n_heads=n_heads, hd=hd)
    return pl.pallas_call(
        body,
        grid=(batch, nb),
        in_specs=[
            pl.BlockSpec((1, blk, d_b), lambda b, i: (b * nb + i, 0, 0)),
            pl.BlockSpec((nb, blk, d_b), lambda b, i: (b, 0, 0)),
            pl.BlockSpec((nb, vt.shape[1], blk), lambda b, i: (b, 0, 0)),
            pl.BlockSpec((1, nb, d_b), lambda b, i: (b, 0, 0)),
            _resident((n_heads, 2, blk, blk), lambda b, i: (0, 0, 0, 0)),
        ],
        out_specs=pl.BlockSpec((1, blk, d_b), lambda b, i: (b * nb + i, 0, 0)),
        out_shape=jax.ShapeDtypeStruct((n // blk, blk, d_b), BF16),
        scratch_shapes=[
            pltpu.VMEM((n_heads, blk, LANES), BF16),
            pltpu.VMEM((n_heads, nb, blk), F32),
            pltpu.VMEM((n_heads, blk), F32),
            pltpu.VMEM((vt.shape[1], blk), F32),
            pltpu.VMEM((n_heads, 2 * blk, blk), F32),
        ],
        compiler_params=_params("arbitrary", "arbitrary"),
        name="moba_prompt",
    )(q.reshape(n // blk, blk, d_b), k16, vt, km.reshape(batch, nb, d_b), bias_t)


def _moba_sample_kernel(pt_ref, q_ref, kn_ref, vn_ref, bown_ref, btail_ref, kt_hbm, vt_hbm, o_ref,
                        s_scr, kv_scr, kbuf, vbuf, ksem, vsem,
                        *, layer, n_chunks, n_heads, t_new, page, ppc):
    b = pl.program_id(0)
    rows, d_b = q_ref.shape[1:]
    hd = d_b // n_heads
    blocks_per_chunk = ppc * page // MOBA_BLOCK
    pages_per_block = MOBA_BLOCK // page
    nbp = n_chunks * blocks_per_chunk

    def page_copy(hbm, buf, sem, seq, chunk, p):
        slot = chunk % 2
        return pltpu.make_async_copy(hbm.at[layer, pt_ref[seq, chunk * ppc + p]], buf.at[slot, p], sem.at[slot])

    def start_chunk(hbm, buf, sem, seq, chunk):
        for p in range(ppc):
            page_copy(hbm, buf, sem, seq, chunk, p).start()

    def wait_chunk(hbm, buf, sem, seq, chunk):
        for p in range(ppc):
            page_copy(hbm, buf, sem, seq, chunk, p).wait()

    @pl.when(b == 0)
    def _():
        start_chunk(kt_hbm, kbuf, ksem, b, 0)

    row_i = lax.broadcasted_iota(jnp.int32, (rows, d_b), 0)
    lane_i = lax.broadcasted_iota(jnp.int32, (rows, d_b), 1)
    head_mask = (lane_i // hd) == (row_i % n_heads)
    blk_lane = lax.broadcasted_iota(jnp.int32, (rows, LANES), 1)
    qb16 = jnp.where(head_mask, q_ref[0], 0.0).astype(BF16)

    bsum = jnp.zeros((rows, LANES), F32)
    for c in range(n_chunks):
        if c + 1 < n_chunks:
            start_chunk(kt_hbm, kbuf, ksem, b, c + 1)
        else:
            start_chunk(vt_hbm, vbuf, vsem, b, 0)
        wait_chunk(kt_hbm, kbuf, ksem, b, c)
        for p in range(ppc):
            s = _dot(qb16, kbuf[c % 2, p].astype(BF16))
            s_scr[c, :, p * page:(p + 1) * page] = s
            ps = jnp.sum(s, axis=-1, keepdims=True)
            psum = ps if p % pages_per_block == 0 else psum + ps
            if p % pages_per_block == pages_per_block - 1:
                bsum = jnp.where(blk_lane == c * blocks_per_chunk + p // pages_per_block, psum, bsum)

    bs = jnp.where(blk_lane < nbp, bsum, -jnp.inf)
    sel = _top_k_mask(bs, blk_lane, 1, MOBA_TOPK, LANES)
    blk_bias = jnp.where(sel, 0.0, NEG)

    kv_scr[...] = jnp.zeros(kv_scr.shape, F32)
    kv_scr[0:kn_ref.shape[1], :] = kn_ref[0]
    s_own = _dot_nt(qb16, kv_scr[...].astype(BF16)) + bown_ref[...]
    m = jnp.max(s_own, axis=-1, keepdims=True)
    for blk in range(nbp):
        cc, off = divmod(blk, blocks_per_chunk)
        cols = slice(off * MOBA_BLOCK, (off + 1) * MOBA_BLOCK)
        s = s_scr[cc, :, cols] + blk_bias[:, blk:blk + 1]
        if blk == nbp - 1:
            s = s + btail_ref[...]
        s_scr[cc, :, cols] = s
        m = jnp.maximum(m, jnp.max(s, axis=-1, keepdims=True))

    p_own = jnp.exp(s_own - m)
    l = jnp.sum(p_own, axis=-1, keepdims=True)
    for cc in range(n_chunks):
        pr = jnp.exp(s_scr[cc] - m)
        l = l + jnp.sum(pr, axis=-1, keepdims=True)
        s_scr[cc] = pr
    kv_scr[0:vn_ref.shape[1], :] = vn_ref[0]
    acc = _dot(p_own.astype(BF16), kv_scr[...].astype(BF16))

    for c in range(n_chunks):
        if c + 1 < n_chunks:
            start_chunk(vt_hbm, vbuf, vsem, b, c + 1)
        else:
            @pl.when(b + 1 < pl.num_programs(0))
            def _():
                start_chunk(kt_hbm, kbuf, ksem, b + 1, 0)
        wait_chunk(vt_hbm, vbuf, vsem, b, c)
        for p in range(ppc):
            pr = s_scr[c, :, p * page:(p + 1) * page]
            acc = acc + _dot_nt(pr.astype(BF16), vbuf[c % 2, p].astype(BF16))

    o = jnp.where(head_mask, acc / l, 0.0)
    o_ref[0] = jnp.sum(o.reshape(t_new, n_heads, d_b), axis=1).astype(BF16)


def _moba_sample(page_table, q_rep, k_new, v_new, bias_own, bias_tail, cache_kt, cache_vt, layer,
                 *, n_heads, t_new):
    b, rows, d_b = q_rep.shape
    page = cache_kt.shape[-1]
    n_pages = page_table.shape[1]
    ppc = min(PAGES_PER_CHUNK, n_pages)
    n_chunks = n_pages // ppc
    assert n_pages % ppc == 0 and (ppc * page) % MOBA_BLOCK == 0 and MOBA_BLOCK % page == 0
    chunk = ppc * page

    per_b = lambda bi, pt: (bi, 0, 0)
    const2 = lambda bi, pt: (0, 0)
    in_specs = [
        pl.BlockSpec((1, rows, d_b), per_b),
        pl.BlockSpec((1, k_new.shape[1], d_b), per_b),
        pl.BlockSpec((1, v_new.shape[1], d_b), per_b),
        pl.BlockSpec(bias_own.shape, const2),
        pl.BlockSpec(bias_tail.shape, const2),
        pl.BlockSpec(memory_space=pl.ANY),
        pl.BlockSpec(memory_space=pl.ANY),
    ]
    body = functools.partial(_moba_sample_kernel, layer=layer, n_chunks=n_chunks, n_heads=n_heads,
                             t_new=t_new, page=page, ppc=ppc)
    grid_spec = pltpu.PrefetchScalarGridSpec(
        num_scalar_prefetch=1,
        grid=(b,),
        in_specs=in_specs,
        out_specs=pl.BlockSpec((1, t_new, d_b), per_b),
        scratch_shapes=[
            pltpu.VMEM((n_chunks, rows, chunk), F32),
            pltpu.VMEM((LANES, d_b), F32),
            pltpu.VMEM((2, ppc, d_b, page), F32),
            pltpu.VMEM((2, ppc, d_b, page), F32),
            pltpu.SemaphoreType.DMA((2,)),
            pltpu.SemaphoreType.DMA((2,)),
        ],
    )
    return pl.pallas_call(
        body,
        grid_spec=grid_spec,
        out_shape=jax.ShapeDtypeStruct((b, t_new, d_b), BF16),
        compiler_params=_params("arbitrary"),
        name="moba_sample",
    )(page_table, q_rep, k_new, v_new, bias_own, bias_tail, cache_kt, cache_vt)


def _gla_kernel(gq_ref, gk_ref, gv_ref, gr_ref, lr_ref, wa_ref, ba_ref, gg_ref, st0_ref,
                o_ref, stf_ref, st_scr, bc_scr, bl_scr, qd_scr, kd_scr, o_scr,
                *, tt, valid, n_heads):
    j = pl.program_id(1)
    c_sz = GLA_CHUNK
    hp = n_heads * LANES

    @pl.when(j == 0)
    def _():
        st_scr[...] = st0_ref[0]

    x = _dot(lr_ref[0].astype(BF16), wa_ref[0]) + ba_ref[0]
    la = _log_sigmoid(x) * (1.0 / GATE_NORM)
    if valid < tt:
        la = jnp.where(lax.broadcasted_iota(jnp.int32, (tt, hp), 0) < valid, la, 0.0)
    r_i = lax.broadcasted_iota(jnp.int32, (tt, tt), 0)
    c_i = lax.broadcasted_iota(jnp.int32, (tt, tt), 1)
    low = jnp.where(((r_i // c_sz) == (c_i // c_sz)) & (c_i <= r_i), 1.0, 0.0).astype(BF16)
    la_hi = la.astype(BF16)
    la_lo = (la - la_hi.astype(F32)).astype(BF16)
    bc = _dot(low, la_hi) + _dot(low, la_lo)
    bc_scr[...] = bc
    n_ch = tt // c_sz
    in_ch = (lax.broadcasted_iota(jnp.int32, (n_ch, tt), 1) // c_sz
             == lax.broadcasted_iota(jnp.int32, (n_ch, tt), 0))
    tot = jnp.where(in_ch, 1.0, 0.0).astype(BF16)
    ends = _dot(tot, la_hi) + _dot(tot, la_lo)
    bl_scr[...] = ends
    bl = jnp.broadcast_to(ends[:, None, :], (n_ch, c_sz, hp)).reshape(tt, hp)
    qd_scr[...] = (gq_ref[0] * jnp.exp(bc)).astype(BF16)
    kd_scr[...] = (gk_ref[0] * jnp.exp(bl - bc)).astype(BF16)

    half = c_sz // 2
    t_half = lax.broadcasted_iota(jnp.int32, (half, LANES), 0)

    def chunk(c, carry):
        rows = pl.ds(c * c_sz, c_sz)
        bc_c = bc_scr[rows, :]
        q_c = gq_ref[0, rows, :]
        k_c = gk_ref[0, rows, :]
        v_c = gv_ref[0, rows, :]
        decay = jnp.exp(bl_scr[pl.ds(c, 1), :])
        for h in range(n_heads):
            hs = slice(h * LANES, (h + 1) * LANES)
            st = st_scr[h]
            o_h = _dot_nt(qd_scr[rows, hs], st.astype(BF16))
            bch = bc_c[:, hs]
            qh = q_c[:, hs]
            parts = [o_h[0:half], o_h[half:c_sz]]
            for s in range(c_sz):
                row_s = pl.ds(c * c_sz + s, 1)
                b_s = jnp.broadcast_to(bc_scr[row_s, hs], (half, LANES))
                k_s = jnp.broadcast_to(gk_ref[0, row_s, hs], (half, LANES))
                v_s = jnp.broadcast_to(gv_ref[0, row_s, hs], (half, LANES))
                for hi in range(2):
                    r0 = hi * half
                    if s >= r0 + half:
                        continue
                    diff = bch[r0:r0 + half] - b_s
                    if s > r0:
                        diff = jnp.where(t_half + r0 >= s, diff, -jnp.inf)
                    a_s = jnp.sum(qh[r0:r0 + half] * k_s * jnp.exp(diff), axis=-1, keepdims=True)
                    parts[hi] = parts[hi] + a_s * v_s
            o_scr[rows, hs] = jnp.concatenate(parts, axis=0)
            kvt = lax.dot_general(v_c[:, hs].astype(BF16), kd_scr[rows, hs], (((0,), (0,)), ((), ())),
                                  preferred_element_type=F32)
            st_scr[h] = decay[:, hs] * st + kvt
        return carry

    for c in range((valid + c_sz - 1) // c_sz):
        chunk(c, 0)

    o = o_scr[...]
    r = gr_ref[0]
    for h in range(n_heads):
        hs = slice(h * LANES, (h + 1) * LANES)
        oh = o[:, hs]
        oh = oh * lax.rsqrt(jnp.mean(oh * oh, axis=-1, keepdims=True) + EPS)
        rh = r[:, hs]
        o_ref[0, :, hs] = ((oh * gg_ref[0, :, hs]) * (rh * jax.nn.sigmoid(rh))).astype(BF16)

    @pl.when(j == pl.num_programs(1) - 1)
    def _():
        stf_ref[0] = st_scr[...]


def _gla(gq, gk, gv, gr, lr, wa, ba, gg, st0, layer, *, tt, valid):
    b, t, hp = gq.shape
    n_heads = hp // LANES
    lay = lambda bi, j: (layer, 0, 0)
    tile = lambda bi, j: (bi, j, 0)
    per_b = lambda bi, j: (bi, 0, 0, 0)
    body = functools.partial(_gla_kernel, tt=tt, valid=valid, n_heads=n_heads)
    return pl.pallas_call(
        body,
        grid=(b, t // tt),
        in_specs=[pl.BlockSpec((1, tt, hp), tile)] * 4 + [
            pl.BlockSpec((1, tt, LANES), tile),
            pl.BlockSpec((1, LANES, hp), lay),
            pl.BlockSpec((1, 1, hp), lay),
            pl.BlockSpec((1, 1, hp), lay),
            pl.BlockSpec((1, n_heads, LANES, LANES), per_b),
        ],
        out_specs=[pl.BlockSpec((1, tt, hp), tile), pl.BlockSpec((1, n_heads, LANES, LANES), per_b)],
        out_shape=[jax.ShapeDtypeStruct((b, t, hp), BF16),
                   jax.ShapeDtypeStruct((b, n_heads, LANES, LANES), F32)],
        scratch_shapes=[
            pltpu.VMEM((n_heads, LANES, LANES), F32),
            pltpu.VMEM((tt, hp), F32),
            pltpu.VMEM((tt // GLA_CHUNK, hp), F32),
            pltpu.VMEM((tt, hp), BF16),
            pltpu.VMEM((tt, hp), BF16),
            pltpu.VMEM((tt, hp), F32),
        ],
        compiler_params=_params("arbitrary", "arbitrary"),
        name="gla",
    )(gq, gk, gv, gr, lr, wa, ba, gg, st0)


def _merge_mlp_kernel(x_ref, ya_ref, ob_ref, oc_ref, gate_ref, gt1_ref, sc2_ref, sh2_ref, gt2_ref,
                      g2_ref, gf_ref, wa_ref, wb_ref, wc_ref, wo_ref, wup_ref, wdn_ref, *outs, final):
    d = x_ref.shape[-1]
    m = (gate_ref[:, 0:d] * _dot(ya_ref[...], wa_ref[0])
         + gate_ref[:, d:2 * d] * _dot(ob_ref[...], wb_ref[0])
         + gate_ref[:, 2 * d:3 * d] * _dot(oc_ref[...], wc_ref[0]))
    x1 = x_ref[...] + gt1_ref[0, 0, 0] * _dot(m.astype(BF16), wo_ref[0])
    h2 = _rms_rows(x1, g2_ref[0]) * (1.0 + sc2_ref[0, 0, 0]) + sh2_ref[0, 0, 0]
    up = jnp.maximum(_dot(h2.astype(BF16), wup_ref[0]), 0.0)
    x2 = x1 + gt2_ref[0, 0, 0] * _dot((up * up).astype(BF16), wdn_ref[0])
    outs[0][...] = x2
    if final:
        outs[1][...] = _rms_rows(x2, gf_ref[...])


def _merge_mlp(x, ya, ob, oc, gates, mods, g2, gf, wa, wb, wc, wo, wup, wdn, layer,
               *, tm, tiles_per_group, final):
    n, d = x.shape
    c = ya.shape[-1]
    dff = wup.shape[-1]
    row = lambda i: (i, 0)
    lay = lambda i: (layer, 0, 0)
    n_out = 2 if final else 1
    return pl.pallas_call(
        functools.partial(_merge_mlp_kernel, final=final),
        grid=(n // tm,),
        in_specs=[pl.BlockSpec((tm, d), row)] + [pl.BlockSpec((tm, c), row)] * 3 + [
            pl.BlockSpec((tm, 3 * d), row),
            _mod_spec(mods, 2, layer, tiles_per_group), _mod_spec(mods, 4, layer, tiles_per_group),
            _mod_spec(mods, 3, layer, tiles_per_group), _mod_spec(mods, 5, layer, tiles_per_group),
            pl.BlockSpec((1, 1, d), lay),
            pl.BlockSpec((1, d), lambda i: (0, 0)),
            _resident((1, c, d), lay), _resident((1, c, d), lay), _resident((1, c, d), lay),
            _resident((1, d, d), lay), _resident((1, d, dff), lay), _resident((1, dff, d), lay),
        ],
        out_specs=[pl.BlockSpec((tm, d), row)] * n_out,
        out_shape=[jax.ShapeDtypeStruct((n, d), F32)] * n_out,
        compiler_params=_params("arbitrary"),
        name="merge_mlp",
    )(x, ya, ob, oc, gates, mods, mods, mods, mods, g2, gf, wa, wb, wc, wo, wup, wdn)


def _t5_bucket_table(n_buckets, max_dist):
    dist = np.arange(max_dist + 1)
    max_exact = n_buckets // 2
    d = np.maximum(dist, 1).astype(np.float32)
    large = max_exact + (np.log(d / max_exact) / math.log(MAX_DISTANCE / max_exact)
                         * (n_buckets - max_exact)).astype(np.int32)
    large = np.minimum(large, n_buckets - 1)
    return np.where(dist < max_exact, dist, large).astype(np.int32)


def _pad_heads(w, n_heads):
    k = w.shape[-1] // n_heads
    w = w.reshape(w.shape[:-1] + (n_heads, k))
    w = jnp.pad(w, [(0, 0)] * (w.ndim - 1) + [(0, LANES - k)])
    return w.reshape(w.shape[:-2] + (n_heads * LANES,))


def kernel(x_prompt, x_sample, c_prompt, c_sample, cache_k, cache_v, page_table, state_conv, state_gla,
           w_ada, b_ada, g_norm1, w_in, w_dw, b_dw, ln_g, ln_b, w_pw2, w_pb, rel_bias, w_a2, b_a, g_gla,
           w_pc, w_o, g_norm2, w_up, w_down, g_final):
    bp, t, d = x_prompt.shape
    bs, ts, _ = x_sample.shape
    depth = w_in.shape[0]
    _, n_pool, page, h_b, hd_b = cache_k.shape
    d_b = h_b * hd_b
    d_conv = w_dw.shape[-1]
    width = w_dw.shape[1]
    rank = w_a2.shape[1]
    h_c = GLA_HEADS
    dk_c = w_a2.shape[-1] // h_c
    dv_c = state_gla.shape[-1]
    hp = h_c * LANES
    n_buckets = rel_bias.shape[0]
    n_pages = page_table.shape[1]
    past_len = n_pages * page
    assert dv_c == LANES and h_c * dv_c == hp and width - 1 <= CONV_HALO

    sizes = (2 * d_conv, d_b, d_b, d_b, h_c * dk_c, h_c * dk_c, h_c * dv_c, h_c * dv_c, rank, d, d, d)
    offs = np.cumsum((0,) + sizes)
    part = [w_in[:, :, offs[i]:offs[i + 1]] for i in range(len(sizes))]
    w_cat = jnp.concatenate([
        part[0], part[1], part[2], part[3],
        _pad_heads(part[4], h_c), _pad_heads(part[5], h_c), part[6], part[7],
        jnp.pad(part[8], ((0, 0), (0, 0), (0, LANES - rank))),
        part[9], part[10], part[11]], axis=-1).astype(BF16)
    wa2 = jnp.pad(_pad_heads(w_a2, h_c), ((0, 0), (0, LANES - rank), (0, 0))).astype(BF16)
    ba2 = _pad_heads(b_a, h_c).reshape(depth, 1, hp)
    gg = g_gla.reshape(depth, 1, hp)
    wpw2, wpb, wpc, wo, wup, wdn = (w.astype(BF16) for w in (w_pw2, w_pb, w_pc, w_o, w_up, w_down))
    g1 = g_norm1.reshape(depth, 1, d)
    g2 = g_norm2.reshape(depth, 1, d)
    gf = g_final.reshape(1, d)
    bdw = b_dw.reshape(depth, 1, d_conv)
    lng = ln_g.reshape(depth, 1, d_conv)
    lnb = ln_b.reshape(depth, 1, d_conv)

    n_c = bp + bs
    rows_c = -(-n_c // 8) * 8
    c_all = jnp.pad(jnp.concatenate([c_prompt, c_sample], axis=0), ((0, rows_c - n_c), (0, 0)))
    mods = _adaln(c_all, w_ada, b_ada).reshape(depth, rows_c, 6, d)

    blk = MOBA_BLOCK
    assert ts <= blk and past_len % blk == 0
    bucket = _t5_bucket_table(n_buckets, 2 * blk - 1)
    period = 2 * blk + LANES
    onehot = np.zeros((period, n_buckets), np.float32)
    onehot[np.arange(2 * blk), bucket] = 1.0
    by_dist = jnp.dot(onehot, rel_bias - rel_bias[n_buckets - 1:n_buckets, :],
                      precision=lax.Precision.HIGHEST).T
    toep = jnp.tile(by_dist, (1, blk))[:, :blk * (period - 1)].reshape(h_b, blk, period - 1)
    causal = np.arange(blk)[:, None] <= np.arange(blk)[None, :]
    bias_t = jnp.stack([jnp.where(causal, toep[:, :, :blk] * LOG2E, NEG), toep[:, :, blk:2 * blk] * LOG2E],
                       axis=1)
    rows_s = ts * h_b
    own = jnp.transpose(toep[:, :ts, :ts], (2, 0, 1)).reshape(rows_s, ts)
    own_ok = np.repeat(np.arange(ts), h_b)[:, None] >= np.arange(LANES)[None, :]
    bias_own = jnp.where(own_ok, jnp.pad(own, ((0, 0), (0, LANES - ts))), NEG)
    bias_tail = jnp.transpose(toep[:, :, blk:blk + ts], (2, 0, 1)).reshape(rows_s, blk)
    assert past_len // blk <= LANES
    cache_kt = jnp.transpose(cache_k, (0, 1, 3, 4, 2)).reshape(depth, n_pool, d_b, page)
    cache_vt = jnp.transpose(cache_v, (0, 1, 3, 4, 2)).reshape(depth, n_pool, d_b, page)

    xp = x_prompt.reshape(bp * t, d)
    xs = x_sample.reshape(bs * ts, d)
    n_s = bs * ts
    tm = TOKEN_TILE
    tpb = t // tm
    dims = (d_conv, d_b, hp, hd_b, dk_c)
    ts_pad = 8
    gla_rows_s = GLA_CHUNK

    cp_l, sp_l, cs_l, ss_l = ([] for _ in range(4))
    y_prompt = y_sample = None
    zeros_hist = jnp.zeros((bp, CONV_HALO, d_conv), F32)
    zeros_state = jnp.zeros((bp, h_c, LANES, LANES), F32)
    mods = jnp.transpose(mods, (0, 2, 1, 3))
    mods_p = mods[:, :, :bp].reshape(depth, 6, bp, 1, d)
    mods_s = jnp.repeat(mods[:, :, bp:n_c], ts, axis=2).reshape(depth, 6, 1, n_s, d)
    kv_p = kv_s = None
    for l in range(depth):
        final = l == depth - 1

        (u, qb, kb, vb, k16, vt, km, gq, gk, gv, gr, lr, gates) = _inproj(
            xp, mods_p, g1, w_cat, kv_p, l, tm=tm, tiles_per_group=tpb, dims=dims, attn_aux=True)
        kv_p = (kb, vb)
        ya, tail = _conv_prompt(u.reshape(bp, t, d_conv), zeros_hist, w_dw, bdw, lng, lnb, l, tt=tm)
        ob = _moba_prompt(qb, k16, vt, km, bias_t, batch=bp, n_heads=h_b, hd=hd_b)
        oc, stf = _gla(*(a.reshape(bp, t, -1) for a in (gq, gk, gv, gr, lr)), wa2, ba2, gg, zeros_state, l,
                       tt=tm, valid=tm)
        res = _merge_mlp(xp, ya.reshape(bp * t, d_conv), ob.reshape(bp * t, d_b), oc.reshape(bp * t, hp),
                         gates, mods_p, g2, gf, wpw2, wpb, wpc, wo, wup, wdn, l,
                         tm=tm, tiles_per_group=tpb, final=final)
        xp = res[0]
        if final:
            y_prompt = res[1].reshape(bp, t, d)
        cp_l.append(tail[:, CONV_HALO - (width - 1):])
        sp_l.append(jnp.swapaxes(stf[..., :dk_c], -1, -2))

        (u, qb, kb, vb, gq, gk, gv, gr, lr, gates) = _inproj(
            xs, mods_s, g1, w_cat, kv_s, l, tm=n_s, tiles_per_group=1, dims=dims, attn_aux=False)
        kv_s = (kb, vb)
        kb, vb = kb[l], vb[l]
        ext = jnp.concatenate([state_conv[l], u.reshape(bs, ts, d_conv)], axis=1)
        ext_p = jnp.pad(ext, ((0, 0), (0, ts_pad + width - 1 - ext.shape[1] + (-(ts_pad + width - 1)) % 8), (0, 0)))
        ya = _conv_sample(ext_p, w_dw, bdw, lng, lnb, l, rows=ts_pad)[:, :ts]
        q_rep = jnp.repeat(qb.reshape(bs, ts, d_b), h_b, axis=1)
        pad_new = lambda a: jnp.pad(a.reshape(bs, ts, d_b), ((0, 0), (0, ts_pad - ts), (0, 0)))
        ob = _moba_sample(page_table, q_rep, pad_new(kb), pad_new(vb), bias_own, bias_tail,
                          cache_kt, cache_vt, l, n_heads=h_b, t_new=ts)
        pad_t = lambda a: jnp.pad(a.reshape(bs, ts, -1), ((0, 0), (0, gla_rows_s - ts), (0, 0)))
        st0 = jnp.pad(jnp.swapaxes(state_gla[l], -1, -2), ((0, 0), (0, 0), (0, 0), (0, LANES - dk_c)))
        oc, stf = _gla(pad_t(gq), pad_t(gk), pad_t(gv), pad_t(gr), pad_t(lr), wa2, ba2, gg, st0, l,
                       tt=gla_rows_s, valid=ts)
        res = _merge_mlp(xs, ya.reshape(n_s, d_conv), ob.reshape(n_s, d_b), oc[:, :ts].reshape(n_s, hp),
                         gates, mods_s, g2, gf, wpw2, wpb, wpc, wo, wup, wdn, l,
                         tm=n_s, tiles_per_group=1, final=final)
        xs = res[0]
        if final:
            y_sample = res[1].reshape(bs, ts, d)
        cs_l.append(ext[:, -(width - 1):])
        ss_l.append(jnp.swapaxes(stf[..., :dk_c], -1, -2))

    rows_p = lambda a: jnp.transpose(a.reshape(depth, bp, h_b, hd_b, t), (0, 1, 4, 2, 3))
    rows_s_ = lambda a: a.reshape(depth, bs, ts, h_b, hd_b)
    return (y_prompt, y_sample, rows_p(kv_p[0]), rows_p(kv_p[1]), jnp.stack(cp_l), jnp.stack(sp_l),
            rows_s_(kv_s[0]), rows_s_(kv_s[1]), jnp.stack(cs_l), jnp.stack(ss_l))
```

```python
import functools
import math

import numpy as np
import jax
import jax.numpy as jnp
from jax import lax
from jax.experimental import pallas as pl
from jax.experimental.pallas import tpu as pltpu

F32 = jnp.float32
BF16 = jnp.bfloat16

MOBA_BLOCK = 256
MOBA_TOPK = 3
MAX_DISTANCE = 128
GLA_HEADS = 4
GLA_CHUNK = 16
GATE_NORM = 16.0
EPS = 1e-6

LANES = 128
SUBLANES = 8
VMEM_LIMIT = 56 * 1024 * 1024

NEG = -1e30
LOG2E = math.log2(math.e)
V_ROW_PAD = 16
TOKEN_TILE = 256
CONV_TILE = 512
SAMPLE_SEQS_PER_STEP = 8
PAGES_PER_CHUNK = 32
CONV_HALO = 32

_NT = (((1,), (1,)), ((), ()))


def _dot(a, b):
    return jnp.dot(a, b, preferred_element_type=F32)


def _dot_nt(a, b):
    return lax.dot_general(a, b, _NT, preferred_element_type=F32)


def _params(*sem):
    return pltpu.CompilerParams(dimension_semantics=sem, vmem_limit_bytes=VMEM_LIMIT)


def _resident(shape, index_map):
    return pl.BlockSpec(shape, index_map, pipeline_mode=pl.Buffered(1))


def _rms_rows(x, g):
    return (x * lax.rsqrt(jnp.mean(x * x, axis=-1, keepdims=True) + EPS)) * g


def _log_sigmoid(x):
    return jnp.minimum(x, 0.0) - jnp.log(1.0 + jnp.exp(-jnp.abs(x)))


def _top_k_mask(s, idx, axis, k, sentinel):
    sel = jnp.zeros(s.shape, dtype=jnp.bool_)
    for _ in range(k):
        mx = jnp.max(s, axis=axis, keepdims=True)
        first = jnp.min(jnp.where(s == mx, idx, sentinel), axis=axis, keepdims=True)
        pick = (idx == first) & (mx > -jnp.inf)
        sel = sel | pick
        s = jnp.where(pick, -jnp.inf, s)
    return sel


def _adaln_kernel(c_ref, w_ref, b_ref, o_ref):
    c = c_ref[...]
    s = c * jax.nn.sigmoid(c)
    o_ref[0] = _dot(s.astype(BF16), w_ref[0].astype(BF16)) + b_ref[0]


def _adaln(c_all, w_ada, b_ada):
    depth, d, n = w_ada.shape
    rows = c_all.shape[0]
    tn = 1536
    return pl.pallas_call(
        _adaln_kernel,
        grid=(depth, n // tn),
        in_specs=[
            pl.BlockSpec((rows, d), lambda l, j: (0, 0)),
            pl.BlockSpec((1, d, tn), lambda l, j: (l, 0, j)),
            pl.BlockSpec((1, 1, tn), lambda l, j: (l, 0, j)),
        ],
        out_specs=pl.BlockSpec((1, rows, tn), lambda l, j: (l, 0, j)),
        out_shape=jax.ShapeDtypeStruct((depth, rows, n), F32),
        compiler_params=_params("arbitrary", "arbitrary"),
        name="adaln",
    )(c_all, w_ada, b_ada.reshape(depth, 1, n))


def _inproj_kernel(x_ref, sc_ref, sh_ref, g_ref, w_ref, *rest, d_conv, d_b, hp, d_model,
                   q_scale, gq_scale, n_heads_b, attn_aux, n_alias, kv_layer):
    outs = rest[n_alias:]
    if attn_aux:
        (u_ref, q_ref, k_ref, v_ref, k16_ref, vt_ref, km_ref,
         gq_ref, gk_ref, gv_ref, gr_ref, lr_ref, gate_ref) = outs
    else:
        (u_ref, q_ref, k_ref, v_ref, gq_ref, gk_ref, gv_ref, gr_ref, lr_ref, gate_ref) = outs
    x = x_ref[...]
    h = _rms_rows(x, g_ref[0]) * (1.0 + sc_ref[0, 0, 0]) + sh_ref[0, 0, 0]
    hb = h.astype(BF16)

    pos = [0]

    def seg(width):
        lo = pos[0]
        pos[0] = lo + width
        return _dot(hb, w_ref[0, :, lo:lo + width])

    a = seg(d_conv)
    g = seg(d_conv)
    u_ref[...] = a * jax.nn.sigmoid(g)
    q_ref[...] = seg(d_b) * q_scale
    k = seg(d_b)
    v = seg(d_b)
    def put(ref, val):
        if n_alias:
            ref[0] = val
        else:
            for l in range(ref.shape[0]):
                ref[l] = val if l == kv_layer else jnp.zeros_like(val)

    if not attn_aux:
        put(k_ref, k)
        put(v_ref, v)
    else:
        v_t = v.T
        put(k_ref, k.T[None])
        put(v_ref, v_t[None])
        k16_ref[0] = k.astype(BF16)
        vt = v_t.astype(BF16)
        hd = d_b // n_heads_b
        tail = jnp.where(lax.broadcasted_iota(jnp.int32, (V_ROW_PAD, vt.shape[1]), 0) == 0, 1.0, 0.0).astype(BF16)
        for h in range(n_heads_b):
            vt_ref[0, h * (hd + V_ROW_PAD):h * (hd + V_ROW_PAD) + hd, :] = vt[h * hd:(h + 1) * hd, :]
            vt_ref[0, h * (hd + V_ROW_PAD) + hd:(h + 1) * (hd + V_ROW_PAD), :] = tail
        km_ref[0] = jnp.mean(k, axis=0, keepdims=True)
    gq_ref[...] = seg(hp) * gq_scale
    gk_ref[...] = seg(hp)
    gv_ref[...] = seg(hp)
    gr_ref[...] = seg(hp)
    lr_ref[...] = seg(LANES)
    for i in range(3):
        gate_ref[:, i * d_model:(i + 1) * d_model] = jax.nn.sigmoid(seg(d_model))


def _mod_spec(mods, idx, layer, tiles_per_group):
    return pl.BlockSpec((1, 1, 1) + mods.shape[3:], lambda i: (layer, idx, i // tiles_per_group, 0, 0))


def _inproj(x, mods, g1, w_cat, kv_bufs, layer, *, tm, tiles_per_group, dims, attn_aux):
    n, d = x.shape
    d_conv, d_b, hp, hd_b, dk_c = dims
    depth, _, nw = w_cat.shape
    grid = (n // tm,)
    row = lambda i: (i, 0)
    lay = lambda i: (layer, 0, 0)
    lead, l0 = (depth, 0) if kv_bufs is None else (1, layer)
    if attn_aux:
        tpg = tiles_per_group
        kv_shape = (depth, n // (tpg * tm), d_b, tpg * tm)
        kv_spec = pl.BlockSpec((lead, 1, d_b, tm), lambda i: (l0, i // tpg, 0, i % tpg))
    else:
        kv_shape, kv_spec = (depth, n, d_b), pl.BlockSpec((lead, tm, d_b), lambda i: (l0, i, 0))
    outs = [((n, d_conv), F32), ((n, d_b), F32), (kv_shape, F32), (kv_shape, F32)]
    specs = [pl.BlockSpec((tm, d_conv), row), pl.BlockSpec((tm, d_b), row), kv_spec, kv_spec]
    if attn_aux:
        assert tm == MOBA_BLOCK
        nblk = n // MOBA_BLOCK
        vt_rows = d_b + (d_b // hd_b) * V_ROW_PAD
        outs += [((nblk, MOBA_BLOCK, d_b), BF16), ((nblk, vt_rows, MOBA_BLOCK), BF16), ((nblk, 1, d_b), F32)]
        specs += [pl.BlockSpec((1, MOBA_BLOCK, d_b), lambda i: (i, 0, 0)),
                  pl.BlockSpec((1, vt_rows, MOBA_BLOCK), lambda i: (i, 0, 0)),
                  pl.BlockSpec((1, 1, d_b), lambda i: (i, 0, 0))]
    outs += [((n, hp), F32)] * 4 + [((n, LANES), F32), ((n, 3 * d), F32)]
    specs += [pl.BlockSpec((tm, hp), row)] * 4 + [pl.BlockSpec((tm, LANES), row), pl.BlockSpec((tm, 3 * d), row)]
    body = functools.partial(_inproj_kernel, d_conv=d_conv, d_b=d_b, hp=hp, d_model=d,
                             q_scale=hd_b ** -0.5, gq_scale=dk_c ** -0.5, n_heads_b=d_b // hd_b,
                             attn_aux=attn_aux, n_alias=0 if kv_bufs is None else 2, kv_layer=layer)
    in_specs = [
        pl.BlockSpec((tm, d), row),
        _mod_spec(mods, 1, layer, tiles_per_group),
        _mod_spec(mods, 0, layer, tiles_per_group),
        pl.BlockSpec((1, 1, d), lay),
        _resident((1, d, nw), lay),
    ]
    args = [x, mods, mods, g1, w_cat]
    aliases = {}
    if kv_bufs is not None:
        in_specs += [pl.BlockSpec(memory_space=pl.ANY)] * 2
        aliases = {len(args): 2, len(args) + 1: 3}
        args += list(kv_bufs)
    return pl.pallas_call(
        body,
        grid=grid,
        in_specs=in_specs,
        out_specs=specs,
        out_shape=[jax.ShapeDtypeStruct(s, t) for s, t in outs],
        input_output_aliases=aliases,
        compiler_params=_params("arbitrary"),
        name="inproj",
    )(*args)


def _conv_post(acc, lng_ref, lnb_ref):
    mu = jnp.mean(acc, axis=-1, keepdims=True)
    cen = acc - mu
    var = jnp.mean(cen * cen, axis=-1, keepdims=True)
    y = (cen * lax.rsqrt(var + EPS)) * lng_ref[0] + lnb_ref[0]
    return y * jax.nn.sigmoid(y)


def _conv_taps(ext_ref, start, rows, wdw_ref, bdw_ref, lng_ref, lnb_ref, width):
    acc = jnp.zeros((rows, wdw_ref.shape[-1]), F32) + bdw_ref[0]
    for k in range(width):
        acc = acc + ext_ref[pl.ds(start + k, rows), :] * wdw_ref[0, k:k + 1, :]
    return _conv_post(acc, lng_ref, lnb_ref)


def _conv_taps_aligned(ext_ref, start, rows, wdw_ref, bdw_ref, lng_ref, lnb_ref, width, shift_scr, out_ref):
    span = shift_scr.shape[1]
    for r in range(1, SUBLANES):
        shift_scr[r - 1] = ext_ref[pl.ds(r, span), :]
    acc = jnp.zeros((rows, wdw_ref.shape[-1]), F32) + bdw_ref[0]
    for k in range(width):
        a, r = divmod(start + k, SUBLANES)
        src = ext_ref if r == 0 else shift_scr.at[r - 1]
        acc = acc + src[pl.ds(a * SUBLANES, rows), :] * wdw_ref[0, k:k + 1, :]
    out_ref[...] = _conv_post(acc, lng_ref, lnb_ref).astype(out_ref.dtype)


def _conv_kernel(u_ref, hist_ref, wdw_ref, bdw_ref, lng_ref, lnb_ref, y_ref, tail_ref, ext_scr, shift_scr,
                 *, tt, width):
    j = pl.program_id(1)

    @pl.when(j == 0)
    def _():
        ext_scr[0:CONV_HALO, :] = hist_ref[0]

    @pl.when(j > 0)
    def _():
        ext_scr[0:CONV_HALO, :] = ext_scr[tt:tt + CONV_HALO, :]

    ext_scr[CONV_HALO:CONV_HALO + tt, :] = u_ref[0]
    start = CONV_HALO - (width - 1)
    _conv_taps_aligned(ext_scr, start, tt, wdw_ref, bdw_ref, lng_ref, lnb_ref, width, shift_scr, y_ref.at[0])

    @pl.when(j == pl.num_programs(1) - 1)
    def _():
        tail_ref[0] = ext_scr[tt:tt + CONV_HALO, :]


def _conv_prompt(u, hist, w_dw, b_dw, ln_g, ln_b, layer, *, tt):
    b, t, c = u.shape
    width = w_dw.shape[1]
    lay = lambda bi, j: (layer, 0, 0)
    return pl.pallas_call(
        functools.partial(_conv_kernel, tt=tt, width=width),
        grid=(b, t // tt),
        in_specs=[
            pl.BlockSpec((1, tt, c), lambda bi, j: (bi, j, 0)),
            pl.BlockSpec((1, CONV_HALO, c), lambda bi, j: (bi, 0, 0)),
            pl.BlockSpec((1, width, c), lay),
            pl.BlockSpec((1, 1, c), lay),
            pl.BlockSpec((1, 1, c), lay),
            pl.BlockSpec((1, 1, c), lay),
        ],
        out_specs=[pl.BlockSpec((1, tt, c), lambda bi, j: (bi, j, 0)),
                   pl.BlockSpec((1, CONV_HALO, c), lambda bi, j: (bi, 0, 0))],
        out_shape=[jax.ShapeDtypeStruct((b, t, c), BF16), jax.ShapeDtypeStruct((b, CONV_HALO, c), F32)],
        scratch_shapes=[pltpu.VMEM((CONV_HALO + tt, c), F32),
                        pltpu.VMEM((SUBLANES - 1, CONV_HALO + tt - SUBLANES, c), F32)],
        compiler_params=_params("arbitrary", "arbitrary"),
        name="conv_prompt",
    )(u, hist, w_dw, b_dw, ln_g, ln_b)


def _conv_small_kernel(ext_ref, wdw_ref, bdw_ref, lng_ref, lnb_ref, y_ref, *, rows, width):
    for i in range(ext_ref.shape[0]):
        y_ref[i] = _conv_taps(ext_ref.at[i], 0, rows, wdw_ref, bdw_ref, lng_ref, lnb_ref, width).astype(BF16)


def _conv_sample(ext, w_dw, b_dw, ln_g, ln_b, layer, *, rows):
    b, r, c = ext.shape
    width = w_dw.shape[1]
    per_step = math.gcd(b, SAMPLE_SEQS_PER_STEP)
    lay = lambda bi: (layer, 0, 0)
    return pl.pallas_call(
        functools.partial(_conv_small_kernel, rows=rows, width=width),
        grid=(b // per_step,),
        in_specs=[
            pl.BlockSpec((per_step, r, c), lambda bi: (bi, 0, 0)),
            pl.BlockSpec((1, width, c), lay),
            pl.BlockSpec((1, 1, c), lay),
            pl.BlockSpec((1, 1, c), lay),
            pl.BlockSpec((1, 1, c), lay),
        ],
        out_specs=pl.BlockSpec((per_step, rows, c), lambda bi: (bi, 0, 0)),
        out_shape=jax.ShapeDtypeStruct((b, rows, c), BF16),
        compiler_params=_params("arbitrary"),
        name="conv_sample",
    )(ext, w_dw, b_dw, ln_g, ln_b)


def _moba_prompt_kernel(q_ref, k16_ref, vt_ref, km_ref, bias_ref, o_ref,
                        qm_scr, sel_scr, m_scr, acc_scr, s2_scr, *, nb, n_heads, hd):
    i = pl.program_id(1)
    blk = MOBA_BLOCK
    q = q_ref[0]
    km = km_ref[0]
    lane = lax.broadcasted_iota(jnp.int32, (blk, LANES), 1)
    blk_idx = lax.broadcasted_iota(jnp.int32, (nb, blk), 0)
    heads_per_slab = LANES // hd

    km16 = km.astype(BF16)
    for h in range(n_heads):
        p, w = divmod(h, heads_per_slab)
        qm = jnp.where((lane // hd) == w, q[:, p * LANES:(p + 1) * LANES], 0.0)
        qm_scr[h] = (qm * LOG2E).astype(BF16)
        s = _dot_nt(km16[:, p * LANES:(p + 1) * LANES], qm.astype(BF16))
        s = jnp.where(blk_idx < i, s, -jnp.inf)
        sel = _top_k_mask(s, blk_idx, 0, MOBA_TOPK, nb)
        sel_scr[h] = jnp.where(sel, 1.0, 0.0)

    hv = vt_ref.shape[1] // n_heads

    def block(n, kind):
        kb = k16_ref[n]
        vb = vt_ref[n]

        for h in range(n_heads):
            p = h // heads_per_slab
            s2_scr[h, 0:blk, :] =_dot_nt(kb[:, p * LANES:(p + 1) * LANES], qm_scr[h])
        for h in range(n_heads):
            s = s2_scr[h, 0:blk, :]
            rows = slice(h * hv, (h + 1) * hv)
            if kind == "own":
                s = s + bias_ref[h, 0]
                m = jnp.max(s, axis=0, keepdims=True)
                m_scr[h:h + 1, :] = m
                acc_scr[rows, :] = _dot(vb[rows, :], jnp.exp2(s - m).astype(BF16))
            else:
                picked = sel_scr[h, pl.ds(n, 1), :] > 0.5
                m_old = m_scr[h:h + 1, :]
                m = jnp.where(picked, jnp.maximum(m_old, jnp.max(s, axis=0, keepdims=True)), m_old)
                alpha = jnp.exp2(m_old - m)
                pr = jnp.exp2(s - jnp.where(picked, m, -NEG))
                m_scr[h:h + 1, :] = m
                acc_scr[rows, :] = alpha * acc_scr[rows, :] + _dot(vb[rows, :], pr.astype(BF16))

    def pair_logits(n, buf):
        kb = k16_ref[pl.ds(n, 2)].reshape(2 * blk, k16_ref.shape[-1])
        for h in range(n_heads):
            p = h // heads_per_slab
            buf[h] = _dot_nt(kb[:, p * LANES:(p + 1) * LANES], qm_scr[h])

    @pl.when(i == 0)
    def _():
        block(0, "own")

    @pl.when(i >= 1)
    def _():
        n = i - 1
        pair_logits(n, s2_scr)
        vb = jnp.concatenate([vt_ref[n], vt_ref[n + 1]], axis=1)
        for h in range(n_heads):
            rows = slice(h * hv, (h + 1) * hv)
            s_a = s2_scr[h, 0:blk, :] + bias_ref[h, 1]
            s_b = s2_scr[h, blk:2 * blk, :] + bias_ref[h, 0]
            pick_a = sel_scr[h, pl.ds(n, 1), :] > 0.5
            m = jnp.maximum(jnp.where(pick_a, jnp.max(s_a, axis=0, keepdims=True), NEG),
                            jnp.max(s_b, axis=0, keepdims=True))
            pr = jnp.concatenate([jnp.exp2(s_a - jnp.where(pick_a, m, -NEG)).astype(BF16),
                                  jnp.exp2(s_b - m).astype(BF16)], axis=0)
            m_scr[h:h + 1, :] = m
            acc_scr[rows, :] = _dot(vb[rows, :], pr)

    def pair_update(n, buf):
        vb = jnp.concatenate([vt_ref[n], vt_ref[n + 1]], axis=1)
        for h in range(n_heads):
            rows = slice(h * hv, (h + 1) * hv)
            s_a = buf[h, 0:blk, :]
            s_b = buf[h, blk:2 * blk, :]
            pick_a = sel_scr[h, pl.ds(n, 1), :] > 0.5
            pick_b = sel_scr[h, pl.ds(n + 1, 1), :] > 0.5
            m_old = m_scr[h:h + 1, :]
            m = jnp.maximum(m_old, jnp.maximum(
                jnp.where(pick_a, jnp.max(s_a, axis=0, keepdims=True), NEG),
                jnp.where(pick_b, jnp.max(s_b, axis=0, keepdims=True), NEG)))
            alpha = jnp.exp2(m_old - m)
            pr = jnp.concatenate([jnp.exp2(s_a - jnp.where(pick_a, m, -NEG)).astype(BF16),
                                  jnp.exp2(s_b - jnp.where(pick_b, m, -NEG)).astype(BF16)], axis=0)
            m_scr[h:h + 1, :] = m
            acc_scr[rows, :] = alpha * acc_scr[rows, :] + _dot(vb[rows, :], pr)

    n_far = jnp.maximum(i - 1, 0)

    def far(t, carry):
        pair_logits(2 * t, s2_scr)
        pair_update(2 * t, s2_scr)
        return carry

    lax.fori_loop(0, n_far // 2, far, 0)

    @pl.when(n_far % 2 == 1)
    def _():
        block(n_far - 1, "far")

    out = [acc_scr[h * hv:h * hv + hd, :] / acc_scr[h * hv + hd:h * hv + hd + 1, :] for h in range(n_heads)]
    o_ref[0] = jnp.concatenate(out, axis=0).T.astype(BF16)


def _moba_prompt(q, k16, vt, km, bias_t, *, batch, n_heads, hd):
    n, d_b = q.shape
    t = n // batch
    nb = t // MOBA_BLOCK
    blk = MOBA_BLOCK
    body = functools.partial(_moba_prompt_kernel, nb=nb, n_heads=n_heads, hd=hd)
    return pl.pallas_call(
        body,
        grid=(batch, nb),
        in_specs=[
            pl.BlockSpec((1, blk, d_b), lambda b, i: (b * nb + i, 0, 0)),
            pl.BlockSpec((nb, blk, d_b), lambda b, i: (b, 0, 0)),
            pl.BlockSpec((nb, vt.shape[1], blk), lambda b, i: (b, 0, 0)),
            pl.BlockSpec((1, nb, d_b), lambda b, i: (b, 0, 0)),
            _resident((n_heads, 2, blk, blk), lambda b, i: (0, 0, 0, 0)),
        ],
        out_specs=pl.BlockSpec((1, blk, d_b), lambda b, i: (b * nb + i, 0, 0)),
        out_shape=jax.ShapeDtypeStruct((n // blk, blk, d_b), BF16),
        scratch_shapes=[
            pltpu.VMEM((n_heads, blk, LANES), BF16),
            pltpu.VMEM((n_heads, nb, blk), F32),
            pltpu.VMEM((n_heads, blk), F32),
            pltpu.VMEM((vt.shape[1], blk), F32),
            pltpu.VMEM((n_heads, 2 * blk, blk), F32),
        ],
        compiler_params=_params("arbitrary", "arbitrary"),
        name="moba_prompt",
    )(q.reshape(n // blk, blk, d_b), k16, vt, km.reshape(batch, nb, d_b), bias_t)


def _moba_sample_kernel(pt_ref, q_ref, kn_ref, vn_ref, bown_ref, btail_ref, kt_hbm, vt_hbm, o_ref,
                        s_scr, kv_scr, kbuf, vbuf, ksem, vsem,
                        *, layer, n_chunks, n_heads, t_new, page, ppc):
    b = pl.program_id(0)
    rows, d_b = q_ref.shape[1:]
    hd = d_b // n_heads
    blocks_per_chunk = ppc * page // MOBA_BLOCK
    pages_per_block = MOBA_BLOCK // page
    nbp = n_chunks * blocks_per_chunk

    def page_copy(hbm, buf, sem, seq, chunk, p):
        slot = chunk % 2
        return pltpu.make_async_copy(hbm.at[layer, pt_ref[seq, chunk * ppc + p]], buf.at[slot, p], sem.at[slot])

    def start_chunk(hbm, buf, sem, seq, chunk):
        for p in range(ppc):
            page_copy(hbm, buf, sem, seq, chunk, p).start()

    def wait_chunk(hbm, buf, sem, seq, chunk):
        for p in range(ppc):
            page_copy(hbm, buf, sem, seq, chunk, p).wait()

    @pl.when(b == 0)
    def _():
        start_chunk(kt_hbm, kbuf, ksem, b, 0)

    row_i = lax.broadcasted_iota(jnp.int32, (rows, d_b), 0)
    lane_i = lax.broadcasted_iota(jnp.int32, (rows, d_b), 1)
    head_mask = (lane_i // hd) == (row_i % n_heads)
    blk_lane = lax.broadcasted_iota(jnp.int32, (rows, LANES), 1)
    qb16 = jnp.where(head_mask, q_ref[0], 0.0).astype(BF16)

    bsum = jnp.zeros((rows, LANES), F32)
    for c in range(n_chunks):
        if c + 1 < n_chunks:
            start_chunk(kt_hbm, kbuf, ksem, b, c + 1)
        else:
            start_chunk(vt_hbm, vbuf, vsem, b, 0)
        wait_chunk(kt_hbm, kbuf, ksem, b, c)
        for p in range(ppc):
            s = _dot(qb16, kbuf[c % 2, p].astype(BF16))
            s_scr[c, :, p * page:(p + 1) * page] = s
            ps = jnp.sum(s, axis=-1, keepdims=True)
            psum = ps if p % pages_per_block == 0 else psum + ps
            if p % pages_per_block == pages_per_block - 1:
                bsum = jnp.where(blk_lane == c * blocks_per_chunk + p // pages_per_block, psum, bsum)

    bs = jnp.where(blk_lane < nbp, bsum, -jnp.inf)
    sel = _top_k_mask(bs, blk_lane, 1, MOBA_TOPK, LANES)
    blk_bias = jnp.where(sel, 0.0, NEG)

    kv_scr[...] = jnp.zeros(kv_scr.shape, F32)
    kv_scr[0:kn_ref.shape[1], :] = kn_ref[0]
    s_own = _dot_nt(qb16, kv_scr[...].astype(BF16)) + bown_ref[...]
    m = jnp.max(s_own, axis=-1, keepdims=True)
    for blk in range(nbp):
        cc, off = divmod(blk, blocks_per_chunk)
        cols = slice(off * MOBA_BLOCK, (off + 1) * MOBA_BLOCK)
        s = s_scr[cc, :, cols] + blk_bias[:, blk:blk + 1]
        if blk == nbp - 1:
            s = s + btail_ref[...]
        s_scr[cc, :, cols] = s
        m = jnp.maximum(m, jnp.max(s, axis=-1, keepdims=True))

    p_own = jnp.exp(s_own - m)
    l = jnp.sum(p_own, axis=-1, keepdims=True)
    for cc in range(n_chunks):
        pr = jnp.exp(s_scr[cc] - m)
        l = l + jnp.sum(pr, axis=-1, keepdims=True)
        s_scr[cc] = pr
    kv_scr[0:vn_ref.shape[1], :] = vn_ref[0]
    acc = _dot(p_own.astype(BF16), kv_scr[...].astype(BF16))

    for c in range(n_chunks):
        if c + 1 < n_chunks:
            start_chunk(vt_hbm, vbuf, vsem, b, c + 1)
        else:
            @pl.when(b + 1 < pl.num_programs(0))
            def _():
                start_chunk(kt_hbm, kbuf, ksem, b + 1, 0)
        wait_chunk(vt_hbm, vbuf, vsem, b, c)
        for p in range(ppc):
            pr = s_scr[c, :, p * page:(p + 1) * page]
            acc = acc + _dot_nt(pr.astype(BF16), vbuf[c % 2, p].astype(BF16))

    o = jnp.where(head_mask, acc / l, 0.0)
    o_ref[0] = jnp.sum(o.reshape(t_new, n_heads, d_b), axis=1).astype(BF16)


def _moba_sample(page_table, q_rep, k_new, v_new, bias_own, bias_tail, cache_kt, cache_vt, layer,
                 *, n_heads, t_new):
    b, rows, d_b = q_rep.shape
    page = cache_kt.shape[-1]
    n_pages = page_table.shape[1]
    ppc = min(PAGES_PER_CHUNK, n_pages)
    n_chunks = n_pages // ppc
    assert n_pages % ppc == 0 and (ppc * page) % MOBA_BLOCK == 0 and MOBA_BLOCK % page == 0
    chunk = ppc * page

    per_b = lambda bi, pt: (bi, 0, 0)
    const2 = lambda bi, pt: (0, 0)
    in_specs = [
        pl.BlockSpec((1, rows, d_b), per_b),
        pl.BlockSpec((1, k_new.shape[1], d_b), per_b),
        pl.BlockSpec((1, v_new.shape[1], d_b), per_b),
        pl.BlockSpec(bias_own.shape, const2),
        pl.BlockSpec(bias_tail.shape, const2),
        pl.BlockSpec(memory_space=pl.ANY),
        pl.BlockSpec(memory_space=pl.ANY),
    ]
    body = functools.partial(_moba_sample_kernel, layer=layer, n_chunks=n_chunks, n_heads=n_heads,
                             t_new=t_new, page=page, ppc=ppc)
    grid_spec = pltpu.PrefetchScalarGridSpec(
        num_scalar_prefetch=1,
        grid=(b,),
        in_specs=in_specs,
        out_specs=pl.BlockSpec((1, t_new, d_b), per_b),
        scratch_shapes=[
            pltpu.VMEM((n_chunks, rows, chunk), F32),
            pltpu.VMEM((LANES, d_b), F32),
            pltpu.VMEM((2, ppc, d_b, page), F32),
            pltpu.VMEM((2, ppc, d_b, page), F32),
            pltpu.SemaphoreType.DMA((2,)),
            pltpu.SemaphoreType.DMA((2,)),
        ],
    )
    return pl.pallas_call(
        body,
        grid_spec=grid_spec,
        out_shape=jax.ShapeDtypeStruct((b, t_new, d_b), BF16),
        compiler_params=_params("arbitrary"),
        name="moba_sample",
    )(page_table, q_rep, k_new, v_new, bias_own, bias_tail, cache_kt, cache_vt)


def _gla_kernel(gq_ref, gk_ref, gv_ref, gr_ref, lr_ref, wa_ref, ba_ref, gg_ref, st0_ref,
                o_ref, stf_ref, st_scr, bc_scr, bl_scr, qd_scr, kd_scr, o_scr,
                *, tt, valid, n_heads):
    j = pl.program_id(1)
    c_sz = GLA_CHUNK
    hp = n_heads * LANES

    @pl.when(j == 0)
    def _():
        st_scr[...] = st0_ref[0]

    x = _dot(lr_ref[0].astype(BF16), wa_ref[0]) + ba_ref[0]
    la = _log_sigmoid(x) * (1.0 / GATE_NORM)
    if valid < tt:
        la = jnp.where(lax.broadcasted_iota(jnp.int32, (tt, hp), 0) < valid, la, 0.0)
    r_i = lax.broadcasted_iota(jnp.int32, (tt, tt), 0)
    c_i = lax.broadcasted_iota(jnp.int32, (tt, tt), 1)
    low = jnp.where(((r_i // c_sz) == (c_i // c_sz)) & (c_i <= r_i), 1.0, 0.0).astype(BF16)
    la_hi = la.astype(BF16)
    la_lo = (la - la_hi.astype(F32)).astype(BF16)
    bc = _dot(low, la_hi) + _dot(low, la_lo)
    bc_scr[...] = bc
    n_ch = tt // c_sz
    in_ch = (lax.broadcasted_iota(jnp.int32, (n_ch, tt), 1) // c_sz
             == lax.broadcasted_iota(jnp.int32, (n_ch, tt), 0))
    tot = jnp.where(in_ch, 1.0, 0.0).astype(BF16)
    ends = _dot(tot, la_hi) + _dot(tot, la_lo)
    bl_scr[...] = ends
    bl = jnp.broadcast_to(ends[:, None, :], (n_ch, c_sz, hp)).reshape(tt, hp)
    qd_scr[...] = (gq_ref[0] * jnp.exp(bc)).astype(BF16)
    kd_scr[...] = (gk_ref[0] * jnp.exp(bl - bc)).astype(BF16)

    half = c_sz // 2
    t_half = lax.broadcasted_iota(jnp.int32, (half, LANES), 0)

    def chunk(c, carry):
        rows = pl.ds(c * c_sz, c_sz)
        bc_c = bc_scr[rows, :]
        q_c = gq_ref[0, rows, :]
        k_c = gk_ref[0, rows, :]
        v_c = gv_ref[0, rows, :]
        decay = jnp.exp(bl_scr[pl.ds(c, 1), :])
        for h in range(n_heads):
            hs = slice(h * LANES, (h + 1) * LANES)
            st = st_scr[h]
            o_h = _dot_nt(qd_scr[rows, hs], st.astype(BF16))
            bch = bc_c[:, hs]
            qh = q_c[:, hs]
            parts = [o_h[0:half], o_h[half:c_sz]]
            for s in range(c_sz):
                row_s = pl.ds(c * c_sz + s, 1)
                b_s = jnp.broadcast_to(bc_scr[row_s, hs], (half, LANES))
                k_s = jnp.broadcast_to(gk_ref[0, row_s, hs], (half, LANES))
                v_s = jnp.broadcast_to(gv_ref[0, row_s, hs], (half, LANES))
                for hi in range(2):
                    r0 = hi * half
                    if s >= r0 + half:
                        continue
                    diff = bch[r0:r0 + half] - b_s
                    if s > r0:
                        diff = jnp.where(t_half + r0 >= s, diff, -jnp.inf)
                    a_s = jnp.sum(qh[r0:r0 + half] * k_s * jnp.exp(diff), axis=-1, keepdims=True)
                    parts[hi] = parts[hi] + a_s * v_s
            o_scr[rows, hs] = jnp.concatenate(parts, axis=0)
            kvt = lax.dot_general(v_c[:, hs].astype(BF16), kd_scr[rows, hs], (((0,), (0,)), ((), ())),
                                  preferred_element_type=F32)
            st_scr[h] = decay[:, hs] * st + kvt
        return carry

    for c in range((valid + c_sz - 1) // c_sz):
        chunk(c, 0)

    o = o_scr[...]
    r = gr_ref[0]
    for h in range(n_heads):
        hs = slice(h * LANES, (h + 1) * LANES)
        oh = o[:, hs]
        oh = oh * lax.rsqrt(jnp.mean(oh * oh, axis=-1, keepdims=True) + EPS)
        rh = r[:, hs]
        o_ref[0, :, hs] = ((oh * gg_ref[0, :, hs]) * (rh * jax.nn.sigmoid(rh))).astype(BF16)

    @pl.when(j == pl.num_programs(1) - 1)
    def _():
        stf_ref[0] = st_scr[...]


def _gla(gq, gk, gv, gr, lr, wa, ba, gg, st0, layer, *, tt, valid):
    b, t, hp = gq.shape
    n_heads = hp // LANES
    lay = lambda bi, j: (layer, 0, 0)
    tile = lambda bi, j: (bi, j, 0)
    per_b = lambda bi, j: (bi, 0, 0, 0)
    body = functools.partial(_gla_kernel, tt=tt, valid=valid, n_heads=n_heads)
    return pl.pallas_call(
        body,
        grid=(b, t // tt),
        in_specs=[pl.BlockSpec((1, tt, hp), tile)] * 4 + [
            pl.BlockSpec((1, tt, LANES), tile),
            pl.BlockSpec((1, LANES, hp), lay),
            pl.BlockSpec((1, 1, hp), lay),
            pl.BlockSpec((1, 1, hp), lay),
            pl.BlockSpec((1, n_heads, LANES, LANES), per_b),
        ],
        out_specs=[pl.BlockSpec((1, tt, hp), tile), pl.BlockSpec((1, n_heads, LANES, LANES), per_b)],
        out_shape=[jax.ShapeDtypeStruct((b, t, hp), BF16),
                   jax.ShapeDtypeStruct((b, n_heads, LANES, LANES), F32)],
        scratch_shapes=[
            pltpu.VMEM((n_heads, LANES, LANES), F32),
            pltpu.VMEM((tt, hp), F32),
            pltpu.VMEM((tt // GLA_CHUNK, hp), F32),
            pltpu.VMEM((tt, hp), BF16),
            pltpu.VMEM((tt, hp), BF16),
            pltpu.VMEM((tt, hp), F32),
        ],
        compiler_params=_params("arbitrary", "arbitrary"),
        name="gla",
    )(gq, gk, gv, gr, lr, wa, ba, gg, st0)


def _merge_mlp_kernel(x_ref, ya_ref, ob_ref, oc_ref, gate_ref, gt1_ref, sc2_ref, sh2_ref, gt2_ref,
                      g2_ref, gf_ref, wa_ref, wb_ref, wc_ref, wo_ref, wup_ref, wdn_ref, *outs, final):
    d = x_ref.shape[-1]
    m = (gate_ref[:, 0:d] * _dot(ya_ref[...], wa_ref[0])
         + gate_ref[:, d:2 * d] * _dot(ob_ref[...], wb_ref[0])
         + gate_ref[:, 2 * d:3 * d] * _dot(oc_ref[...], wc_ref[0]))
    x1 = x_ref[...] + gt1_ref[0, 0, 0] * _dot(m.astype(BF16), wo_ref[0])
    h2 = _rms_rows(x1, g2_ref[0]) * (1.0 + sc2_ref[0, 0, 0]) + sh2_ref[0, 0, 0]
    up = jnp.maximum(_dot(h2.astype(BF16), wup_ref[0]), 0.0)
    x2 = x1 + gt2_ref[0, 0, 0] * _dot((up * up).astype(BF16), wdn_ref[0])
    outs[0][...] = x2
    if final:
        outs[1][...] = _rms_rows(x2, gf_ref[...])


def _merge_mlp(x, ya, ob, oc, gates, mods, g2, gf, wa, wb, wc, wo, wup, wdn, layer,
               *, tm, tiles_per_group, final):
    n, d = x.shape
    c = ya.shape[-1]
    dff = wup.shape[-1]
    row = lambda i: (i, 0)
    lay = lambda i: (layer, 0, 0)
    n_out = 2 if final else 1
    return pl.pallas_call(
        functools.partial(_merge_mlp_kernel, final=final),
        grid=(n // tm,),
        in_specs=[pl.BlockSpec((tm, d), row)] + [pl.BlockSpec((tm, c), row)] * 3 + [
            pl.BlockSpec((tm, 3 * d), row),
            _mod_spec(mods, 2, layer, tiles_per_group), _mod_spec(mods, 4, layer, tiles_per_group),
            _mod_spec(mods, 3, layer, tiles_per_group), _mod_spec(mods, 5, layer, tiles_per_group),
            pl.BlockSpec((1, 1, d), lay),
            pl.BlockSpec((1, d), lambda i: (0, 0)),
            _resident((1, c, d), lay), _resident((1, c, d), lay), _resident((1, c, d), lay),
            _resident((1, d, d), lay), _resident((1, d, dff), lay), _resident((1, dff, d), lay),
        ],
        out_specs=[pl.BlockSpec((tm, d), row)] * n_out,
        out_shape=[jax.ShapeDtypeStruct((n, d), F32)] * n_out,
        compiler_params=_params("arbitrary"),
        name="merge_mlp",
    )(x, ya, ob, oc, gates, mods, mods, mods, mods, g2, gf, wa, wb, wc, wo, wup, wdn)


def _t5_bucket_table(n_buckets, max_dist):
    dist = np.arange(max_dist + 1)
    max_exact = n_buckets // 2
    d = np.maximum(dist, 1).astype(np.float32)
    large = max_exact + (np.log(d / max_exact) / math.log(MAX_DISTANCE / max_exact)
                         * (n_buckets - max_exact)).astype(np.int32)
    large = np.minimum(large, n_buckets - 1)
    return np.where(dist < max_exact, dist, large).astype(np.int32)


def _pad_heads(w, n_heads):
    k = w.shape[-1] // n_heads
    w = w.reshape(w.shape[:-1] + (n_heads, k))
    w = jnp.pad(w, [(0, 0)] * (w.ndim - 1) + [(0, LANES - k)])
    return w.reshape(w.shape[:-2] + (n_heads * LANES,))


def kernel(x_prompt, x_sample, c_prompt, c_sample, cache_k, cache_v, page_table, state_conv, state_gla,
           w_ada, b_ada, g_norm1, w_in, w_dw, b_dw, ln_g, ln_b, w_pw2, w_pb, rel_bias, w_a2, b_a, g_gla,
           w_pc, w_o, g_norm2, w_up, w_down, g_final):
    bp, t, d = x_prompt.shape
    bs, ts, _ = x_sample.shape
    depth = w_in.shape[0]
    _, n_pool, page, h_b, hd_b = cache_k.shape
    d_b = h_b * hd_b
    d_conv = w_dw.shape[-1]
    width = w_dw.shape[1]
    rank = w_a2.shape[1]
    h_c = GLA_HEADS
    dk_c = w_a2.shape[-1] // h_c
    dv_c = state_gla.shape[-1]
    hp = h_c * LANES
    n_buckets = rel_bias.shape[0]
    n_pages = page_table.shape[1]
    past_len = n_pages * page
    assert dv_c == LANES and h_c * dv_c == hp and width - 1 <= CONV_HALO

    sizes = (2 * d_conv, d_b, d_b, d_b, h_c * dk_c, h_c * dk_c, h_c * dv_c, h_c * dv_c, rank, d, d, d)
    offs = np.cumsum((0,) + sizes)
    part = [w_in[:, :, offs[i]:offs[i + 1]] for i in range(len(sizes))]
    w_cat = jnp.concatenate([
        part[0], part[1], part[2], part[3],
        _pad_heads(part[4], h_c), _pad_heads(part[5], h_c), part[6], part[7],
        jnp.pad(part[8], ((0, 0), (0, 0), (0, LANES - rank))),
        part[9], part[10], part[11]], axis=-1).astype(BF16)
    wa2 = jnp.pad(_pad_heads(w_a2, h_c), ((0, 0), (0, LANES - rank), (0, 0))).astype(BF16)
    ba2 = _pad_heads(b_a, h_c).reshape(depth, 1, hp)
    gg = g_gla.reshape(depth, 1, hp)
    wpw2, wpb, wpc, wo, wup, wdn = (w.astype(BF16) for w in (w_pw2, w_pb, w_pc, w_o, w_up, w_down))
    g1 = g_norm1.reshape(depth, 1, d)
    g2 = g_norm2.reshape(depth, 1, d)
    gf = g_final.reshape(1, d)
    bdw = b_dw.reshape(depth, 1, d_conv)
    lng = ln_g.reshape(depth, 1, d_conv)
    lnb = ln_b.reshape(depth, 1, d_conv)

    n_c = bp + bs
    rows_c = -(-n_c // 8) * 8
    c_all = jnp.pad(jnp.concatenate([c_prompt, c_sample], axis=0), ((0, rows_c - n_c), (0, 0)))
    mods = _adaln(c_all, w_ada, b_ada).reshape(depth, rows_c, 6, d)

    blk = MOBA_BLOCK
    assert ts <= blk and past_len % blk == 0
    bucket = _t5_bucket_table(n_buckets, 2 * blk - 1)
    period = 2 * blk + LANES
    onehot = np.zeros((period, n_buckets), np.float32)
    onehot[np.arange(2 * blk), bucket] = 1.0
    by_dist = jnp.dot(onehot, rel_bias - rel_bias[n_buckets - 1:n_buckets, :],
                      precision=lax.Precision.HIGHEST).T
    toep = jnp.tile(by_dist, (1, blk))[:, :blk * (period - 1)].reshape(h_b, blk, period - 1)
    causal = np.arange(blk)[:, None] <= np.arange(blk)[None, :]
    bias_t = jnp.stack([jnp.where(causal, toep[:, :, :blk] * LOG2E, NEG), toep[:, :, blk:2 * blk] * LOG2E],
                       axis=1)
    rows_s = ts * h_b
    own = jnp.transpose(toep[:, :ts, :ts], (2, 0, 1)).reshape(rows_s, ts)
    own_ok = np.repeat(np.arange(ts), h_b)[:, None] >= np.arange(LANES)[None, :]
    bias_own = jnp.where(own_ok, jnp.pad(own, ((0, 0), (0, LANES - ts))), NEG)
    bias_tail = jnp.transpose(toep[:, :, blk:blk + ts], (2, 0, 1)).reshape(rows_s, blk)
    assert past_len // blk <= LANES
    cache_kt = jnp.transpose(cache_k, (0, 1, 3, 4, 2)).reshape(depth, n_pool, d_b, page)
    cache_vt = jnp.transpose(cache_v, (0, 1, 3, 4, 2)).reshape(depth, n_pool, d_b, page)

    xp = x_prompt.reshape(bp * t, d)
    xs = x_sample.reshape(bs * ts, d)
    n_s = bs * ts
    tm = TOKEN_TILE
    tpb = t // tm
    dims = (d_conv, d_b, hp, hd_b, dk_c)
    ts_pad = 8
    gla_rows_s = GLA_CHUNK

    cp_l, sp_l, cs_l, ss_l = ([] for _ in range(4))
    y_prompt = y_sample = None
    zeros_hist = jnp.zeros((bp, CONV_HALO, d_conv), F32)
    zeros_state = jnp.zeros((bp, h_c, LANES, LANES), F32)
    mods = jnp.transpose(mods, (0, 2, 1, 3))
    mods_p = mods[:, :, :bp].reshape(depth, 6, bp, 1, d)
    mods_s = jnp.repeat(mods[:, :, bp:n_c], ts, axis=2).reshape(depth, 6, 1, n_s, d)
    kv_p = kv_s = None
    for l in range(depth):
        final = l == depth - 1

        (u, qb, kb, vb, k16, vt, km, gq, gk, gv, gr, lr, gates) = _inproj(
            xp, mods_p, g1, w_cat, kv_p, l, tm=tm, tiles_per_group=tpb, dims=dims, attn_aux=True)
        kv_p = (kb, vb)
        ya, tail = _conv_prompt(u.reshape(bp, t, d_conv), zeros_hist, w_dw, bdw, lng, lnb, l,
                                tt=math.gcd(t, CONV_TILE))
        ob = _moba_prompt(qb, k16, vt, km, bias_t, batch=bp, n_heads=h_b, hd=hd_b)
        oc, stf = _gla(*(a.reshape(bp, t, -1) for a in (gq, gk, gv, gr, lr)), wa2, ba2, gg, zeros_state, l,
                       tt=tm, valid=tm)
        res = _merge_mlp(xp, ya.reshape(bp * t, d_conv), ob.reshape(bp * t, d_b), oc.reshape(bp * t, hp),
                         gates, mods_p, g2, gf, wpw2, wpb, wpc, wo, wup, wdn, l,
                         tm=tm, tiles_per_group=tpb, final=final)
        xp = res[0]
        if final:
            y_prompt = res[1].reshape(bp, t, d)
        cp_l.append(tail[:, CONV_HALO - (width - 1):])
        sp_l.append(jnp.swapaxes(stf[..., :dk_c], -1, -2))

        (u, qb, kb, vb, gq, gk, gv, gr, lr, gates) = _inproj(
            xs, mods_s, g1, w_cat, kv_s, l, tm=n_s, tiles_per_group=1, dims=dims, attn_aux=False)
        kv_s = (kb, vb)
        kb, vb = kb[l], vb[l]
        ext = jnp.concatenate([state_conv[l], u.reshape(bs, ts, d_conv)], axis=1)
        ext_p = jnp.pad(ext, ((0, 0), (0, ts_pad + width - 1 - ext.shape[1] + (-(ts_pad + width - 1)) % 8), (0, 0)))
        ya = _conv_sample(ext_p, w_dw, bdw, lng, lnb, l, rows=ts_pad)[:, :ts]
        q_rep = jnp.repeat(qb.reshape(bs, ts, d_b), h_b, axis=1)
        pad_new = lambda a: jnp.pad(a.reshape(bs, ts, d_b), ((0, 0), (0, ts_pad - ts), (0, 0)))
        ob = _moba_sample(page_table, q_rep, pad_new(kb), pad_new(vb), bias_own, bias_tail,
                          cache_kt, cache_vt, l, n_heads=h_b, t_new=ts)
        pad_t = lambda a: jnp.pad(a.reshape(bs, ts, -1), ((0, 0), (0, gla_rows_s - ts), (0, 0)))
        st0 = jnp.pad(jnp.swapaxes(state_gla[l], -1, -2), ((0, 0), (0, 0), (0, 0), (0, LANES - dk_c)))
        oc, stf = _gla(pad_t(gq), pad_t(gk), pad_t(gv), pad_t(gr), pad_t(lr), wa2, ba2, gg, st0, l,
                       tt=gla_rows_s, valid=ts)
        res = _merge_mlp(xs, ya.reshape(n_s, d_conv), ob.reshape(n_s, d_b), oc[:, :ts].reshape(n_s, hp),
                         gates, mods_s, g2, gf, wpw2, wpb, wpc, wo, wup, wdn, l,
                         tm=n_s, tiles_per_group=1, final=final)
        xs = res[0]
        if final:
            y_sample = res[1].reshape(bs, ts, d)
        cs_l.append(ext[:, -(width - 1):])
        ss_l.append(jnp.swapaxes(stf[..., :dk_c], -1, -2))

    rows_p = lambda a: jnp.transpose(a.reshape(depth, bp, h_b, hd_b, t), (0, 1, 4, 2, 3))
    rows_s_ = lambda a: a.reshape(depth, bs, ts, h_b, hd_b)
    return (y_prompt, y_sample, rows_p(kv_p[0]), rows_p(kv_p[1]), jnp.stack(cp_l), jnp.stack(sp_l),
            rows_s_(kv_s[0]), rows_s_(kv_s[1]), jnp.stack(cs_l), jnp.stack(ss_l))
```

```python
import functools
import math

import numpy as np
import jax
import jax.numpy as jnp
from jax import lax
from jax.experimental import pallas as pl
from jax.experimental.pallas import tpu as pltpu

F32 = jnp.float32
BF16 = jnp.bfloat16

MOBA_BLOCK = 256
MOBA_TOPK = 3
MAX_DISTANCE = 128
GLA_HEADS = 4
GLA_CHUNK = 16
GATE_NORM = 16.0
EPS = 1e-6

LANES = 128
SUBLANES = 8
VMEM_LIMIT = 56 * 1024 * 1024

NEG = -1e30
LOG2E = math.log2(math.e)
V_ROW_PAD = 16
TOKEN_TILE = 256
CONV_TILE = 512
SAMPLE_SEQS_PER_STEP = 8
PAGES_PER_CHUNK = 32
CONV_HALO = 32

_NT = (((1,), (1,)), ((), ()))


def _dot(a, b):
    return jnp.dot(a, b, preferred_element_type=F32)


def _dot_nt(a, b):
    return lax.dot_general(a, b, _NT, preferred_element_type=F32)


def _params(*sem):
    return pltpu.CompilerParams(dimension_semantics=sem, vmem_limit_bytes=VMEM_LIMIT)


def _resident(shape, index_map):
    return pl.BlockSpec(shape, index_map, pipeline_mode=pl.Buffered(1))


def _rms_rows(x, g):
    return (x * lax.rsqrt(jnp.mean(x * x, axis=-1, keepdims=True) + EPS)) * g


def _log_sigmoid(x):
    return jnp.minimum(x, 0.0) - jnp.log(1.0 + jnp.exp(-jnp.abs(x)))


def _top_k_mask(s, idx, axis, k, sentinel):
    sel = jnp.zeros(s.shape, dtype=jnp.bool_)
    for _ in range(k):
        mx = jnp.max(s, axis=axis, keepdims=True)
        first = jnp.min(jnp.where(s == mx, idx, sentinel), axis=axis, keepdims=True)
        pick = (idx == first) & (mx > -jnp.inf)
        sel = sel | pick
        s = jnp.where(pick, -jnp.inf, s)
    return sel


def _adaln_kernel(c_ref, w_ref, b_ref, o_ref):
    c = c_ref[...]
    s = c * jax.nn.sigmoid(c)
    o_ref[0] = _dot(s.astype(BF16), w_ref[0].astype(BF16)) + b_ref[0]


def _adaln(c_all, w_ada, b_ada):
    depth, d, n = w_ada.shape
    rows = c_all.shape[0]
    tn = 1536
    return pl.pallas_call(
        _adaln_kernel,
        grid=(depth, n // tn),
        in_specs=[
            pl.BlockSpec((rows, d), lambda l, j: (0, 0)),
            pl.BlockSpec((1, d, tn), lambda l, j: (l, 0, j)),
            pl.BlockSpec((1, 1, tn), lambda l, j: (l, 0, j)),
        ],
        out_specs=pl.BlockSpec((1, rows, tn), lambda l, j: (l, 0, j)),
        out_shape=jax.ShapeDtypeStruct((depth, rows, n), F32),
        compiler_params=_params("arbitrary", "arbitrary"),
        name="adaln",
    )(c_all, w_ada, b_ada.reshape(depth, 1, n))


def _inproj_kernel(x_ref, sc_ref, sh_ref, g_ref, w_ref, *rest, d_conv, d_b, hp, d_model,
                   q_scale, gq_scale, n_heads_b, attn_aux, n_alias, kv_layer):
    outs = rest[n_alias:]
    if attn_aux:
        (u_ref, q_ref, k_ref, v_ref, k16_ref, vt_ref, km_ref,
         gq_ref, gk_ref, gv_ref, gr_ref, lr_ref, gate_ref) = outs
    else:
        (u_ref, q_ref, k_ref, v_ref, gq_ref, gk_ref, gv_ref, gr_ref, lr_ref, gate_ref) = outs
    x = x_ref[...]
    h = _rms_rows(x, g_ref[0]) * (1.0 + sc_ref[0, 0, 0]) + sh_ref[0, 0, 0]
    hb = h.astype(BF16)

    pos = [0]

    def seg(width):
        lo = pos[0]
        pos[0] = lo + width
        return _dot(hb, w_ref[0, :, lo:lo + width])

    a = seg(d_conv)
    g = seg(d_conv)
    u_ref[...] = a * jax.nn.sigmoid(g)
    q_ref[...] = seg(d_b) * q_scale
    k = seg(d_b)
    v = seg(d_b)
    def put(ref, val):
        if n_alias:
            ref[0] = val
        else:
            for l in range(ref.shape[0]):
                ref[l] = val if l == kv_layer else jnp.zeros_like(val)

    if not attn_aux:
        put(k_ref, k)
        put(v_ref, v)
    else:
        v_t = v.T
        put(k_ref, k.T[None])
        put(v_ref, v_t[None])
        k16_ref[0] = k.astype(BF16)
        vt = v_t.astype(BF16)
        hd = d_b // n_heads_b
        tail = jnp.where(lax.broadcasted_iota(jnp.int32, (V_ROW_PAD, vt.shape[1]), 0) == 0, 1.0, 0.0).astype(BF16)
        for h in range(n_heads_b):
            vt_ref[0, h * (hd + V_ROW_PAD):h * (hd + V_ROW_PAD) + hd, :] = vt[h * hd:(h + 1) * hd, :]
            vt_ref[0, h * (hd + V_ROW_PAD) + hd:(h + 1) * (hd + V_ROW_PAD), :] = tail
        km_ref[0] = jnp.mean(k, axis=0, keepdims=True)
    gq_ref[...] = seg(hp) * gq_scale
    gk_ref[...] = seg(hp)
    gv_ref[...] = seg(hp)
    gr_ref[...] = seg(hp)
    lr_ref[...] = seg(LANES)
    for i in range(3):
        gate_ref[:, i * d_model:(i + 1) * d_model] = jax.nn.sigmoid(seg(d_model))


def _mod_spec(mods, idx, layer, tiles_per_group):
    return pl.BlockSpec((1, 1, 1) + mods.shape[3:], lambda i: (layer, idx, i // tiles_per_group, 0, 0))


def _inproj(x, mods, g1, w_cat, kv_bufs, layer, *, tm, tiles_per_group, dims, attn_aux):
    n, d = x.shape
    d_conv, d_b, hp, hd_b, dk_c = dims
    depth, _, nw = w_cat.shape
    grid = (n // tm,)
    row = lambda i: (i, 0)
    lay = lambda i: (layer, 0, 0)
    lead, l0 = (depth, 0) if kv_bufs is None else (1, layer)
    if attn_aux:
        tpg = tiles_per_group
        kv_shape = (depth, n // (tpg * tm), d_b, tpg * tm)
        kv_spec = pl.BlockSpec((lead, 1, d_b, tm), lambda i: (l0, i // tpg, 0, i % tpg))
    else:
        kv_shape, kv_spec = (depth, n, d_b), pl.BlockSpec((lead, tm, d_b), lambda i: (l0, i, 0))
    outs = [((n, d_conv), F32), ((n, d_b), F32), (kv_shape, F32), (kv_shape, F32)]
    specs = [pl.BlockSpec((tm, d_conv), row), pl.BlockSpec((tm, d_b), row), kv_spec, kv_spec]
    if attn_aux:
        assert tm == MOBA_BLOCK
        nblk = n // MOBA_BLOCK
        vt_rows = d_b + (d_b // hd_b) * V_ROW_PAD
        outs += [((nblk, MOBA_BLOCK, d_b), BF16), ((nblk, vt_rows, MOBA_BLOCK), BF16), ((nblk, 1, d_b), F32)]
        specs += [pl.BlockSpec((1, MOBA_BLOCK, d_b), lambda i: (i, 0, 0)),
                  pl.BlockSpec((1, vt_rows, MOBA_BLOCK), lambda i: (i, 0, 0)),
                  pl.BlockSpec((1, 1, d_b), lambda i: (i, 0, 0))]
    outs += [((n, hp), F32)] * 4 + [((n, LANES), F32), ((n, 3 * d), F32)]
    specs += [pl.BlockSpec((tm, hp), row)] * 4 + [pl.BlockSpec((tm, LANES), row), pl.BlockSpec((tm, 3 * d), row)]
    body = functools.partial(_inproj_kernel, d_conv=d_conv, d_b=d_b, hp=hp, d_model=d,
                             q_scale=hd_b ** -0.5, gq_scale=dk_c ** -0.5, n_heads_b=d_b // hd_b,
                             attn_aux=attn_aux, n_alias=0 if kv_bufs is None else 2, kv_layer=layer)
    in_specs = [
        pl.BlockSpec((tm, d), row),
        _mod_spec(mods, 1, layer, tiles_per_group),
        _mod_spec(mods, 0, layer, tiles_per_group),
        pl.BlockSpec((1, 1, d), lay),
        _resident((1, d, nw), lay),
    ]
    args = [x, mods, mods, g1, w_cat]
    aliases = {}
    if kv_bufs is not None:
        in_specs += [pl.BlockSpec(memory_space=pl.ANY)] * 2
        aliases = {len(args): 2, len(args) + 1: 3}
        args += list(kv_bufs)
    return pl.pallas_call(
        body,
        grid=grid,
        in_specs=in_specs,
        out_specs=specs,
        out_shape=[jax.ShapeDtypeStruct(s, t) for s, t in outs],
        input_output_aliases=aliases,
        compiler_params=_params("arbitrary"),
        name="inproj",
    )(*args)


def _conv_post(acc, lng_ref, lnb_ref):
    mu = jnp.mean(acc, axis=-1, keepdims=True)
    cen = acc - mu
    var = jnp.mean(cen * cen, axis=-1, keepdims=True)
    y = (cen * lax.rsqrt(var + EPS)) * lng_ref[0] + lnb_ref[0]
    return y * jax.nn.sigmoid(y)


def _conv_taps(ext_ref, start, rows, wdw_ref, bdw_ref, lng_ref, lnb_ref, width):
    acc = jnp.zeros((rows, wdw_ref.shape[-1]), F32) + bdw_ref[0]
    for k in range(width):
        acc = acc + ext_ref[pl.ds(start + k, rows), :] * wdw_ref[0, k:k + 1, :]
    return _conv_post(acc, lng_ref, lnb_ref)


def _conv_taps_aligned(ext_ref, start, rows, wdw_ref, bdw_ref, lng_ref, lnb_ref, width, shift_scr, out_ref):
    span = shift_scr.shape[1]
    for r in range(1, SUBLANES):
        shift_scr[r - 1] = ext_ref[pl.ds(r, span), :]
    acc = jnp.zeros((rows, wdw_ref.shape[-1]), F32) + bdw_ref[0]
    for k in range(width):
        a, r = divmod(start + k, SUBLANES)
        src = ext_ref if r == 0 else shift_scr.at[r - 1]
        acc = acc + src[pl.ds(a * SUBLANES, rows), :] * wdw_ref[0, k:k + 1, :]
    out_ref[...] = _conv_post(acc, lng_ref, lnb_ref).astype(out_ref.dtype)


def _conv_kernel(u_ref, hist_ref, wdw_ref, bdw_ref, lng_ref, lnb_ref, y_ref, tail_ref, ext_scr, shift_scr,
                 *, tt, width):
    j = pl.program_id(1)

    @pl.when(j == 0)
    def _():
        ext_scr[0:CONV_HALO, :] = hist_ref[0]

    @pl.when(j > 0)
    def _():
        ext_scr[0:CONV_HALO, :] = ext_scr[tt:tt + CONV_HALO, :]

    ext_scr[CONV_HALO:CONV_HALO + tt, :] = u_ref[0]
    start = CONV_HALO - (width - 1)
    _conv_taps_aligned(ext_scr, start, tt, wdw_ref, bdw_ref, lng_ref, lnb_ref, width, shift_scr, y_ref.at[0])

    @pl.when(j == pl.num_programs(1) - 1)
    def _():
        tail_ref[0] = ext_scr[tt:tt + CONV_HALO, :]


def _conv_prompt(u, hist, w_dw, b_dw, ln_g, ln_b, layer, *, tt):
    b, t, c = u.shape
    width = w_dw.shape[1]
    lay = lambda bi, j: (layer, 0, 0)
    return pl.pallas_call(
        functools.partial(_conv_kernel, tt=tt, width=width),
        grid=(b, t // tt),
        in_specs=[
            pl.BlockSpec((1, tt, c), lambda bi, j: (bi, j, 0)),
            pl.BlockSpec((1, CONV_HALO, c), lambda bi, j: (bi, 0, 0)),
            pl.BlockSpec((1, width, c), lay),
            pl.BlockSpec((1, 1, c), lay),
            pl.BlockSpec((1, 1, c), lay),
            pl.BlockSpec((1, 1, c), lay),
        ],
        out_specs=[pl.BlockSpec((1, tt, c), lambda bi, j: (bi, j, 0)),
                   pl.BlockSpec((1, CONV_HALO, c), lambda bi, j: (bi, 0, 0))],
        out_shape=[jax.ShapeDtypeStruct((b, t, c), BF16), jax.ShapeDtypeStruct((b, CONV_HALO, c), F32)],
        scratch_shapes=[pltpu.VMEM((CONV_HALO + tt, c), F32),
                        pltpu.VMEM((SUBLANES - 1, CONV_HALO + tt - SUBLANES, c), F32)],
        compiler_params=_params("arbitrary", "arbitrary"),
        name="conv_prompt",
    )(u, hist, w_dw, b_dw, ln_g, ln_b)


def _conv_small_kernel(ext_ref, wdw_ref, bdw_ref, lng_ref, lnb_ref, y_ref, *, rows, width):
    for i in range(ext_ref.shape[0]):
        y_ref[i] = _conv_taps(ext_ref.at[i], 0, rows, wdw_ref, bdw_ref, lng_ref, lnb_ref, width).astype(BF16)


def _conv_sample(ext, w_dw, b_dw, ln_g, ln_b, layer, *, rows):
    b, r, c = ext.shape
    width = w_dw.shape[1]
    per_step = math.gcd(b, SAMPLE_SEQS_PER_STEP)
    lay = lambda bi: (layer, 0, 0)
    return pl.pallas_call(
        functools.partial(_conv_small_kernel, rows=rows, width=width),
        grid=(b // per_step,),
        in_specs=[
            pl.BlockSpec((per_step, r, c), lambda bi: (bi, 0, 0)),
            pl.BlockSpec((1, width, c), lay),
            pl.BlockSpec((1, 1, c), lay),
            pl.BlockSpec((1, 1, c), lay),
            pl.BlockSpec((1, 1, c), lay),
        ],
        out_specs=pl.BlockSpec((per_step, rows, c), lambda bi: (bi, 0, 0)),
        out_shape=jax.ShapeDtypeStruct((b, rows, c), BF16),
        compiler_params=_params("arbitrary"),
        name="conv_sample",
    )(ext, w_dw, b_dw, ln_g, ln_b)


def _moba_prompt_kernel(q_ref, k16_ref, vt_ref, km_ref, bias_ref, o_ref,
                        qm_scr, sel_scr, m_scr, acc_scr, s2_scr, *, nb, n_heads, hd):
    i = pl.program_id(1)
    blk = MOBA_BLOCK
    q = q_ref[0]
    km = km_ref[0]
    lane = lax.broadcasted_iota(jnp.int32, (blk, LANES), 1)
    blk_idx = lax.broadcasted_iota(jnp.int32, (nb, blk), 0)
    heads_per_slab = LANES // hd

    km16 = km.astype(BF16)
    for h in range(n_heads):
        p, w = divmod(h, heads_per_slab)
        qm = jnp.where((lane // hd) == w, q[:, p * LANES:(p + 1) * LANES], 0.0)
        qm_scr[h] = (qm * LOG2E).astype(BF16)
        s = _dot_nt(km16[:, p * LANES:(p + 1) * LANES], qm.astype(BF16))
        s = jnp.where(blk_idx < i, s, -jnp.inf)
        sel = _top_k_mask(s, blk_idx, 0, MOBA_TOPK, nb)
        sel_scr[h] = jnp.where(sel, 1.0, 0.0)

    hv = vt_ref.shape[1] // n_heads

    def block(n, kind):
        kb = k16_ref[n]
        vb = vt_ref[n]

        for h in range(n_heads):
            p = h // heads_per_slab
            s2_scr[h, 0:blk, :] =_dot_nt(kb[:, p * LANES:(p + 1) * LANES], qm_scr[h])
        for h in range(n_heads):
            s = s2_scr[h, 0:blk, :]
            rows = slice(h * hv, (h + 1) * hv)
            if kind == "own":
                s = s + bias_ref[h, 0]
                m = jnp.max(s, axis=0, keepdims=True)
                m_scr[h:h + 1, :] = m
                acc_scr[rows, :] = _dot(vb[rows, :], jnp.exp2(s - m).astype(BF16))
            else:
                picked = sel_scr[h, pl.ds(n, 1), :] > 0.5
                m_old = m_scr[h:h + 1, :]
                m = jnp.where(picked, jnp.maximum(m_old, jnp.max(s, axis=0, keepdims=True)), m_old)
                alpha = jnp.exp2(m_old - m)
                pr = jnp.exp2(s - jnp.where(picked, m, -NEG))
                m_scr[h:h + 1, :] = m
                acc_scr[rows, :] = alpha * acc_scr[rows, :] + _dot(vb[rows, :], pr.astype(BF16))

    def pair_logits(n, buf):
        kb = k16_ref[pl.ds(n, 2)].reshape(2 * blk, k16_ref.shape[-1])
        for h in range(n_heads):
            p = h // heads_per_slab
            buf[h] = _dot_nt(kb[:, p * LANES:(p + 1) * LANES], qm_scr[h])

    @pl.when(i == 0)
    def _():
        block(0, "own")

    @pl.when(i >= 1)
    def _():
        n = i - 1
        pair_logits(n, s2_scr)
        vb = jnp.concatenate([vt_ref[n], vt_ref[n + 1]], axis=1)
        for h in range(n_heads):
            rows = slice(h * hv, (h + 1) * hv)
            s_a = s2_scr[h, 0:blk, :] + bias_ref[h, 1]
            s_b = s2_scr[h, blk:2 * blk, :] + bias_ref[h, 0]
            pick_a = sel_scr[h, pl.ds(n, 1), :] > 0.5
            m = jnp.maximum(jnp.where(pick_a, jnp.max(s_a, axis=0, keepdims=True), NEG),
                            jnp.max(s_b, axis=0, keepdims=True))
            pr = jnp.concatenate([jnp.exp2(s_a - jnp.where(pick_a, m, -NEG)).astype(BF16),
                                  jnp.exp2(s_b - m).astype(BF16)], axis=0)
            m_scr[h:h + 1, :] = m
            acc_scr[rows, :] = _dot(vb[rows, :], pr)

    def pair_update(n, buf):
        vb = jnp.concatenate([vt_ref[n], vt_ref[n + 1]], axis=1)
        for h in range(n_heads):
            rows = slice(h * hv, (h + 1) * hv)
            s_a = buf[h, 0:blk, :]
            s_b = buf[h, blk:2 * blk, :]
            pick_a = sel_scr[h, pl.ds(n, 1), :] > 0.5
            pick_b = sel_scr[h, pl.ds(n + 1, 1), :] > 0.5
            m_old = m_scr[h:h + 1, :]
            m = jnp.maximum(m_old, jnp.maximum(
                jnp.where(pick_a, jnp.max(s_a, axis=0, keepdims=True), NEG),
                jnp.where(pick_b, jnp.max(s_b, axis=0, keepdims=True), NEG)))
            alpha = jnp.exp2(m_old - m)
            pr = jnp.concatenate([jnp.exp2(s_a - jnp.where(pick_a, m, -NEG)).astype(BF16),
                                  jnp.exp2(s_b - jnp.where(pick_b, m, -NEG)).astype(BF16)], axis=0)
            m_scr[h:h + 1, :] = m
            acc_scr[rows, :] = alpha * acc_scr[rows, :] + _dot(vb[rows, :], pr)

    n_far = jnp.maximum(i - 1, 0)

    def far(t, carry):
        pair_logits(2 * t, s2_scr)
        pair_update(2 * t, s2_scr)
        return carry

    lax.fori_loop(0, n_far // 2, far, 0)

    @pl.when(n_far % 2 == 1)
    def _():
        block(n_far - 1, "far")

    out = [acc_scr[h * hv:h * hv + hd, :] / acc_scr[h * hv + hd:h * hv + hd + 1, :] for h in range(n_heads)]
    o_ref[0] = jnp.concatenate(out, axis=0).T.astype(BF16)


def _moba_prompt(q, k16, vt, km, bias_t, *, batch, n_heads, hd):
    n, d_b = q.shape
    t = n // batch
    nb = t // MOBA_BLOCK
    blk = MOBA_BLOCK
    body = functools.partial(_moba_prompt_kernel, nb=nb, n_heads=n_heads, hd=hd)
    return pl.pallas_call(
        body,
        grid=(batch, nb),
        in_specs=[
            pl.BlockSpec((1, blk, d_b), lambda b, i: (b * nb + i, 0, 0)),
            pl.BlockSpec((nb, blk, d_b), lambda b, i: (b, 0, 0)),
            pl.BlockSpec((nb, vt.shape[1], blk), lambda b, i: (b, 0, 0)),
            pl.BlockSpec((1, nb, d_b), lambda b, i: (b, 0, 0)),
            _resident((n_heads, 2, blk, blk), lambda b, i: (0, 0, 0, 0)),
        ],
        out_specs=pl.BlockSpec((1, blk, d_b), lambda b, i: (b * nb + i, 0, 0)),
        out_shape=jax.ShapeDtypeStruct((n // blk, blk, d_b), BF16),
        scratch_shapes=[
            pltpu.VMEM((n_heads, blk, LANES), BF16),
            pltpu.VMEM((n_heads, nb, blk), F32),
            pltpu.VMEM((n_heads, blk), F32),
            pltpu.VMEM((vt.shape[1], blk), F32),
            pltpu.VMEM((n_heads, 2 * blk, blk), F32),
        ],
        compiler_params=_params("arbitrary", "arbitrary"),
        name="moba_prompt",
    )(q.reshape(n // blk, blk, d_b), k16, vt, km.reshape(batch, nb, d_b), bias_t)


def _moba_sample_kernel(pt_ref, q_ref, kn_ref, vn_ref, bown_ref, btail_ref, kt_hbm, vt_hbm, o_ref,
                        s_scr, kv_scr, kbuf, vbuf, ksem, vsem,
                        *, layer, n_chunks, n_heads, t_new, page, ppc):
    b = pl.program_id(0)
    rows, d_b = q_ref.shape[1:]
    hd = d_b // n_heads
    blocks_per_chunk = ppc * page // MOBA_BLOCK
    pages_per_block = MOBA_BLOCK // page
    nbp = n_chunks * blocks_per_chunk

    def page_copy(hbm, buf, sem, seq, chunk, p):
        slot = chunk % 2
        return pltpu.make_async_copy(hbm.at[layer, pt_ref[seq, chunk * ppc + p]], buf.at[slot, p], sem.at[slot])

    def start_chunk(hbm, buf, sem, seq, chunk):
        for p in range(ppc):
            page_copy(hbm, buf, sem, seq, chunk, p).start()

    def wait_chunk(hbm, buf, sem, seq, chunk):
        for p in range(ppc):
            page_copy(hbm, buf, sem, seq, chunk, p).wait()

    @pl.when(b == 0)
    def _():
        start_chunk(kt_hbm, kbuf, ksem, b, 0)

    row_i = lax.broadcasted_iota(jnp.int32, (rows, d_b), 0)
    lane_i = lax.broadcasted_iota(jnp.int32, (rows, d_b), 1)
    head_mask = (lane_i // hd) == (row_i % n_heads)
    blk_lane = lax.broadcasted_iota(jnp.int32, (rows, LANES), 1)
    qb16 = jnp.where(head_mask, q_ref[0], 0.0).astype(BF16)

    bsum = jnp.zeros((rows, LANES), F32)
    for c in range(n_chunks):
        if c + 1 < n_chunks:
            start_chunk(kt_hbm, kbuf, ksem, b, c + 1)
        else:
            start_chunk(vt_hbm, vbuf, vsem, b, 0)
        wait_chunk(kt_hbm, kbuf, ksem, b, c)
        for p in range(ppc):
            s = _dot(qb16, kbuf[c % 2, p].astype(BF16))
            s_scr[c, :, p * page:(p + 1) * page] = s
            ps = jnp.sum(s, axis=-1, keepdims=True)
            psum = ps if p % pages_per_block == 0 else psum + ps
            if p % pages_per_block == pages_per_block - 1:
                bsum = jnp.where(blk_lane == c * blocks_per_chunk + p // pages_per_block, psum, bsum)

    bs = jnp.where(blk_lane < nbp, bsum, -jnp.inf)
    sel = _top_k_mask(bs, blk_lane, 1, MOBA_TOPK, LANES)
    blk_bias = jnp.where(sel, 0.0, NEG)

    kv_scr[...] = jnp.zeros(kv_scr.shape, F32)
    kv_scr[0:kn_ref.shape[1], :] = kn_ref[0]
    s_own = _dot_nt(qb16, kv_scr[...].astype(BF16)) + bown_ref[...]
    m = jnp.max(s_own, axis=-1, keepdims=True)
    for blk in range(nbp):
        cc, off = divmod(blk, blocks_per_chunk)
        cols = slice(off * MOBA_BLOCK, (off + 1) * MOBA_BLOCK)
        s = s_scr[cc, :, cols] + blk_bias[:, blk:blk + 1]
        if blk == nbp - 1:
            s = s + btail_ref[...]
        s_scr[cc, :, cols] = s
        m = jnp.maximum(m, jnp.max(s, axis=-1, keepdims=True))

    p_own = jnp.exp(s_own - m)
    l = jnp.sum(p_own, axis=-1, keepdims=True)
    for cc in range(n_chunks):
        pr = jnp.exp(s_scr[cc] - m)
        l = l + jnp.sum(pr, axis=-1, keepdims=True)
        s_scr[cc] = pr
    kv_scr[0:vn_ref.shape[1], :] = vn_ref[0]
    acc = _dot(p_own.astype(BF16), kv_scr[...].astype(BF16))

    for c in range(n_chunks):
        if c + 1 < n_chunks:
            start_chunk(vt_hbm, vbuf, vsem, b, c + 1)
        else:
            @pl.when(b + 1 < pl.num_programs(0))
            def _():
                start_chunk(kt_hbm, kbuf, ksem, b + 1, 0)
        wait_chunk(vt_hbm, vbuf, vsem, b, c)
        for p in range(ppc):
            pr = s_scr[c, :, p * page:(p + 1) * page]
            acc = acc + _dot_nt(pr.astype(BF16), vbuf[c % 2, p].astype(BF16))

    o = jnp.where(head_mask, acc / l, 0.0)
    o_ref[0] = jnp.sum(o.reshape(t_new, n_heads, d_b), axis=1).astype(BF16)


def _moba_sample(page_table, q_rep, k_new, v_new, bias_own, bias_tail, cache_kt, cache_vt, layer,
                 *, n_heads, t_new):
    b, rows, d_b = q_rep.shape
    page = cache_kt.shape[-1]
    n_pages = page_table.shape[1]
    ppc = min(PAGES_PER_CHUNK, n_pages)
    n_chunks = n_pages // ppc
    assert n_pages % ppc == 0 and (ppc * page) % MOBA_BLOCK == 0 and MOBA_BLOCK % page == 0
    chunk = ppc * page

    per_b = lambda bi, pt: (bi, 0, 0)
    const2 = lambda bi, pt: (0, 0)
    in_specs = [
        pl.BlockSpec((1, rows, d_b), per_b),
        pl.BlockSpec((1, k_new.shape[1], d_b), per_b),
        pl.BlockSpec((1, v_new.shape[1], d_b), per_b),
        pl.BlockSpec(bias_own.shape, const2),
        pl.BlockSpec(bias_tail.shape, const2),
        pl.BlockSpec(memory_space=pl.ANY),
        pl.BlockSpec(memory_space=pl.ANY),
    ]
    body = functools.partial(_moba_sample_kernel, layer=layer, n_chunks=n_chunks, n_heads=n_heads,
                             t_new=t_new, page=page, ppc=ppc)
    grid_spec = pltpu.PrefetchScalarGridSpec(
        num_scalar_prefetch=1,
        grid=(b,),
        in_specs=in_specs,
        out_specs=pl.BlockSpec((1, t_new, d_b), per_b),
        scratch_shapes=[
            pltpu.VMEM((n_chunks, rows, chunk), F32),
            pltpu.VMEM((LANES, d_b), F32),
            pltpu.VMEM((2, ppc, d_b, page), F32),
            pltpu.VMEM((2, ppc, d_b, page), F32),
            pltpu.SemaphoreType.DMA((2,)),
            pltpu.SemaphoreType.DMA((2,)),
        ],
    )
    return pl.pallas_call(
        body,
        grid_spec=grid_spec,
        out_shape=jax.ShapeDtypeStruct((b, t_new, d_b), BF16),
        compiler_params=_params("arbitrary"),
        name="moba_sample",
    )(page_table, q_rep, k_new, v_new, bias_own, bias_tail, cache_kt, cache_vt)


def _gla_kernel(gq_ref, gk_ref, gv_ref, gr_ref, lr_ref, wa_ref, ba_ref, gg_ref, st0_ref,
                o_ref, stf_ref, *scratch, n_seq, **static):
    for i in range(n_seq):
        one = lambda ref: ref.at[pl.ds(i, 1)]
        _gla_sequence(one(gq_ref), one(gk_ref), one(gv_ref), one(gr_ref), one(lr_ref), wa_ref, ba_ref, gg_ref,
                      one(st0_ref), one(o_ref), one(stf_ref), *scratch, **static)


def _gla_sequence(gq_ref, gk_ref, gv_ref, gr_ref, lr_ref, wa_ref, ba_ref, gg_ref, st0_ref,
                  o_ref, stf_ref, st_scr, bc_scr, bl_scr, qd_scr, kd_scr, o_scr,
                  *, tt, valid, n_heads):
    j = pl.program_id(1)
    c_sz = GLA_CHUNK
    hp = n_heads * LANES

    @pl.when(j == 0)
    def _():
        st_scr[...] = st0_ref[0]

    x = _dot(lr_ref[0].astype(BF16), wa_ref[0]) + ba_ref[0]
    la = _log_sigmoid(x) * (1.0 / GATE_NORM)
    if valid < tt:
        la = jnp.where(lax.broadcasted_iota(jnp.int32, (tt, hp), 0) < valid, la, 0.0)
    r_i = lax.broadcasted_iota(jnp.int32, (tt, tt), 0)
    c_i = lax.broadcasted_iota(jnp.int32, (tt, tt), 1)
    low = jnp.where(((r_i // c_sz) == (c_i // c_sz)) & (c_i <= r_i), 1.0, 0.0).astype(BF16)
    la_hi = la.astype(BF16)
    la_lo = (la - la_hi.astype(F32)).astype(BF16)
    bc = _dot(low, la_hi) + _dot(low, la_lo)
    bc_scr[...] = bc
    n_ch = tt // c_sz
    in_ch = (lax.broadcasted_iota(jnp.int32, (n_ch, tt), 1) // c_sz
             == lax.broadcasted_iota(jnp.int32, (n_ch, tt), 0))
    tot = jnp.where(in_ch, 1.0, 0.0).astype(BF16)
    ends = _dot(tot, la_hi) + _dot(tot, la_lo)
    bl_scr[...] = ends
    bl = jnp.broadcast_to(ends[:, None, :], (n_ch, c_sz, hp)).reshape(tt, hp)
    qd_scr[...] = (gq_ref[0] * jnp.exp(bc)).astype(BF16)
    kd_scr[...] = (gk_ref[0] * jnp.exp(bl - bc)).astype(BF16)

    half = c_sz // 2
    t_half = lax.broadcasted_iota(jnp.int32, (half, LANES), 0)

    def chunk(c, carry):
        rows = pl.ds(c * c_sz, c_sz)
        bc_c = bc_scr[rows, :]
        q_c = gq_ref[0, rows, :]
        k_c = gk_ref[0, rows, :]
        v_c = gv_ref[0, rows, :]
        decay = jnp.exp(bl_scr[pl.ds(c, 1), :])
        for h in range(n_heads):
            hs = slice(h * LANES, (h + 1) * LANES)
            st = st_scr[h]
            o_h = _dot_nt(qd_scr[rows, hs], st.astype(BF16))
            bch = bc_c[:, hs]
            qh = q_c[:, hs]
            parts = [o_h[0:half], o_h[half:c_sz]]
            for s in range(c_sz):
                row_s = pl.ds(c * c_sz + s, 1)
                b_s = jnp.broadcast_to(bc_scr[row_s, hs], (half, LANES))
                k_s = jnp.broadcast_to(gk_ref[0, row_s, hs], (half, LANES))
                v_s = jnp.broadcast_to(gv_ref[0, row_s, hs], (half, LANES))
                for hi in range(2):
                    r0 = hi * half
                    if s >= r0 + half:
                        continue
                    diff = bch[r0:r0 + half] - b_s
                    if s > r0:
                        diff = jnp.where(t_half + r0 >= s, diff, -jnp.inf)
                    a_s = jnp.sum(qh[r0:r0 + half] * k_s * jnp.exp(diff), axis=-1, keepdims=True)
                    parts[hi] = parts[hi] + a_s * v_s
            o_scr[rows, hs] = jnp.concatenate(parts, axis=0)
            kvt = lax.dot_general(v_c[:, hs].astype(BF16), kd_scr[rows, hs], (((0,), (0,)), ((), ())),
                                  preferred_element_type=F32)
            st_scr[h] = decay[:, hs] * st + kvt
        return carry

    for c in range((valid + c_sz - 1) // c_sz):
        chunk(c, 0)

    o = o_scr[...]
    r = gr_ref[0]
    for h in range(n_heads):
        hs = slice(h * LANES, (h + 1) * LANES)
        oh = o[:, hs]
        oh = oh * lax.rsqrt(jnp.mean(oh * oh, axis=-1, keepdims=True) + EPS)
        rh = r[:, hs]
        o_ref[0, :, hs] = ((oh * gg_ref[0, :, hs]) * (rh * jax.nn.sigmoid(rh))).astype(BF16)

    @pl.when(j == pl.num_programs(1) - 1)
    def _():
        stf_ref[0] = st_scr[...]


def _gla(gq, gk, gv, gr, lr, wa, ba, gg, st0, layer, *, tt, valid):
    b, t, hp = gq.shape
    n_heads = hp // LANES
    n_seq = math.gcd(b, SAMPLE_SEQS_PER_STEP) if t == tt else 1
    lay = lambda bi, j: (layer, 0, 0)
    tile = lambda bi, j: (bi, j, 0)
    per_b = lambda bi, j: (bi, 0, 0, 0)
    body = functools.partial(_gla_kernel, n_seq=n_seq, tt=tt, valid=valid, n_heads=n_heads)
    return pl.pallas_call(
        body,
        grid=(b // n_seq, t // tt),
        in_specs=[pl.BlockSpec((n_seq, tt, hp), tile)] * 4 + [
            pl.BlockSpec((n_seq, tt, LANES), tile),
            pl.BlockSpec((1, LANES, hp), lay),
            pl.BlockSpec((1, 1, hp), lay),
            pl.BlockSpec((1, 1, hp), lay),
            pl.BlockSpec((n_seq, n_heads, LANES, LANES), per_b),
        ],
        out_specs=[pl.BlockSpec((n_seq, tt, hp), tile), pl.BlockSpec((n_seq, n_heads, LANES, LANES), per_b)],
        out_shape=[jax.ShapeDtypeStruct((b, t, hp), BF16),
                   jax.ShapeDtypeStruct((b, n_heads, LANES, LANES), F32)],
        scratch_shapes=[
            pltpu.VMEM((n_heads, LANES, LANES), F32),
            pltpu.VMEM((tt, hp), F32),
            pltpu.VMEM((tt // GLA_CHUNK, hp), F32),
            pltpu.VMEM((tt, hp), BF16),
            pltpu.VMEM((tt, hp), BF16),
            pltpu.VMEM((tt, hp), F32),
        ],
        compiler_params=_params("arbitrary", "arbitrary"),
        name="gla",
    )(gq, gk, gv, gr, lr, wa, ba, gg, st0)


def _merge_mlp_kernel(x_ref, ya_ref, ob_ref, oc_ref, gate_ref, gt1_ref, sc2_ref, sh2_ref, gt2_ref,
                      g2_ref, gf_ref, wa_ref, wb_ref, wc_ref, wo_ref, wup_ref, wdn_ref, *outs, final):
    d = x_ref.shape[-1]
    m = (gate_ref[:, 0:d] * _dot(ya_ref[...], wa_ref[0])
         + gate_ref[:, d:2 * d] * _dot(ob_ref[...], wb_ref[0])
         + gate_ref[:, 2 * d:3 * d] * _dot(oc_ref[...], wc_ref[0]))
    x1 = x_ref[...] + gt1_ref[0, 0, 0] * _dot(m.astype(BF16), wo_ref[0])
    h2 = _rms_rows(x1, g2_ref[0]) * (1.0 + sc2_ref[0, 0, 0]) + sh2_ref[0, 0, 0]
    up = jnp.maximum(_dot(h2.astype(BF16), wup_ref[0]), 0.0)
    x2 = x1 + gt2_ref[0, 0, 0] * _dot((up * up).astype(BF16), wdn_ref[0])
    outs[0][...] = x2
    if final:
        outs[1][...] = _rms_rows(x2, gf_ref[...])


def _merge_mlp(x, ya, ob, oc, gates, mods, g2, gf, wa, wb, wc, wo, wup, wdn, layer,
               *, tm, tiles_per_group, final):
    n, d = x.shape
    c = ya.shape[-1]
    dff = wup.shape[-1]
    row = lambda i: (i, 0)
    lay = lambda i: (layer, 0, 0)
    n_out = 2 if final else 1
    return pl.pallas_call(
        functools.partial(_merge_mlp_kernel, final=final),
        grid=(n // tm,),
        in_specs=[pl.BlockSpec((tm, d), row)] + [pl.BlockSpec((tm, c), row)] * 3 + [
            pl.BlockSpec((tm, 3 * d), row),
            _mod_spec(mods, 2, layer, tiles_per_group), _mod_spec(mods, 4, layer, tiles_per_group),
            _mod_spec(mods, 3, layer, tiles_per_group), _mod_spec(mods, 5, layer, tiles_per_group),
            pl.BlockSpec((1, 1, d), lay),
            pl.BlockSpec((1, d), lambda i: (0, 0)),
            _resident((1, c, d), lay), _resident((1, c, d), lay), _resident((1, c, d), lay),
            _resident((1, d, d), lay), _resident((1, d, dff), lay), _resident((1, dff, d), lay),
        ],
        out_specs=[pl.BlockSpec((tm, d), row)] * n_out,
        out_shape=[jax.ShapeDtypeStruct((n, d), F32)] * n_out,
        compiler_params=_params("arbitrary"),
        name="merge_mlp",
    )(x, ya, ob, oc, gates, mods, mods, mods, mods, g2, gf, wa, wb, wc, wo, wup, wdn)


def _t5_bucket_table(n_buckets, max_dist):
    dist = np.arange(max_dist + 1)
    max_exact = n_buckets // 2
    d = np.maximum(dist, 1).astype(np.float32)
    large = max_exact + (np.log(d / max_exact) / math.log(MAX_DISTANCE / max_exact)
                         * (n_buckets - max_exact)).astype(np.int32)
    large = np.minimum(large, n_buckets - 1)
    return np.where(dist < max_exact, dist, large).astype(np.int32)


def _pad_heads(w, n_heads):
    k = w.shape[-1] // n_heads
    w = w.reshape(w.shape[:-1] + (n_heads, k))
    w = jnp.pad(w, [(0, 0)] * (w.ndim - 1) + [(0, LANES - k)])
    return w.reshape(w.shape[:-2] + (n_heads * LANES,))


def kernel(x_prompt, x_sample, c_prompt, c_sample, cache_k, cache_v, page_table, state_conv, state_gla,
           w_ada, b_ada, g_norm1, w_in, w_dw, b_dw, ln_g, ln_b, w_pw2, w_pb, rel_bias, w_a2, b_a, g_gla,
           w_pc, w_o, g_norm2, w_up, w_down, g_final):
    bp, t, d = x_prompt.shape
    bs, ts, _ = x_sample.shape
    depth = w_in.shape[0]
    _, n_pool, page, h_b, hd_b = cache_k.shape
    d_b = h_b * hd_b
    d_conv = w_dw.shape[-1]
    width = w_dw.shape[1]
    rank = w_a2.shape[1]
    h_c = GLA_HEADS
    dk_c = w_a2.shape[-1] // h_c
    dv_c = state_gla.shape[-1]
    hp = h_c * LANES
    n_buckets = rel_bias.shape[0]
    n_pages = page_table.shape[1]
    past_len = n_pages * page
    assert dv_c == LANES and h_c * dv_c == hp and width - 1 <= CONV_HALO

    sizes = (2 * d_conv, d_b, d_b, d_b, h_c * dk_c, h_c * dk_c, h_c * dv_c, h_c * dv_c, rank, d, d, d)
    offs = np.cumsum((0,) + sizes)
    part = [w_in[:, :, offs[i]:offs[i + 1]] for i in range(len(sizes))]
    w_cat = jnp.concatenate([
        part[0], part[1], part[2], part[3],
        _pad_heads(part[4], h_c), _pad_heads(part[5], h_c), part[6], part[7],
        jnp.pad(part[8], ((0, 0), (0, 0), (0, LANES - rank))),
        part[9], part[10], part[11]], axis=-1).astype(BF16)
    wa2 = jnp.pad(_pad_heads(w_a2, h_c), ((0, 0), (0, LANES - rank), (0, 0))).astype(BF16)
    ba2 = _pad_heads(b_a, h_c).reshape(depth, 1, hp)
    gg = g_gla.reshape(depth, 1, hp)
    wpw2, wpb, wpc, wo, wup, wdn = (w.astype(BF16) for w in (w_pw2, w_pb, w_pc, w_o, w_up, w_down))
    g1 = g_norm1.reshape(depth, 1, d)
    g2 = g_norm2.reshape(depth, 1, d)
    gf = g_final.reshape(1, d)
    bdw = b_dw.reshape(depth, 1, d_conv)
    lng = ln_g.reshape(depth, 1, d_conv)
    lnb = ln_b.reshape(depth, 1, d_conv)

    n_c = bp + bs
    rows_c = -(-n_c // 8) * 8
    c_all = jnp.pad(jnp.concatenate([c_prompt, c_sample], axis=0), ((0, rows_c - n_c), (0, 0)))
    mods = _adaln(c_all, w_ada, b_ada).reshape(depth, rows_c, 6, d)

    blk = MOBA_BLOCK
    assert ts <= blk and past_len % blk == 0
    bucket = _t5_bucket_table(n_buckets, 2 * blk - 1)
    period = 2 * blk + LANES
    onehot = np.zeros((period, n_buckets), np.float32)
    onehot[np.arange(2 * blk), bucket] = 1.0
    by_dist = jnp.dot(onehot, rel_bias - rel_bias[n_buckets - 1:n_buckets, :],
                      precision=lax.Precision.HIGHEST).T
    toep = jnp.tile(by_dist, (1, blk))[:, :blk * (period - 1)].reshape(h_b, blk, period - 1)
    causal = np.arange(blk)[:, None] <= np.arange(blk)[None, :]
    bias_t = jnp.stack([jnp.where(causal, toep[:, :, :blk] * LOG2E, NEG), toep[:, :, blk:2 * blk] * LOG2E],
                       axis=1)
    rows_s = ts * h_b
    own = jnp.transpose(toep[:, :ts, :ts], (2, 0, 1)).reshape(rows_s, ts)
    own_ok = np.repeat(np.arange(ts), h_b)[:, None] >= np.arange(LANES)[None, :]
    bias_own = jnp.where(own_ok, jnp.pad(own, ((0, 0), (0, LANES - ts))), NEG)
    bias_tail = jnp.transpose(toep[:, :, blk:blk + ts], (2, 0, 1)).reshape(rows_s, blk)
    assert past_len // blk <= LANES
    cache_kt = jnp.transpose(cache_k, (0, 1, 3, 4, 2)).reshape(depth, n_pool, d_b, page)
    cache_vt = jnp.transpose(cache_v, (0, 1, 3, 4, 2)).reshape(depth, n_pool, d_b, page)

    xp = x_prompt.reshape(bp * t, d)
    xs = x_sample.reshape(bs * ts, d)
    n_s = bs * ts
    tm = TOKEN_TILE
    tpb = t // tm
    dims = (d_conv, d_b, hp, hd_b, dk_c)
    ts_pad = 8
    gla_rows_s = GLA_CHUNK

    cp_l, sp_l, cs_l, ss_l = ([] for _ in range(4))
    y_prompt = y_sample = None
    zeros_hist = jnp.zeros((bp, CONV_HALO, d_conv), F32)
    zeros_state = jnp.zeros((bp, h_c, LANES, LANES), F32)
    mods = jnp.transpose(mods, (0, 2, 1, 3))
    mods_p = mods[:, :, :bp].reshape(depth, 6, bp, 1, d)
    mods_s = jnp.repeat(mods[:, :, bp:n_c], ts, axis=2).reshape(depth, 6, 1, n_s, d)
    kv_p = kv_s = None
    for l in range(depth):
        final = l == depth - 1

        (u, qb, kb, vb, k16, vt, km, gq, gk, gv, gr, lr, gates) = _inproj(
            xp, mods_p, g1, w_cat, kv_p, l, tm=tm, tiles_per_group=tpb, dims=dims, attn_aux=True)
        kv_p = (kb, vb)
        ya, tail = _conv_prompt(u.reshape(bp, t, d_conv), zeros_hist, w_dw, bdw, lng, lnb, l,
                                tt=math.gcd(t, CONV_TILE))
        ob = _moba_prompt(qb, k16, vt, km, bias_t, batch=bp, n_heads=h_b, hd=hd_b)
        oc, stf = _gla(*(a.reshape(bp, t, -1) for a in (gq, gk, gv, gr, lr)), wa2, ba2, gg, zeros_state, l,
                       tt=tm, valid=tm)
        res = _merge_mlp(xp, ya.reshape(bp * t, d_conv), ob.reshape(bp * t, d_b), oc.reshape(bp * t, hp),
                         gates, mods_p, g2, gf, wpw2, wpb, wpc, wo, wup, wdn, l,
                         tm=tm, tiles_per_group=tpb, final=final)
        xp = res[0]
        if final:
            y_prompt = res[1].reshape(bp, t, d)
        cp_l.append(tail[:, CONV_HALO - (width - 1):])
        sp_l.append(jnp.swapaxes(stf[..., :dk_c], -1, -2))

        (u, qb, kb, vb, gq, gk, gv, gr, lr, gates) = _inproj(
            xs, mods_s, g1, w_cat, kv_s, l, tm=n_s, tiles_per_group=1, dims=dims, attn_aux=False)
        kv_s = (kb, vb)
        kb, vb = kb[l], vb[l]
        ext = jnp.concatenate([state_conv[l], u.reshape(bs, ts, d_conv)], axis=1)
        ext_p = jnp.pad(ext, ((0, 0), (0, ts_pad + width - 1 - ext.shape[1] + (-(ts_pad + width - 1)) % 8), (0, 0)))
        ya = _conv_sample(ext_p, w_dw, bdw, lng, lnb, l, rows=ts_pad)[:, :ts]
        q_rep = jnp.repeat(qb.reshape(bs, ts, d_b), h_b, axis=1)
        pad_new = lambda a: jnp.pad(a.reshape(bs, ts, d_b), ((0, 0), (0, ts_pad - ts), (0, 0)))
        ob = _moba_sample(page_table, q_rep, pad_new(kb), pad_new(vb), bias_own, bias_tail,
                          cache_kt, cache_vt, l, n_heads=h_b, t_new=ts)
        pad_t = lambda a: jnp.pad(a.reshape(bs, ts, -1), ((0, 0), (0, gla_rows_s - ts), (0, 0)))
        st0 = jnp.pad(jnp.swapaxes(state_gla[l], -1, -2), ((0, 0), (0, 0), (0, 0), (0, LANES - dk_c)))
        oc, stf = _gla(pad_t(gq), pad_t(gk), pad_t(gv), pad_t(gr), pad_t(lr), wa2, ba2, gg, st0, l,
                       tt=gla_rows_s, valid=ts)
        res = _merge_mlp(xs, ya.reshape(n_s, d_conv), ob.reshape(n_s, d_b), oc[:, :ts].reshape(n_s, hp),
                         gates, mods_s, g2, gf, wpw2, wpb, wpc, wo, wup, wdn, l,
                         tm=n_s, tiles_per_group=1, final=final)
        xs = res[0]
        if final:
            y_sample = res[1].reshape(bs, ts, d)
        cs_l.append(ext[:, -(width - 1):])
        ss_l.append(jnp.swapaxes(stf[..., :dk_c], -1, -2))

    rows_p = lambda a: jnp.transpose(a.reshape(depth, bp, h_b, hd_b, t), (0, 1, 4, 2, 3))
    rows_s_ = lambda a: a.reshape(depth, bs, ts, h_b, hd_b)
    return (y_prompt, y_sample, rows_p(kv_p[0]), rows_p(kv_p[1]), jnp.stack(cp_l), jnp.stack(sp_l),
            rows_s_(kv_s[0]), rows_s_(kv_s[1]), jnp.stack(cs_l), jnp.stack(ss_l))
```
